```python
import jax, jax.numpy as jnp
from jax import lax
import numpy as np

D_MODEL = 4096
BATCH = 8
SEQ = 2048
DEPTH = 4

GRID_W = 64
CTX_LEN = 256
N_MIXERS = 3
BLOCK = 128
RMS_EPS = 1e-6
ADA_RANK = 256
N_MOD = 6

N_HEADS = 32
N_KV_HEADS = 8
HEAD_DIM = D_MODEL // N_HEADS
KV_GROUP = N_HEADS // N_KV_HEADS
ROPE_THETA = 10000.0
WINDOW = 128

M_HEADS = 8
M_V_DIM = D_MODEL // M_HEADS
M_QK_DIM = M_V_DIM // 2
M_CHUNK = 128
GATE_CAP = 15.0

N_EXPERTS = 64
TOP_K = 8
N_GROUPS = 8
TOPK_GROUPS = 4
D_EXPERT = 192
D_SHARED = 192
ROUTED_SCALE = 2.5

kernel_name = 'hybrid_interleaved_flow_trunk'

F32 = jnp.float32


def rmsnorm(x, g):
    xf = x.astype(F32)
    y = xf * lax.rsqrt(jnp.mean(xf * xf, axis=-1, keepdims=True) + RMS_EPS)
    return (y * g.astype(F32)).astype(x.dtype)


def modulate(x, shift, scale):
    return x * (1 + scale) + shift


def ada_modulation(cond, down, up, bias):
    m = jnp.dot(jnp.dot(jax.nn.silu(cond), down), up) + bias
    return jnp.split(m, N_MOD, axis=-1)


def axial_rope(n_tokens):
    rows = n_tokens // GRID_W
    pos = jnp.arange(rows * GRID_W)
    row = (pos // GRID_W).astype(F32)
    col = (pos % GRID_W).astype(F32)
    n_freq = HEAD_DIM // 4
    inv_freq = ROPE_THETA ** (-jnp.arange(n_freq, dtype=F32) / n_freq)
    ang = jnp.concatenate([row[:, None] * inv_freq, col[:, None] * inv_freq], axis=-1)
    return jnp.cos(ang), jnp.sin(ang)


def apply_rope(x, cos, sin):
    xf = x.astype(F32)
    x1, x2 = jnp.split(xf, 2, axis=-1)
    c = cos[None, :, None, :]
    s = sin[None, :, None, :]
    return jnp.concatenate([x1 * c - x2 * s, x1 * s + x2 * c], axis=-1).astype(x.dtype)


def project_qkv(x, w_qkv):
    bsz, t, _ = x.shape
    q, k, v = jnp.split(jnp.dot(x, w_qkv), [N_HEADS * HEAD_DIM, (N_HEADS + N_KV_HEADS) * HEAD_DIM], axis=-1)
    return (q.reshape(bsz, t, N_HEADS, HEAD_DIM),
            k.reshape(bsz, t, N_KV_HEADS, HEAD_DIM),
            v.reshape(bsz, t, N_KV_HEADS, HEAD_DIM))


def group_queries(q):
    bsz, t = q.shape[:2]
    return q.reshape(bsz, t, N_KV_HEADS, KV_GROUP, HEAD_DIM)


def gqa_scores(q, k):
    return jnp.einsum('bqkgd,bskd->bkgqs', q, k, preferred_element_type=F32) * (HEAD_DIM ** -0.5)


def gqa_values(p, v):
    return jnp.einsum('bkgqs,bskd->bqkgd', p.astype(v.dtype), v)


def softmax_with_sink(logits, sink):
    sink_col = jnp.broadcast_to(sink.astype(F32).reshape(1, N_KV_HEADS, KV_GROUP, 1, 1), logits.shape[:-1] + (1,))
    p = jax.nn.softmax(jnp.concatenate([logits, sink_col], axis=-1), axis=-1)
    return p[..., :-1]


def to_blocks(q):
    bsz, t = q.shape[:2]
    return jnp.moveaxis(q.reshape((bsz, t // BLOCK, BLOCK) + q.shape[2:]), 1, 0)


def from_blocks(o):
    nb, bsz = o.shape[:2]
    return jnp.moveaxis(o, 0, 1).reshape(bsz, nb * BLOCK, N_HEADS * HEAD_DIM)


def global_attention_mixer(x_lat, x_ctx, w_qkv, q_norm, k_norm, w_o, cos, sin, need_ctx):
    q_l, k_l, v_l = project_qkv(x_lat, w_qkv)
    q_c, k_c, v_c = project_qkv(x_ctx, w_qkv)
    q_l = apply_rope(rmsnorm(q_l, q_norm), cos, sin)
    k_l = apply_rope(rmsnorm(k_l, k_norm), cos, sin)
    k_c = rmsnorm(k_c, k_norm)
    k_all = jnp.concatenate([k_c, k_l], axis=1)
    v_all = jnp.concatenate([v_c, v_l], axis=1)

    def block_fn(q_blk):
        return gqa_values(jax.nn.softmax(gqa_scores(q_blk, k_all), axis=-1), v_all)

    out_lat = jnp.dot(from_blocks(lax.map(block_fn, to_blocks(group_queries(q_l)))), w_o)
    if not need_ctx:
        return out_lat, None
    bsz, n_ctx = x_ctx.shape[:2]
    q_c = group_queries(rmsnorm(q_c, q_norm))
    o_c = gqa_values(jax.nn.softmax(gqa_scores(q_c, k_c), axis=-1), v_c)
    return out_lat, jnp.dot(o_c.reshape(bsz, n_ctx, N_HEADS * HEAD_DIM), w_o)


def sliding_window_mixer(x_lat, x_ctx, w_qkv, sink, w_o, cos, sin, need_ctx):
    q_l, k_l, v_l = project_qkv(x_lat, w_qkv)
    q_c, k_c, v_c = project_qkv(x_ctx, w_qkv)
    q_l = apply_rope(q_l, cos, sin)
    k_l = apply_rope(k_l, cos, sin)
    seq = x_lat.shape[1]
    pad = ((0, 0), (BLOCK, BLOCK), (0, 0), (0, 0))
    k_pad = jnp.pad(k_l, pad)
    v_pad = jnp.pad(v_l, pad)
    offs_q = jnp.arange(BLOCK)
    offs_k = jnp.arange(3 * BLOCK)

    def block_fn(args):
        blk, q_blk = args
        start = blk * BLOCK
        k_band = lax.dynamic_slice_in_dim(k_pad, start, 3 * BLOCK, axis=1)
        v_band = lax.dynamic_slice_in_dim(v_pad, start, 3 * BLOCK, axis=1)
        q_pos = start + offs_q
        k_pos = start - BLOCK + offs_k
        valid = (jnp.abs(q_pos[:, None] - k_pos[None, :]) <= WINDOW) & (k_pos[None, :] >= 0) & (k_pos[None, :] < seq)
        s_band = jnp.where(valid, gqa_scores(q_blk, k_band), -jnp.inf)
        p = softmax_with_sink(jnp.concatenate([s_band, gqa_scores(q_blk, k_c)], axis=-1), sink)
        return gqa_values(p[..., :3 * BLOCK], v_band) + gqa_values(p[..., 3 * BLOCK:], v_c)

    o = lax.map(block_fn, (jnp.arange(seq // BLOCK), to_blocks(group_queries(q_l))))
    out_lat = jnp.dot(from_blocks(o), w_o)
    if not need_ctx:
        return out_lat, None
    bsz, n_ctx = x_ctx.shape[:2]
    q_c = group_queries(q_c)
    o_c = gqa_values(softmax_with_sink(gqa_scores(q_c, k_c), sink), v_c)
    return out_lat, jnp.dot(o_c.reshape(bsz, n_ctx, N_HEADS * HEAD_DIM), w_o)


def mlstm_chunkwise(q, k, v, log_i, log_f, state):
    bsz, nh, t = log_i.shape
    n_chunks = t // M_CHUNK

    def chunked(a):
        return jnp.moveaxis(a.reshape((bsz, nh, n_chunks, M_CHUNK) + a.shape[3:]), 2, 0)

    causal = jnp.tril(jnp.ones((M_CHUNK, M_CHUNK), dtype=bool))

    def step(carry, xs):
        c_mat, n_vec, m_prev = carry
        qc, kc, vc, ic, fc = xs
        b_cum = jnp.cumsum(fc, axis=-1)
        log_inter = b_cum + m_prev[..., None]
        log_intra = jnp.where(causal, b_cum[..., :, None] - b_cum[..., None, :] + ic[..., None, :], -jnp.inf)
        m_t = jnp.maximum(log_inter, jnp.max(log_intra, axis=-1))
        w_inter = jnp.exp(log_inter - m_t)
        s_qk = jnp.einsum('bhtd,bhsd->bhts', qc, kc) * jnp.exp(log_intra - m_t[..., None])
        num = jnp.einsum('bhts,bhsv->bhtv', s_qk, vc) + w_inter[..., None] * jnp.einsum('bhvd,bhtd->bhtv', c_mat, qc)
        den = jnp.sum(s_qk, axis=-1) + w_inter * jnp.einsum('bhd,bhtd->bht', n_vec, qc)
        h_out = num / jnp.maximum(jnp.abs(den), jnp.exp(-m_t))[..., None]
        b_tot = b_cum[..., -1]
        log_w = b_tot[..., None] - b_cum + ic
        m_new = jnp.maximum(b_tot + m_prev, jnp.max(log_w, axis=-1))
        decay = jnp.exp(b_tot + m_prev - m_new)
        w_k = jnp.exp(log_w - m_new[..., None])
        c_new = decay[..., None, None] * c_mat + jnp.einsum('bhsv,bhsd->bhvd', vc * w_k[..., None], kc)
        n_new = decay[..., None] * n_vec + jnp.einsum('bhs,bhsd->bhd', w_k, kc)
        return (c_new, n_new, m_new), h_out

    state, h_chunks = lax.scan(step, state, (chunked(q), chunked(k), chunked(v), chunked(log_i), chunked(log_f)))
    h = jnp.moveaxis(h_chunks, 0, 2).reshape(bsz, nh, t, -1)
    return h, state


def mlstm_mixer(x_lat, x_ctx, w_in, gate_b, head_norm, w_o, need_ctx):
    qk_w = M_HEADS * M_QK_DIM
    v_w = M_HEADS * M_V_DIM
    splits = [qk_w, 2 * qk_w, 2 * qk_w + v_w, 2 * qk_w + v_w + D_MODEL]

    def project(x):
        bsz, t, _ = x.shape
        q, k, v, o, g = jnp.split(jnp.dot(x, w_in), splits, axis=-1)

        def heads(a, d):
            return jnp.moveaxis(a.reshape(bsz, t, M_HEADS, d), 2, 1).astype(F32)

        g = GATE_CAP * jnp.tanh((g.astype(F32) + gate_b.astype(F32)) / GATE_CAP)
        g = jnp.moveaxis(g.reshape(bsz, t, 4, M_HEADS), 1, -1)
        return heads(q, M_QK_DIM) * (M_QK_DIM ** -0.5), heads(k, M_QK_DIM), heads(v, M_V_DIM), o, g

    def run(q, k, v, g, direction, state):
        def flip(a):
            return jnp.flip(a, axis=2) if direction == 1 else a
        log_i = g[:, 2 * direction]
        log_f = jax.nn.log_sigmoid(g[:, 2 * direction + 1])
        h, st = mlstm_chunkwise(flip(q), flip(k), flip(v), flip(log_i), flip(log_f), state)
        return flip(h), st

    def finish(h, o):
        bsz, _, t, _ = h.shape
        h = rmsnorm(jnp.moveaxis(h, 1, 2), head_norm.reshape(M_HEADS, M_V_DIM)).reshape(bsz, t, D_MODEL)
        return jnp.dot(h.astype(o.dtype) * jax.nn.sigmoid(o), w_o)

    q_l, k_l, v_l, o_l, g_l = project(x_lat)
    q_c, k_c, v_c, o_c, g_c = project(x_ctx)
    bsz = x_lat.shape[0]
    zero_state = (jnp.zeros((bsz, M_HEADS, M_V_DIM, M_QK_DIM), F32),
                  jnp.zeros((bsz, M_HEADS, M_QK_DIM), F32),
                  jnp.zeros((bsz, M_HEADS), F32))
    lat_dirs = []
    ctx_dirs = []
    for direction in range(2):
        h_c, ctx_state = run(q_c, k_c, v_c, g_c, direction, zero_state)
        h_l, _ = run(q_l, k_l, v_l, g_l, direction, ctx_state)
        lat_dirs.append(h_l)
        ctx_dirs.append(h_c)
    out_lat = finish(lat_dirs[0] + lat_dirs[1], o_l)
    if not need_ctx:
        return out_lat, None
    return out_lat, finish(ctx_dirs[0] + ctx_dirs[1], o_c)


def moe_ffn(x, router_w, router_b, exp_gu, exp_down, shared_gu, shared_down):
    shape = x.shape
    tok = x.reshape(-1, D_MODEL)
    n = tok.shape[0]
    scores = jax.nn.sigmoid(jnp.dot(tok, router_w, preferred_element_type=F32))
    biased = scores + router_b.astype(F32)
    grp = biased.reshape(n, N_GROUPS, N_EXPERTS // N_GROUPS)
    grp_score = jnp.sum(lax.top_k(grp, 2)[0], axis=-1)
    _, grp_idx = lax.top_k(grp_score, TOPK_GROUPS)
    grp_mask = jnp.sum(jax.nn.one_hot(grp_idx, N_GROUPS, dtype=F32), axis=1)
    exp_mask = jnp.repeat(grp_mask, N_EXPERTS // N_GROUPS, axis=1) > 0
    _, idx = lax.top_k(jnp.where(exp_mask, biased, -jnp.inf), TOP_K)
    w = jnp.take_along_axis(scores, idx, axis=-1)
    w = w / jnp.sum(w, axis=-1, keepdims=True) * ROUTED_SCALE
    combine = jnp.sum(jax.nn.one_hot(idx, N_EXPERTS, dtype=F32) * w[..., None], axis=1)

    def block_fn(args):
        xb, cb = args
        g, u = jnp.split(jnp.einsum('td,edf->tef', xb, exp_gu), 2, axis=-1)
        act = jax.nn.silu(g) * u * cb[..., None].astype(xb.dtype)
        return jnp.einsum('tef,efd->td', act, exp_down)

    nb = n // BLOCK
    routed = lax.map(block_fn, (tok.reshape(nb, BLOCK, D_MODEL), combine.reshape(nb, BLOCK, N_EXPERTS)))
    sg, su = jnp.split(jnp.dot(tok, shared_gu), 2, axis=-1)
    shared = jnp.dot(jax.nn.silu(sg) * su, shared_down)
    return (routed.reshape(n, D_MODEL) + shared).reshape(shape)


def setup_inputs(seed: int = 0) -> dict:
    key = jax.random.key(seed)
    keys = iter(jax.random.split(key, 32 * DEPTH + 8))

    def nrm(shape, scale):
        return jax.random.normal(next(keys), shape, F32) * scale

    def gain(n):
        return 1.0 + nrm((n,), 0.02)

    D = D_MODEL
    qkv_cols = (N_HEADS + 2 * N_KV_HEADS) * HEAD_DIM
    m_cols = 2 * M_HEADS * M_QK_DIM + M_HEADS * M_V_DIM + D + 4 * M_HEADS
    inputs = {}
    inputs['x'] = nrm((BATCH, SEQ, D), 1.0)
    inputs['c'] = nrm((BATCH, D), 1.0)
    inputs['ctx'] = nrm((BATCH, CTX_LEN, D), 1.0)
    inputs['c_ctx'] = nrm((D,), 1.0)
    for i in range(DEPTH):
        p = f'l{i}_'
        kind = i % N_MIXERS
        inputs[p + 'ada_down'] = nrm((D, ADA_RANK), D ** -0.5)
        inputs[p + 'ada_up'] = nrm((ADA_RANK, N_MOD * D), 0.3 * ADA_RANK ** -0.5)
        inputs[p + 'ada_b'] = nrm((N_MOD * D,), 0.02)
        inputs[p + 'norm1'] = gain(D)
        if kind == 0:
            inputs[p + 'attn_qkv'] = nrm((D, qkv_cols), D ** -0.5)
            inputs[p + 'q_norm'] = gain(HEAD_DIM)
            inputs[p + 'k_norm'] = gain(HEAD_DIM)
            inputs[p + 'attn_o'] = nrm((N_HEADS * HEAD_DIM, D), (N_HEADS * HEAD_DIM) ** -0.5)
        elif kind == 1:
            inputs[p + 'mlstm_in'] = nrm((D, m_cols), D ** -0.5)
            inputs[p + 'mlstm_gate_b'] = jnp.concatenate([
                nrm((M_HEADS,), 0.1), 3.0 + nrm((M_HEADS,), 0.5),
                nrm((M_HEADS,), 0.1), 3.0 + nrm((M_HEADS,), 0.5)])
            inputs[p + 'mlstm_head_norm'] = gain(D)
            inputs[p + 'mlstm_o'] = nrm((D, D), D ** -0.5)
        else:
            inputs[p + 'swa_qkv'] = nrm((D, qkv_cols), D ** -0.5)
            inputs[p + 'swa_sink'] = nrm((N_HEADS,), 0.5)
            inputs[p + 'swa_o'] = nrm((N_HEADS * HEAD_DIM, D), (N_HEADS * HEAD_DIM) ** -0.5)
        inputs[p + 'norm2'] = gain(D)
        inputs[p + 'router_w'] = nrm((D, N_EXPERTS), D ** -0.5)
        inputs[p + 'router_b'] = nrm((N_EXPERTS,), 0.01)
        inputs[p + 'exp_gu'] = nrm((N_EXPERTS, D, 2 * D_EXPERT), D ** -0.5)
        inputs[p + 'exp_down'] = nrm((N_EXPERTS, D_EXPERT, D), D_EXPERT ** -0.5)
        inputs[p + 'shared_gu'] = nrm((D, 2 * D_SHARED), D ** -0.5)
        inputs[p + 'shared_down'] = nrm((D_SHARED, D), D_SHARED ** -0.5)
    inputs['final_norm'] = gain(D)
    return inputs


def reference(x, c, ctx, c_ctx,
              l0_ada_down, l0_ada_up, l0_ada_b, l0_norm1, l0_attn_qkv, l0_q_norm, l0_k_norm, l0_attn_o,
              l0_norm2, l0_router_w, l0_router_b, l0_exp_gu, l0_exp_down, l0_shared_gu, l0_shared_down,
              l1_ada_down, l1_ada_up, l1_ada_b, l1_norm1, l1_mlstm_in, l1_mlstm_gate_b, l1_mlstm_head_norm, l1_mlstm_o,
              l1_norm2, l1_router_w, l1_router_b, l1_exp_gu, l1_exp_down, l1_shared_gu, l1_shared_down,
              l2_ada_down, l2_ada_up, l2_ada_b, l2_norm1, l2_swa_qkv, l2_swa_sink, l2_swa_o,
              l2_norm2, l2_router_w, l2_router_b, l2_exp_gu, l2_exp_down, l2_shared_gu, l2_shared_down,
              l3_ada_down, l3_ada_up, l3_ada_b, l3_norm1, l3_attn_qkv, l3_q_norm, l3_k_norm, l3_attn_o,
              l3_norm2, l3_router_w, l3_router_b, l3_exp_gu, l3_exp_down, l3_shared_gu, l3_shared_down,
              final_norm):
    layers = [
        ((l0_ada_down, l0_ada_up, l0_ada_b), l0_norm1, (l0_attn_qkv, l0_q_norm, l0_k_norm, l0_attn_o), l0_norm2,
         (l0_router_w, l0_router_b, l0_exp_gu, l0_exp_down, l0_shared_gu, l0_shared_down)),
        ((l1_ada_down, l1_ada_up, l1_ada_b), l1_norm1, (l1_mlstm_in, l1_mlstm_gate_b, l1_mlstm_head_norm, l1_mlstm_o), l1_norm2,
         (l1_router_w, l1_router_b, l1_exp_gu, l1_exp_down, l1_shared_gu, l1_shared_down)),
        ((l2_ada_down, l2_ada_up, l2_ada_b), l2_norm1, (l2_swa_qkv, l2_swa_sink, l2_swa_o), l2_norm2,
         (l2_router_w, l2_router_b, l2_exp_gu, l2_exp_down, l2_shared_gu, l2_shared_down)),
        ((l3_ada_down, l3_ada_up, l3_ada_b), l3_norm1, (l3_attn_qkv, l3_q_norm, l3_k_norm, l3_attn_o), l3_norm2,
         (l3_router_w, l3_router_b, l3_exp_gu, l3_exp_down, l3_shared_gu, l3_shared_down)),
    ]
    cos, sin = axial_rope(x.shape[1])
    cond = jnp.concatenate([c, c_ctx[None, :]], axis=0)
    h_lat, h_ctx = x, ctx
    for i in range(DEPTH):
        ada, norm1, mixer_params, norm2, moe_params = layers[i]
        need_ctx = i < DEPTH - 1
        mods = ada_modulation(cond, *ada)
        lat_mod = [m[:-1, None, :] for m in mods]
        ctx_mod = [m[-1] for m in mods]
        a_lat = modulate(rmsnorm(h_lat, norm1), lat_mod[0], lat_mod[1])
        a_ctx = modulate(rmsnorm(h_ctx, norm1), ctx_mod[0], ctx_mod[1])
        kind = i % N_MIXERS
        if kind == 0:
            o_lat, o_ctx = global_attention_mixer(a_lat, a_ctx, *mixer_params, cos, sin, need_ctx)
        elif kind == 1:
            o_lat, o_ctx = mlstm_mixer(a_lat, a_ctx, *mixer_params, need_ctx)
        else:
            o_lat, o_ctx = sliding_window_mixer(a_lat, a_ctx, *mixer_params, cos, sin, need_ctx)
        h_lat = h_lat + lat_mod[2] * o_lat
        f_lat = modulate(rmsnorm(h_lat, norm2), lat_mod[3], lat_mod[4])
        h_lat = h_lat + lat_mod[5] * moe_ffn(f_lat, *moe_params)
        if need_ctx:
            h_ctx = h_ctx + ctx_mod[2] * o_ctx
            f_ctx = modulate(rmsnorm(h_ctx, norm2), ctx_mod[3], ctx_mod[4])
            h_ctx = h_ctx + ctx_mod[5] * moe_ffn(f_ctx, *moe_params)
    return rmsnorm(h_lat, final_norm)
```

```python
import functools
import math
from typing import NamedTuple

import jax
import jax.numpy as jnp
from jax import lax
from jax.experimental import pallas as pl
from jax.experimental.pallas import tpu as pltpu

F32 = jnp.float32
BF16 = jnp.bfloat16

D_MODEL = 4096
GRID_W = 64
RMS_EPS = 1e-6
N_MOD = 6
N_HEADS = 32
N_KV_HEADS = 8
HEAD_DIM = D_MODEL // N_HEADS
KV_GROUP = N_HEADS // N_KV_HEADS
ROPE_THETA = 10000.0
WINDOW = 128
M_HEADS = 8
M_V_DIM = D_MODEL // M_HEADS
M_QK_DIM = M_V_DIM // 2
M_CHUNK = 128
GATE_CAP = 15.0
N_EXPERTS = 64
TOP_K = 8
N_GROUPS = 8
TOPK_GROUPS = 4
D_EXPERT = 192
ROUTED_SCALE = 2.5

LANES = 128
VMEM_LIMIT = 56 * 1024 * 1024
NEG_INF = float("-inf")


def _cparams(n_axes):
    return pltpu.CompilerParams(dimension_semantics=("arbitrary",) * n_axes,
                                vmem_limit_bytes=VMEM_LIMIT)


def _pick(n, pref):
    t = pref
    while n % t:
        t //= 2
    return t


class _Segments(NamedTuple):
    seq: int
    bsz: int
    n_ctx_rows: int

    @property
    def tile_unit(self):
        return math.gcd(self.seq, self.n_ctx_rows)

    def of_row(self, row):
        return jnp.minimum(row // self.seq, self.bsz)


def _ada_kernel(cond_ref, down_ref, up_ref, b_ref, out_ref):
    c = cond_ref[...]
    t = jnp.dot(c * jax.nn.sigmoid(c), down_ref[...], precision=lax.Precision.HIGHEST,
                preferred_element_type=F32)
    out_ref[...] = jnp.dot(t, up_ref[...], precision=lax.Precision.HIGHEST,
                           preferred_element_type=F32) + b_ref[...]


def ada_table(cond_pad, down, up, bias):
    r, d = cond_pad.shape
    rank = down.shape[1]
    n = up.shape[1]
    tn = 2048
    out = pl.pallas_call(
        _ada_kernel,
        grid=(n // tn,),
        in_specs=[pl.BlockSpec((r, d), lambda j: (0, 0)),
                  pl.BlockSpec((d, rank), lambda j: (0, 0)),
                  pl.BlockSpec((rank, tn), lambda j: (0, j)),
                  pl.BlockSpec((1, tn), lambda j: (0, j))],
        out_specs=pl.BlockSpec((r, tn), lambda j: (0, j)),
        out_shape=jax.ShapeDtypeStruct((r, n), F32),
        compiler_params=_cparams(1),
        name="ada_table",
    )(cond_pad, down, up, bias.reshape(1, n))
    return out.reshape(r, N_MOD, d)


def _rowwise_kernel(*refs, gate_idx, shift_idx, route):
    it = iter(refs)
    h_ref = next(it)
    y_ref, gmod_ref = (next(it), next(it)) if gate_idx is not None else (None, None)
    g_ref = next(it)
    smod_ref = next(it) if shift_idx is not None else None
    wt_ref, rb_ref = (next(it), next(it)) if route else (None, None)
    h = h_ref[...]
    if gate_idx is not None:
        h = h + gmod_ref[0, gate_idx:gate_idx + 1, :] * y_ref[...].astype(F32)
        next(it)[...] = h
    a = h * lax.rsqrt(jnp.mean(h * h, axis=-1, keepdims=True) + RMS_EPS) * g_ref[...]
    if shift_idx is not None:
        a = a * (1.0 + smod_ref[0, shift_idx + 1:shift_idx + 2, :]) + smod_ref[0, shift_idx:shift_idx + 1, :]
    a_ref = next(it)
    a_ref[...] = a.astype(a_ref.dtype)
    if route:
        next(it)[...] = _route(a, wt_ref[...], rb_ref[...])


def rowwise(h, gain, seg, *, resid=None, shift=None, out_dtype=BF16, rows=None, route=None):
    n, d = h.shape
    rows = n if rows is None else rows
    tm = _pick(seg.tile_unit, 256)
    row_spec = pl.BlockSpec((tm, d), lambda i: (i, 0))
    mod_spec = pl.BlockSpec((1, N_MOD, d), lambda i: (seg.of_row(i * tm), 0, 0))
    in_specs, args = [row_spec], [h]
    if resid is not None:
        in_specs += [row_spec, mod_spec]
        args += [resid[0], resid[1]]
    in_specs.append(pl.BlockSpec((1, d), lambda i: (0, 0)))
    args.append(gain.reshape(1, d))
    if shift is not None:
        in_specs.append(mod_spec)
        args.append(shift[0])
    if route is not None:
        in_specs += [pl.BlockSpec((N_EXPERTS, d), lambda i: (0, 0)),
                     pl.BlockSpec((N_EXPERTS, 1), lambda i: (0, 0))]
        args += [route[0], route[1].reshape(N_EXPERTS, 1)]
    out_specs, out_shape = [], []
    if resid is not None:
        out_specs.append(row_spec)
        out_shape.append(jax.ShapeDtypeStruct((rows, d), F32))
    out_specs.append(row_spec)
    out_shape.append(jax.ShapeDtypeStruct((rows, d), out_dtype))
    if route is not None:
        out_specs.append(pl.BlockSpec((tm, LANES), lambda i: (i, 0)))
        out_shape.append(jax.ShapeDtypeStruct((rows, LANES), F32))
    return pl.pallas_call(
        functools.partial(_rowwise_kernel, gate_idx=None if resid is None else resid[2],
                          shift_idx=None if shift is None else shift[1], route=route is not None),
        grid=(rows // tm,),
        in_specs=in_specs, out_specs=out_specs, out_shape=out_shape,
        compiler_params=_cparams(1),
        name="rowwise",
    )(*args)


def _mm_plain_kernel(x_ref, w_ref, o_ref, *, scale_tiles, scale):
    acc = jnp.dot(x_ref[...], w_ref[...], preferred_element_type=F32)
    if scale_tiles:
        acc = acc * jnp.where(pl.program_id(1) < scale_tiles, scale, 1.0)
    o_ref[...] = acc.astype(o_ref.dtype)


def _mm_resid_kernel(x_ref, w_ref, h_ref, mod_ref, o_ref, *, gate_idx):
    acc = jnp.dot(x_ref[...], w_ref[...], preferred_element_type=F32)
    o_ref[...] = h_ref[...] + mod_ref[0, gate_idx:gate_idx + 1, :] * acc


def _mm_qkv_kernel(x_ref, w_ref, cos_ref, sin_ref, qn_ref, kn_ref, o_ref, *, nq_tiles, nk_tiles, qk_norm):
    acc = jnp.dot(x_ref[...], w_ref[...], preferred_element_type=F32)
    j = pl.program_id(1)

    @pl.when(j >= nq_tiles + nk_tiles)
    def _():
        o_ref[...] = acc.astype(o_ref.dtype)

    @pl.when(j < nq_tiles + nk_tiles)
    def _():
        is_q = j < nq_tiles
        post = jnp.where(is_q, HEAD_DIM ** -0.5, 1.0)
        gain = jnp.where(is_q, qn_ref[...], kn_ref[...])
        cos = cos_ref[...]
        sin = sin_ref[...]
        for s in range(0, acc.shape[1], HEAD_DIM):
            xh = acc[:, s:s + HEAD_DIM]
            if qk_norm:
                xh = xh * lax.rsqrt(jnp.mean(xh * xh, axis=-1, keepdims=True) + RMS_EPS) * gain
            xh = xh * cos + pltpu.roll(xh, HEAD_DIM // 2, axis=1) * sin
            o_ref[:, s:s + HEAD_DIM] = (xh * post).astype(o_ref.dtype)


def matmul(x, w, *, tm=1024, tn=512, out_dtype=BF16, rows=None, scale_tiles=0, scale=1.0,
           resid=None, qkv=None):
    m, k = x.shape
    m = m if rows is None else rows
    n = w.shape[1]
    tm = _pick(m if resid is None else math.gcd(m, resid[3].tile_unit), tm)
    tn = _pick(n, tn)
    grid = (m // tm, n // tn)
    x_spec = pl.BlockSpec((tm, k), lambda i, j: (i, 0))
    w_spec = pl.BlockSpec((k, tn), lambda i, j: (0, j))
    o_spec = pl.BlockSpec((tm, tn), lambda i, j: (i, j))
    if resid is not None:
        h, mods, gate_idx, seg = resid
        kern = functools.partial(_mm_resid_kernel, gate_idx=gate_idx)
        in_specs = [x_spec, w_spec, o_spec,
                    pl.BlockSpec((1, N_MOD, tn), lambda i, j: (seg.of_row(i * tm), 0, j))]
        args = (x, w, h, mods)
        out_dtype = F32
    elif qkv is not None:
        cos, sin, qn, kn, qk_norm = qkv
        kern = functools.partial(_mm_qkv_kernel, nq_tiles=N_HEADS * HEAD_DIM // tn,
                                 nk_tiles=N_KV_HEADS * HEAD_DIM // tn, qk_norm=qk_norm)
        tab_spec = pl.BlockSpec((tm, HEAD_DIM), lambda i, j: (i, 0))
        vec_spec = pl.BlockSpec((1, HEAD_DIM), lambda i, j: (0, 0))
        in_specs = [x_spec, w_spec, tab_spec, tab_spec, vec_spec, vec_spec]
        args = (x, w, cos, sin, qn.reshape(1, HEAD_DIM), kn.reshape(1, HEAD_DIM))
    else:
        kern = functools.partial(_mm_plain_kernel, scale_tiles=scale_tiles, scale=scale)
        in_specs = [x_spec, w_spec]
        args = (x, w)
    return pl.pallas_call(
        kern, grid=grid, in_specs=in_specs, out_specs=o_spec,
        out_shape=jax.ShapeDtypeStruct((m, n), out_dtype),
        compiler_params=_cparams(2),
        name="matmul",
    )(*args)


def _attn_kernel(*refs, tq, lat_keys, window, has_sink):
    it = iter(refs)
    sink_ref = next(it) if has_sink else None
    q_ref = next(it)
    kc_ref, vc_ref = next(it), next(it)
    kl_ref, vl_ref = (next(it), next(it)) if lat_keys else (None, None)
    o_ref = next(it)
    kvh = pl.program_id(1)
    qi = pl.program_id(2)
    rows = KV_GROUP * tq
    q = q_ref[...]
    qs = jnp.concatenate([q[:, g * HEAD_DIM:(g + 1) * HEAD_DIM] for g in range(KV_GROUP)], axis=0)
    nt = (((1,), (1,)), ((), ()))
    s_c = lax.dot_general(qs, kc_ref[...], nt, preferred_element_type=F32)
    m = jnp.max(s_c, axis=-1, keepdims=True)
    if lat_keys:
        if window:
            seq = kl_ref.shape[0]
            band = min(tq + 2 * WINDOW, seq)
            start = pl.multiple_of(jnp.clip(qi * tq - WINDOW, 0, seq - band), LANES)
            kl = kl_ref[pl.ds(start, band), :]
            vl = vl_ref[pl.ds(start, band), :]
            s_l = lax.dot_general(qs, kl, nt, preferred_element_type=F32)
            q_pos = qi * tq + lax.broadcasted_iota(jnp.int32, (rows, band), 0) % tq
            k_pos = start + lax.broadcasted_iota(jnp.int32, (rows, band), 1)
            s_l = jnp.where(jnp.abs(q_pos - k_pos) <= WINDOW, s_l, NEG_INF)
        else:
            kl = kl_ref[...]
            vl = vl_ref[...]
            s_l = lax.dot_general(qs, kl, nt, preferred_element_type=F32)
        m = jnp.maximum(m, jnp.max(s_l, axis=-1, keepdims=True))
    if has_sink:
        sink = jnp.concatenate(
            [jnp.full((tq, 1), sink_ref[kvh * KV_GROUP + g], F32) for g in range(KV_GROUP)], axis=0)
        m = jnp.maximum(m, sink)
    p_c = jnp.exp(s_c - m)
    den = jnp.sum(p_c, axis=-1, keepdims=True)
    acc = jnp.dot(p_c.astype(BF16), vc_ref[...], preferred_element_type=F32)
    if lat_keys:
        p_l = jnp.exp(s_l - m)
        den = den + jnp.sum(p_l, axis=-1, keepdims=True)
        acc = acc + jnp.dot(p_l.astype(BF16), vl, preferred_element_type=F32)
    if has_sink:
        den = den + jnp.exp(sink - m)
    out = acc / den
    for g in range(KV_GROUP):
        o_ref[:, g * HEAD_DIM:(g + 1) * HEAD_DIM] = out[g * tq:(g + 1) * tq].astype(o_ref.dtype)


def attention(qkv, dims, *, lat_queries, window, sink):
    bsz, seq, n_ctx = dims
    n_lat = bsz * seq
    g_cols = KV_GROUP * HEAD_DIM
    k_col0 = N_HEADS * HEAD_DIM // HEAD_DIM
    v_col0 = k_col0 + N_KV_HEADS
    has_sink = sink is not None
    if lat_queries:
        tq = _pick(seq, 256)
        q_tiles = seq // tq
        q_row0 = 0
        out_rows = n_lat
    else:
        tq = n_ctx
        q_tiles = 1
        q_row0 = n_lat // tq
        out_rows = bsz * n_ctx
    ctx_blk0 = n_lat // n_ctx
    in_specs = []
    args = []
    if has_sink:
        in_specs.append(pl.BlockSpec(memory_space=pltpu.SMEM))
        args.append(sink.astype(F32))
    in_specs += [
        pl.BlockSpec((tq, g_cols), lambda b, h, i: (q_row0 + b * q_tiles + i, h)),
        pl.BlockSpec((n_ctx, HEAD_DIM), lambda b, h, i: (ctx_blk0 + b, k_col0 + h)),
        pl.BlockSpec((n_ctx, HEAD_DIM), lambda b, h, i: (ctx_blk0 + b, v_col0 + h)),
    ]
    args += [qkv, qkv, qkv]
    if lat_queries:
        in_specs += [
            pl.BlockSpec((seq, HEAD_DIM), lambda b, h, i: (b, k_col0 + h)),
            pl.BlockSpec((seq, HEAD_DIM), lambda b, h, i: (b, v_col0 + h)),
        ]
        args += [qkv, qkv]
    return pl.pallas_call(
        functools.partial(_attn_kernel, tq=tq, lat_keys=lat_queries, window=window, has_sink=has_sink),
        grid=(bsz, N_KV_HEADS, q_tiles),
        in_specs=in_specs,
        out_specs=pl.BlockSpec((tq, g_cols), lambda b, h, i: (b * q_tiles + i, h)),
        out_shape=jax.ShapeDtypeStruct((out_rows, N_HEADS * HEAD_DIM), BF16),
        compiler_params=_cparams(3),
        name="attention",
    )(*args)


def _mlstm_gate_kernel(x_ref, w_ref, b_ref, o_ref):
    g = jnp.dot(x_ref[...], w_ref[...], preferred_element_type=F32) + b_ref[...]
    g = GATE_CAP * jnp.tanh(g / GATE_CAP)
    log_f = jnp.minimum(g, 0.0) - jnp.log(1.0 + jnp.exp(-jnp.abs(g)))
    col = lax.broadcasted_iota(jnp.int32, g.shape, 1)
    is_f = (col // M_HEADS) % 2 == 1
    o_ref[...] = jnp.transpose(jnp.where(is_f, log_f, g))


def mlstm_gates(a, w_gate_pad, gate_b_pad):
    n, k = a.shape
    tm = _pick(n, 1024)
    return pl.pallas_call(
        _mlstm_gate_kernel,
        grid=(n // tm,),
        in_specs=[pl.BlockSpec((tm, k), lambda i: (i, 0)),
                  pl.BlockSpec((k, LANES), lambda i: (0, 0)),
                  pl.BlockSpec((1, LANES), lambda i: (0, 0))],
        out_specs=pl.BlockSpec((LANES, tm), lambda i: (0, i)),
        out_shape=jax.ShapeDtypeStruct((LANES, n), F32),
        compiler_params=_cparams(1),
        name="mlstm_gates",
    )(a, w_gate_pad, gate_b_pad)


def _mlstm_kernel(q_ref, k_ref, v_ref, gi_ref, gf_ref, o_ref, ct_ref, n_ref, m_ref):
    d = pl.program_id(1)
    c = pl.program_id(2)
    L = M_CHUNK

    @pl.when(c == 0)
    def _():
        ct_ref[...] = jnp.zeros_like(ct_ref)
        n_ref[...] = jnp.zeros_like(n_ref)
        m_ref[...] = jnp.zeros_like(m_ref)

    sgn = 1 - 2 * d
    diff = (lax.broadcasted_iota(jnp.int32, (L, L), 0) - lax.broadcasted_iota(jnp.int32, (L, L), 1)) * sgn
    mask = diff >= 0
    mask_t = diff <= 0
    eye = diff == 0
    nt = (((1,), (1,)), ((), ()))
    tn = (((0,), (0,)), ((), ()))
    for h in range(M_HEADS):
        qc = q_ref[:, h * M_QK_DIM:(h + 1) * M_QK_DIM]
        kc = k_ref[:, h * M_QK_DIM:(h + 1) * M_QK_DIM]
        vc = v_ref[:, h * M_V_DIM:(h + 1) * M_V_DIM]
        i_row = gi_ref[h:h + 1, :]
        f_row = gf_ref[h:h + 1, :]
        m_prev = m_ref[h, :, 0:1]
        n_prev = n_ref[h]
        ct_prev = ct_ref[h]
        f_col = jnp.sum(jnp.where(eye, f_row, 0.0), axis=1, keepdims=True)
        i_col = jnp.sum(jnp.where(eye, i_row, 0.0), axis=1, keepdims=True)
        b_col = jnp.sum(jnp.where(mask, f_row, 0.0), axis=1, keepdims=True)
        b_row = jnp.sum(jnp.where(mask_t, f_col, 0.0), axis=0, keepdims=True)
        log_inter = b_col + m_prev
        log_intra = jnp.where(mask, b_col - b_row + i_row, NEG_INF)
        m_t = jnp.maximum(log_inter, jnp.max(log_intra, axis=1, keepdims=True))
        w_inter = jnp.exp(log_inter - m_t)
        s_qk = lax.dot_general(qc, kc, nt, preferred_element_type=F32) * jnp.exp(log_intra - m_t)
        num = (jnp.dot(s_qk.astype(BF16), vc, preferred_element_type=F32)
               + w_inter * jnp.dot(qc, ct_prev.astype(BF16), preferred_element_type=F32))
        den = (jnp.sum(s_qk, axis=1, keepdims=True)
               + w_inter * jnp.sum(qc.astype(F32) * n_prev, axis=1, keepdims=True))
        h_out = num / jnp.maximum(jnp.abs(den), jnp.exp(-m_t))
        o_ref[0, :, h * M_V_DIM:(h + 1) * M_V_DIM] = h_out.astype(o_ref.dtype)
        b_tot = jnp.sum(f_row, axis=1, keepdims=True)
        log_w = b_tot - b_col + i_col
        m_new = jnp.maximum(b_tot + m_prev, jnp.max(log_w, axis=0, keepdims=True))
        decay = jnp.exp(b_tot + m_prev - m_new)
        w_k = jnp.exp(log_w - m_new)
        vw = (vc.astype(F32) * w_k).astype(BF16)
        ct_ref[h] = decay * ct_prev + lax.dot_general(kc, vw, tn, preferred_element_type=F32)
        n_ref[h] = decay * n_prev + jnp.sum(kc.astype(F32) * w_k, axis=0, keepdims=True)
        m_ref[h] = jnp.broadcast_to(m_new, (1, LANES))


def mlstm_scan(proj, gates_t, dims):
    bsz, seq, n_ctx = dims
    nt_rows = proj.shape[0]
    L = M_CHUNK
    ncc, nlc = n_ctx // L, seq // L
    lat_blocks = bsz * nlc
    qk_w = M_HEADS * M_QK_DIM

    def row_blk(b, d, c):
        cc = jnp.where(d == 0, c, ncc - 1 - c)
        lc = jnp.where(d == 0, c - ncc, nlc - 1 - (c - ncc))
        return jnp.where(c < ncc, lat_blocks + b * ncc + cc, b * nlc + lc)

    return pl.pallas_call(
        _mlstm_kernel,
        grid=(bsz, 2, ncc + nlc),
        in_specs=[
            pl.BlockSpec((L, qk_w), lambda b, d, c: (row_blk(b, d, c), 0)),
            pl.BlockSpec((L, qk_w), lambda b, d, c: (row_blk(b, d, c), 1)),
            pl.BlockSpec((L, D_MODEL), lambda b, d, c: (row_blk(b, d, c), 1)),
            pl.BlockSpec((M_HEADS, L), lambda b, d, c: (2 * d, row_blk(b, d, c))),
            pl.BlockSpec((M_HEADS, L), lambda b, d, c: (2 * d + 1, row_blk(b, d, c))),
        ],
        out_specs=pl.BlockSpec((1, L, D_MODEL), lambda b, d, c: (d, row_blk(b, d, c), 0)),
        out_shape=jax.ShapeDtypeStruct((2, nt_rows, D_MODEL), BF16),
        scratch_shapes=[pltpu.VMEM((M_HEADS, M_QK_DIM, M_V_DIM), F32),
                        pltpu.VMEM((M_HEADS, 1, M_QK_DIM), F32),
                        pltpu.VMEM((M_HEADS, 1, LANES), F32)],
        compiler_params=_cparams(3),
        name="mlstm_scan",
    )(proj, proj, proj, gates_t, gates_t)


def _mlstm_finish_kernel(hf_ref, hb_ref, og_ref, g_ref, o_ref):
    for h in range(M_HEADS):
        sl = slice(h * M_V_DIM, (h + 1) * M_V_DIM)
        x = hf_ref[0, :, sl].astype(F32) + hb_ref[0, :, sl].astype(F32)
        y = x * lax.rsqrt(jnp.mean(x * x, axis=-1, keepdims=True) + RMS_EPS) * g_ref[:, sl]
        o_ref[:, sl] = (y * jax.nn.sigmoid(og_ref[:, sl].astype(F32))).astype(o_ref.dtype)


def mlstm_finish(hdirs, proj, head_norm):
    _, n, d = hdirs.shape
    tm = _pick(n, 256)
    return pl.pallas_call(
        _mlstm_finish_kernel,
        grid=(n // tm,),
        in_specs=[pl.BlockSpec((1, tm, d), lambda i: (0, i, 0)),
                  pl.BlockSpec((1, tm, d), lambda i: (1, i, 0)),
                  pl.BlockSpec((tm, d), lambda i: (i, 2)),
                  pl.BlockSpec((1, d), lambda i: (0, 0))],
        out_specs=pl.BlockSpec((tm, d), lambda i: (i, 0)),
        out_shape=jax.ShapeDtypeStruct((n, d), BF16),
        compiler_params=_cparams(1),
        name="mlstm_finish",
    )(hdirs, hdirs, proj, head_norm.reshape(1, d))


def _route(a, w_t, bias):
    tm = a.shape[0]
    per = N_EXPERTS // N_GROUPS
    nt = (((1,), (1,)), ((), ()))
    w_hi = w_t.astype(BF16)
    w_lo = (w_t - w_hi.astype(F32)).astype(BF16)
    a_hi = a.astype(BF16)
    a_lo = (a - a_hi.astype(F32)).astype(BF16)
    logits = (lax.dot_general(w_hi, a_hi, nt, preferred_element_type=F32)
              + lax.dot_general(w_lo, a_hi, nt, preferred_element_type=F32)
              + lax.dot_general(w_hi, a_lo, nt, preferred_element_type=F32))
    scores = jax.nn.sigmoid(logits).reshape(N_GROUPS, per, tm)
    biased = scores + bias.reshape(N_GROUPS, per, 1)
    e_iota = lax.broadcasted_iota(jnp.int32, (N_GROUPS, per, tm), 1).astype(F32)
    g_iota = lax.broadcasted_iota(jnp.int32, (N_GROUPS, 1, tm), 0).astype(F32)
    lin_iota = lax.broadcasted_iota(jnp.int32, (N_GROUPS, per, tm), 0).astype(F32) * per + e_iota
    m1 = jnp.max(biased, axis=1, keepdims=True)
    i1 = jnp.min(jnp.where(biased == m1, e_iota, float(per)), axis=1, keepdims=True)
    m2 = jnp.max(jnp.where(e_iota == i1, NEG_INF, biased), axis=1, keepdims=True)
    gscore = m1 + m2
    gsel = jnp.zeros(gscore.shape, F32)
    for _ in range(TOPK_GROUPS):
        cur = jnp.where(gsel > 0.0, NEG_INF, gscore)
        gm = jnp.max(cur, axis=0, keepdims=True)
        gi = jnp.min(jnp.where(cur == gm, g_iota, float(N_GROUPS)), axis=0, keepdims=True)
        gsel = jnp.where(g_iota == gi, 1.0, gsel)
    cand = jnp.where(gsel > 0.0, biased, NEG_INF)
    sel = jnp.zeros(cand.shape, F32)
    for _ in range(TOP_K):
        cur = jnp.where(sel > 0.0, NEG_INF, cand)
        em = jnp.max(jnp.max(cur, axis=1, keepdims=True), axis=0, keepdims=True)
        hit = jnp.where(cur == em, lin_iota, float(N_EXPERTS))
        ei = jnp.min(jnp.min(hit, axis=1, keepdims=True), axis=0, keepdims=True)
        sel = jnp.where(lin_iota == ei, 1.0, sel)
    wsel = jnp.where(sel > 0.0, scores, 0.0)
    wsum = jnp.sum(jnp.sum(wsel, axis=1, keepdims=True), axis=0, keepdims=True)
    comb = (wsel / wsum * ROUTED_SCALE).reshape(N_EXPERTS, tm)
    comb = jnp.concatenate([comb, jnp.ones((LANES - N_EXPERTS, tm), F32)], axis=0)
    return jnp.transpose(comb)


def _moe_dense_kernel(x_ref, gu_ref, dn_ref, comb_ref, o_ref):
    e = pl.program_id(1)

    @pl.when(e == 0)
    def _():
        o_ref[...] = jnp.zeros_like(o_ref)

    gu = jnp.dot(x_ref[...], gu_ref[0], preferred_element_type=F32)
    g = gu[:, :D_EXPERT]
    u = gu[:, D_EXPERT:]
    comb = comb_ref[...]
    lane = lax.broadcasted_iota(jnp.int32, comb.shape, 1)
    cw = jnp.sum(jnp.where(lane == e, comb, 0.0), axis=1, keepdims=True)
    act = (g * jax.nn.sigmoid(g) * u * cw).astype(BF16)
    o_ref[...] += jnp.dot(act, dn_ref[0], preferred_element_type=F32)


def moe_dense(f, comb, gu_all, dn_all):
    n, k = f.shape
    ne = gu_all.shape[0]
    tm = _pick(n, 512)
    return pl.pallas_call(
        _moe_dense_kernel,
        grid=(n // tm, ne),
        in_specs=[pl.BlockSpec((tm, k), lambda i, e: (i, 0)),
                  pl.BlockSpec((1, k, 2 * D_EXPERT), lambda i, e: (e, 0, 0)),
                  pl.BlockSpec((1, D_EXPERT, k), lambda i, e: (e, 0, 0)),
                  pl.BlockSpec((tm, LANES), lambda i, e: (i, 0))],
        out_specs=pl.BlockSpec((tm, k), lambda i, e: (i, 0)),
        out_shape=jax.ShapeDtypeStruct((n, k), F32),
        compiler_params=_cparams(2),
        name="moe_dense",
    )(f, gu_all, dn_all, comb)


def _rope_tables(seq, n_rows):
    pos = jnp.arange(seq)
    row = (pos // GRID_W).astype(F32)
    col = (pos % GRID_W).astype(F32)
    n_freq = HEAD_DIM // 4
    inv_freq = ROPE_THETA ** (-jnp.arange(n_freq, dtype=F32) / n_freq)
    ang = jnp.concatenate([row[:, None] * inv_freq, col[:, None] * inv_freq], axis=-1)
    cos, sin = jnp.cos(ang), jnp.sin(ang)
    cos_full = jnp.concatenate([cos, cos], axis=-1)
    sin_full = jnp.concatenate([-sin, sin], axis=-1)
    return cos_full, sin_full


def _trunk(x, c, ctx, c_ctx, layers, final_norm):
    bsz, seq, d = x.shape
    n_ctx = ctx.shape[1]
    dims = (bsz, seq, n_ctx)
    n_lat = bsz * seq
    n_all = n_lat + bsz * n_ctx
    depth = len(layers)

    seg = _Segments(seq, bsz, bsz * n_ctx)

    h = jnp.concatenate([x.reshape(n_lat, d), ctx.reshape(bsz * n_ctx, d)], axis=0)
    mod_rows = -(-(bsz + 1) // 8) * 8
    cond = jnp.zeros((mod_rows, d), F32).at[:bsz].set(c).at[bsz].set(c_ctx)

    cos1, sin1 = _rope_tables(seq, n_all)
    cos_t = jnp.concatenate([jnp.tile(cos1, (bsz, 1)), jnp.ones((bsz * n_ctx, HEAD_DIM), F32)], axis=0)
    sin_t = jnp.concatenate([jnp.tile(sin1, (bsz, 1)), jnp.zeros((bsz * n_ctx, HEAD_DIM), F32)], axis=0)
    ones_hd = jnp.ones((HEAD_DIM,), F32)

    tables = [ada_table(cond, *layer[1]) for layer in layers]
    a, = rowwise(h, layers[0][2], seg, shift=(tables[0], 0))
    for li, (kind, ada, norm1, mixer, norm2, moe) in enumerate(layers):
        need_ctx = li < depth - 1
        rows_out = n_all if need_ctx else n_lat
        mods = tables[li]
        resid = (h, mods, 2, seg)
        if kind == "mlstm":
            w_in, gate_b, head_norm, w_o = mixer
            n_main = 2 * M_HEADS * M_QK_DIM + M_HEADS * M_V_DIM + D_MODEL
            proj = matmul(a, w_in[:, :n_main].astype(BF16), scale_tiles=M_HEADS * M_QK_DIM // 512,
                          scale=M_QK_DIM ** -0.5)
            n_gate = 4 * M_HEADS
            w_gate = jnp.zeros((d, LANES), BF16).at[:, :n_gate].set(w_in[:, n_main:].astype(BF16))
            gate_b_pad = jnp.zeros((1, LANES), F32).at[0, :n_gate].set(gate_b.astype(F32))
            gates_t = mlstm_gates(a, w_gate, gate_b_pad)
            hdirs = mlstm_scan(proj, gates_t, dims)
            mixed = mlstm_finish(hdirs, proj, head_norm)
            h = matmul(mixed, w_o.astype(BF16), resid=resid, rows=rows_out)
        else:
            if kind == "global":
                w_qkv, q_norm, k_norm, w_o = mixer
                qkv = matmul(a, w_qkv.astype(BF16), qkv=(cos_t, sin_t, q_norm, k_norm, True))
                sink = None
            else:
                w_qkv, sink, w_o = mixer
                qkv = matmul(a, w_qkv.astype(BF16), qkv=(cos_t, sin_t, ones_hd, ones_hd, False))
            mixed = attention(qkv, dims, lat_queries=True, window=kind == "swa", sink=sink)
            if need_ctx:
                mixed_ctx = attention(qkv, dims, lat_queries=False, window=False, sink=sink)
                mixed = jnp.concatenate([mixed, mixed_ctx], axis=0)
            h = matmul(mixed, w_o.astype(BF16), resid=resid, rows=rows_out)
        router_w, router_b, exp_gu, exp_down, shared_gu, shared_down = moe
        f, comb = rowwise(h, norm2, seg, shift=(mods, 3), rows=rows_out, route=(router_w.T, router_b))
        gu_all = jnp.concatenate([exp_gu.astype(BF16), shared_gu.astype(BF16)[None]], axis=0)
        dn_all = jnp.concatenate([exp_down.astype(BF16), shared_down.astype(BF16)[None]], axis=0)
        y = moe_dense(f, comb, gu_all, dn_all)
        if li + 1 < depth:
            h, a = rowwise(h, layers[li + 1][2], seg, resid=(y, mods, 5), shift=(tables[li + 1], 0), rows=rows_out)
        else:
            _h, out = rowwise(h, final_norm, seg, resid=(y, mods, 5), rows=rows_out, out_dtype=F32)
    return out.reshape(bsz, seq, d)


def kernel(x, c, ctx, c_ctx, l0_ada_down, l0_ada_up, l0_ada_b, l0_norm1, l0_attn_qkv, l0_q_norm, l0_k_norm, l0_attn_o, l0_norm2, l0_router_w, l0_router_b, l0_exp_gu, l0_exp_down, l0_shared_gu, l0_shared_down, l1_ada_down, l1_ada_up, l1_ada_b, l1_norm1, l1_mlstm_in, l1_mlstm_gate_b, l1_mlstm_head_norm, l1_mlstm_o, l1_norm2, l1_router_w, l1_router_b, l1_exp_gu, l1_exp_down, l1_shared_gu, l1_shared_down, l2_ada_down, l2_ada_up, l2_ada_b, l2_norm1, l2_swa_qkv, l2_swa_sink, l2_swa_o, l2_norm2, l2_router_w, l2_router_b, l2_exp_gu, l2_exp_down, l2_shared_gu, l2_shared_down, l3_ada_down, l3_ada_up, l3_ada_b, l3_norm1, l3_attn_qkv, l3_q_norm, l3_k_norm, l3_attn_o, l3_norm2, l3_router_w, l3_router_b, l3_exp_gu, l3_exp_down, l3_shared_gu, l3_shared_down, final_norm):
    layers = [
        ("global", (l0_ada_down, l0_ada_up, l0_ada_b), l0_norm1, (l0_attn_qkv, l0_q_norm, l0_k_norm, l0_attn_o), l0_norm2,
         (l0_router_w, l0_router_b, l0_exp_gu, l0_exp_down, l0_shared_gu, l0_shared_down)),
        ("mlstm", (l1_ada_down, l1_ada_up, l1_ada_b), l1_norm1, (l1_mlstm_in, l1_mlstm_gate_b, l1_mlstm_head_norm, l1_mlstm_o), l1_norm2,
         (l1_router_w, l1_router_b, l1_exp_gu, l1_exp_down, l1_shared_gu, l1_shared_down)),
        ("swa", (l2_ada_down, l2_ada_up, l2_ada_b), l2_norm1, (l2_swa_qkv, l2_swa_sink, l2_swa_o), l2_norm2,
         (l2_router_w, l2_router_b, l2_exp_gu, l2_exp_down, l2_shared_gu, l2_shared_down)),
        ("global", (l3_ada_down, l3_ada_up, l3_ada_b), l3_norm1, (l3_attn_qkv, l3_q_norm, l3_k_norm, l3_attn_o), l3_norm2,
         (l3_router_w, l3_router_b, l3_exp_gu, l3_exp_down, l3_shared_gu, l3_shared_down)),
    ]
    return _trunk(x, c, ctx, c_ctx, layers, final_norm)
```

```python
import functools
import math
from typing import NamedTuple

import jax
import jax.numpy as jnp
from jax import lax
from jax.experimental import pallas as pl
from jax.experimental.pallas import tpu as pltpu

F32 = jnp.float32
BF16 = jnp.bfloat16
U32 = jnp.uint32
I32 = jnp.int32

D_MODEL = 4096
GRID_W = 64
RMS_EPS = 1e-6
N_MOD = 6
N_HEADS = 32
N_KV_HEADS = 8
HEAD_DIM = D_MODEL // N_HEADS
KV_GROUP = N_HEADS // N_KV_HEADS
ROPE_THETA = 10000.0
WINDOW = 128
M_HEADS = 8
M_V_DIM = D_MODEL // M_HEADS
M_QK_DIM = M_V_DIM // 2
M_CHUNK = 128
GATE_CAP = 15.0
N_EXPERTS = 64
TOP_K = 8
N_GROUPS = 8
TOPK_GROUPS = 4
D_EXPERT = 192
ROUTED_SCALE = 2.5

MOE_TM = 512
LANES = 128
VMEM_LIMIT = 56 * 1024 * 1024
NEG_INF = float("-inf")


def _cparams(n_axes):
    return pltpu.CompilerParams(dimension_semantics=("arbitrary",) * n_axes,
                                vmem_limit_bytes=VMEM_LIMIT)


def _pick(n, pref):
    t = pref
    while n % t:
        t //= 2
    return t


class _Segments(NamedTuple):
    seq: int
    bsz: int
    n_ctx_rows: int

    @property
    def tile_unit(self):
        return math.gcd(self.seq, self.n_ctx_rows)

    def of_row(self, row):
        return jnp.minimum(row // self.seq, self.bsz)


def _ada_kernel(cond_ref, down_ref, up_ref, b_ref, out_ref):
    c = cond_ref[...]
    t = jnp.dot(c * jax.nn.sigmoid(c), down_ref[...], precision=lax.Precision.HIGHEST,
                preferred_element_type=F32)
    out_ref[...] = jnp.dot(t, up_ref[...], precision=lax.Precision.HIGHEST,
                           preferred_element_type=F32) + b_ref[...]


def ada_table(cond_pad, down, up, bias):
    r, d = cond_pad.shape
    rank = down.shape[1]
    n = up.shape[1]
    tn = 2048
    out = pl.pallas_call(
        _ada_kernel,
        grid=(n // tn,),
        in_specs=[pl.BlockSpec((r, d), lambda j: (0, 0)),
                  pl.BlockSpec((d, rank), lambda j: (0, 0)),
                  pl.BlockSpec((rank, tn), lambda j: (0, j)),
                  pl.BlockSpec((1, tn), lambda j: (0, j))],
        out_specs=pl.BlockSpec((r, tn), lambda j: (0, j)),
        out_shape=jax.ShapeDtypeStruct((r, n), F32),
        compiler_params=_cparams(1),
        name="ada_table",
    )(cond_pad, down, up, bias.reshape(1, n))
    return out.reshape(r, N_MOD, d)


def _rowwise_kernel(*refs, gate_idx, shift_idx, route):
    it = iter(refs)
    h_ref = next(it)
    y_ref, gmod_ref = (next(it), next(it)) if gate_idx is not None else (None, None)
    g_ref = next(it)
    smod_ref = next(it) if shift_idx is not None else None
    wt_ref, rb_ref = (next(it), next(it)) if route else (None, None)
    h = h_ref[...]
    if gate_idx is not None:
        h = h + gmod_ref[0, gate_idx:gate_idx + 1, :] * y_ref[...].astype(F32)
        next(it)[...] = h
    a = h * lax.rsqrt(jnp.mean(h * h, axis=-1, keepdims=True) + RMS_EPS) * g_ref[...]
    if shift_idx is not None:
        a = a * (1.0 + smod_ref[0, shift_idx + 1:shift_idx + 2, :]) + smod_ref[0, shift_idx:shift_idx + 1, :]
    if not route:
        a_ref = next(it)
        a_ref[...] = a.astype(a_ref.dtype)
        return
    packed_ref, ids_ref, rank_ref, w_ref, cnt_ref = (next(it) for _ in range(5))

    @pl.when(pl.program_id(0) == 0)
    def _():
        cnt_ref[...] = jnp.zeros_like(cnt_ref)

    packed_ref[...] = _pack_rows(a)
    ids, ranks, wmat, counts = _route(a, wt_ref[...], rb_ref[...], cnt_ref[:, 0:1])
    ids_ref[...] = ids
    rank_ref[...] = ranks
    w_ref[...] = wmat
    cnt_ref[...] = jnp.broadcast_to(counts, cnt_ref.shape)


def rowwise(h, gain, seg, *, resid=None, shift=None, out_dtype=BF16, rows=None, route=None):
    n, d = h.shape
    rows = n if rows is None else rows
    tm = _pick(seg.tile_unit, 256)
    row_spec = pl.BlockSpec((tm, d), lambda i: (i, 0))
    mod_spec = pl.BlockSpec((1, N_MOD, d), lambda i: (seg.of_row(i * tm), 0, 0))
    in_specs, args = [row_spec], [h]
    if resid is not None:
        in_specs += [row_spec, mod_spec]
        args += [resid[0], resid[1]]
    in_specs.append(pl.BlockSpec((1, d), lambda i: (0, 0)))
    args.append(gain.reshape(1, d))
    if shift is not None:
        in_specs.append(mod_spec)
        args.append(shift[0])
    if route is not None:
        in_specs += [pl.BlockSpec((N_EXPERTS, d), lambda i: (0, 0)),
                     pl.BlockSpec((N_EXPERTS, 1), lambda i: (0, 0))]
        args += [route[0], route[1].reshape(N_EXPERTS, 1)]
    out_specs, out_shape = [], []
    if resid is not None:
        out_specs.append(row_spec)
        out_shape.append(jax.ShapeDtypeStruct((rows, d), F32))
    if route is None:
        out_specs.append(row_spec)
        out_shape.append(jax.ShapeDtypeStruct((rows, d), out_dtype))
    else:
        out_specs += [pl.BlockSpec((tm, d // 2), lambda i: (i, 0)),
                      pl.BlockSpec((TOP_K, tm), lambda i: (0, i)),
                      pl.BlockSpec((TOP_K, tm), lambda i: (0, i)),
                      pl.BlockSpec((tm, LANES), lambda i: (i, 0)),
                      pl.BlockSpec((N_EXPERTS, LANES), lambda i: (0, 0))]
        out_shape += [jax.ShapeDtypeStruct((rows, d // 2), U32),
                      jax.ShapeDtypeStruct((TOP_K, rows), I32),
                      jax.ShapeDtypeStruct((TOP_K, rows), I32),
                      jax.ShapeDtypeStruct((rows, LANES), F32),
                      jax.ShapeDtypeStruct((N_EXPERTS, LANES), F32)]
    return pl.pallas_call(
        functools.partial(_rowwise_kernel, gate_idx=None if resid is None else resid[2],
                          shift_idx=None if shift is None else shift[1], route=route is not None),
        grid=(rows // tm,),
        in_specs=in_specs, out_specs=out_specs, out_shape=out_shape,
        compiler_params=_cparams(1),
        name="rowwise",
    )(*args)


def _mm_plain_kernel(x_ref, w_ref, o_ref, *, scale_tiles, scale):
    acc = jnp.dot(x_ref[...], w_ref[...], preferred_element_type=F32)
    if scale_tiles:
        acc = acc * jnp.where(pl.program_id(1) < scale_tiles, scale, 1.0)
    o_ref[...] = acc.astype(o_ref.dtype)


def _mm_resid_kernel(x_ref, w_ref, h_ref, mod_ref, o_ref, *, gate_idx):
    acc = jnp.dot(x_ref[...], w_ref[...], preferred_element_type=F32)
    o_ref[...] = h_ref[...] + mod_ref[0, gate_idx:gate_idx + 1, :] * acc


def _mm_qkv_kernel(x_ref, w_ref, cos_ref, sin_ref, qn_ref, kn_ref, o_ref, *, nq_tiles, nk_tiles, qk_norm):
    acc = jnp.dot(x_ref[...], w_ref[...], preferred_element_type=F32)
    j = pl.program_id(1)

    @pl.when(j >= nq_tiles + nk_tiles)
    def _():
        o_ref[...] = acc.astype(o_ref.dtype)

    @pl.when(j < nq_tiles + nk_tiles)
    def _():
        is_q = j < nq_tiles
        post = jnp.where(is_q, HEAD_DIM ** -0.5, 1.0)
        gain = jnp.where(is_q, qn_ref[...], kn_ref[...])
        cos = cos_ref[...]
        sin = sin_ref[...]
        for s in range(0, acc.shape[1], HEAD_DIM):
            xh = acc[:, s:s + HEAD_DIM]
            if qk_norm:
                xh = xh * lax.rsqrt(jnp.mean(xh * xh, axis=-1, keepdims=True) + RMS_EPS) * gain
            xh = xh * cos + pltpu.roll(xh, HEAD_DIM // 2, axis=1) * sin
            o_ref[:, s:s + HEAD_DIM] = (xh * post).astype(o_ref.dtype)


def matmul(x, w, *, tm=1024, tn=512, out_dtype=BF16, rows=None, scale_tiles=0, scale=1.0,
           resid=None, qkv=None):
    m, k = x.shape
    m = m if rows is None else rows
    n = w.shape[1]
    tm = _pick(m if resid is None else math.gcd(m, resid[3].tile_unit), tm)
    tn = _pick(n, tn)
    grid = (m // tm, n // tn)
    x_spec = pl.BlockSpec((tm, k), lambda i, j: (i, 0))
    w_spec = pl.BlockSpec((k, tn), lambda i, j: (0, j))
    o_spec = pl.BlockSpec((tm, tn), lambda i, j: (i, j))
    if resid is not None:
        h, mods, gate_idx, seg = resid
        kern = functools.partial(_mm_resid_kernel, gate_idx=gate_idx)
        in_specs = [x_spec, w_spec, o_spec,
                    pl.BlockSpec((1, N_MOD, tn), lambda i, j: (seg.of_row(i * tm), 0, j))]
        args = (x, w, h, mods)
        out_dtype = F32
    elif qkv is not None:
        cos, sin, qn, kn, qk_norm = qkv
        kern = functools.partial(_mm_qkv_kernel, nq_tiles=N_HEADS * HEAD_DIM // tn,
                                 nk_tiles=N_KV_HEADS * HEAD_DIM // tn, qk_norm=qk_norm)
        tab_spec = pl.BlockSpec((tm, HEAD_DIM), lambda i, j: (i, 0))
        vec_spec = pl.BlockSpec((1, HEAD_DIM), lambda i, j: (0, 0))
        in_specs = [x_spec, w_spec, tab_spec, tab_spec, vec_spec, vec_spec]
        args = (x, w, cos, sin, qn.reshape(1, HEAD_DIM), kn.reshape(1, HEAD_DIM))
    else:
        kern = functools.partial(_mm_plain_kernel, scale_tiles=scale_tiles, scale=scale)
        in_specs = [x_spec, w_spec]
        args = (x, w)
    return pl.pallas_call(
        kern, grid=grid, in_specs=in_specs, out_specs=o_spec,
        out_shape=jax.ShapeDtypeStruct((m, n), out_dtype),
        compiler_params=_cparams(2),
        name="matmul",
    )(*args)


def _attn_kernel(*refs, tq, lat_keys, window, has_sink):
    it = iter(refs)
    sink_ref = next(it) if has_sink else None
    q_ref = next(it)
    kc_ref, vc_ref = next(it), next(it)
    kl_ref, vl_ref = (next(it), next(it)) if lat_keys else (None, None)
    o_ref = next(it)
    kvh = pl.program_id(1)
    qi = pl.program_id(2)
    rows = KV_GROUP * tq
    q = q_ref[...]
    qs = jnp.concatenate([q[:, g * HEAD_DIM:(g + 1) * HEAD_DIM] for g in range(KV_GROUP)], axis=0)
    nt = (((1,), (1,)), ((), ()))
    s_c = lax.dot_general(qs, kc_ref[...], nt, preferred_element_type=F32)
    m = jnp.max(s_c, axis=-1, keepdims=True)
    if lat_keys:
        if window:
            seq = kl_ref.shape[0]
            band = min(tq + 2 * WINDOW, seq)
            start = pl.multiple_of(jnp.clip(qi * tq - WINDOW, 0, seq - band), LANES)
            kl = kl_ref[pl.ds(start, band), :]
            vl = vl_ref[pl.ds(start, band), :]
            s_l = lax.dot_general(qs, kl, nt, preferred_element_type=F32)
            q_pos = qi * tq + lax.broadcasted_iota(jnp.int32, (rows, band), 0) % tq
            k_pos = start + lax.broadcasted_iota(jnp.int32, (rows, band), 1)
            s_l = jnp.where(jnp.abs(q_pos - k_pos) <= WINDOW, s_l, NEG_INF)
        else:
            kl = kl_ref[...]
            vl = vl_ref[...]
            s_l = lax.dot_general(qs, kl, nt, preferred_element_type=F32)
        m = jnp.maximum(m, jnp.max(s_l, axis=-1, keepdims=True))
    if has_sink:
        sink = jnp.concatenate(
            [jnp.full((tq, 1), sink_ref[kvh * KV_GROUP + g], F32) for g in range(KV_GROUP)], axis=0)
        m = jnp.maximum(m, sink)
    p_c = jnp.exp(s_c - m)
    den = jnp.sum(p_c, axis=-1, keepdims=True)
    acc = jnp.dot(p_c.astype(BF16), vc_ref[...], preferred_element_type=F32)
    if lat_keys:
        p_l = jnp.exp(s_l - m)
        den = den + jnp.sum(p_l, axis=-1, keepdims=True)
        acc = acc + jnp.dot(p_l.astype(BF16), vl, preferred_element_type=F32)
    if has_sink:
        den = den + jnp.exp(sink - m)
    out = acc / den
    for g in range(KV_GROUP):
        o_ref[:, g * HEAD_DIM:(g + 1) * HEAD_DIM] = out[g * tq:(g + 1) * tq].astype(o_ref.dtype)


def attention(qkv, dims, *, lat_queries, window, sink):
    bsz, seq, n_ctx = dims
    n_lat = bsz * seq
    g_cols = KV_GROUP * HEAD_DIM
    k_col0 = N_HEADS * HEAD_DIM // HEAD_DIM
    v_col0 = k_col0 + N_KV_HEADS
    has_sink = sink is not None
    if lat_queries:
        tq = _pick(seq, 256)
        q_tiles = seq // tq
        q_row0 = 0
        out_rows = n_lat
    else:
        tq = n_ctx
        q_tiles = 1
        q_row0 = n_lat // tq
        out_rows = bsz * n_ctx
    ctx_blk0 = n_lat // n_ctx
    in_specs = []
    args = []
    if has_sink:
        in_specs.append(pl.BlockSpec(memory_space=pltpu.SMEM))
        args.append(sink.astype(F32))
    in_specs += [
        pl.BlockSpec((tq, g_cols), lambda b, h, i: (q_row0 + b * q_tiles + i, h)),
        pl.BlockSpec((n_ctx, HEAD_DIM), lambda b, h, i: (ctx_blk0 + b, k_col0 + h)),
        pl.BlockSpec((n_ctx, HEAD_DIM), lambda b, h, i: (ctx_blk0 + b, v_col0 + h)),
    ]
    args += [qkv, qkv, qkv]
    if lat_queries:
        in_specs += [
            pl.BlockSpec((seq, HEAD_DIM), lambda b, h, i: (b, k_col0 + h)),
            pl.BlockSpec((seq, HEAD_DIM), lambda b, h, i: (b, v_col0 + h)),
        ]
        args += [qkv, qkv]
    return pl.pallas_call(
        functools.partial(_attn_kernel, tq=tq, lat_keys=lat_queries, window=window, has_sink=has_sink),
        grid=(bsz, N_KV_HEADS, q_tiles),
        in_specs=in_specs,
        out_specs=pl.BlockSpec((tq, g_cols), lambda b, h, i: (b * q_tiles + i, h)),
        out_shape=jax.ShapeDtypeStruct((out_rows, N_HEADS * HEAD_DIM), BF16),
        compiler_params=_cparams(3),
        name="attention",
    )(*args)


def _mlstm_gate_kernel(x_ref, w_ref, b_ref, o_ref):
    g = jnp.dot(x_ref[...], w_ref[...], preferred_element_type=F32) + b_ref[...]
    g = GATE_CAP * jnp.tanh(g / GATE_CAP)
    log_f = jnp.minimum(g, 0.0) - jnp.log(1.0 + jnp.exp(-jnp.abs(g)))
    col = lax.broadcasted_iota(jnp.int32, g.shape, 1)
    is_f = (col // M_HEADS) % 2 == 1
    o_ref[...] = jnp.transpose(jnp.where(is_f, log_f, g))


def mlstm_gates(a, w_gate_pad, gate_b_pad):
    n, k = a.shape
    tm = _pick(n, 1024)
    return pl.pallas_call(
        _mlstm_gate_kernel,
        grid=(n // tm,),
        in_specs=[pl.BlockSpec((tm, k), lambda i: (i, 0)),
                  pl.BlockSpec((k, LANES), lambda i: (0, 0)),
                  pl.BlockSpec((1, LANES), lambda i: (0, 0))],
        out_specs=pl.BlockSpec((LANES, tm), lambda i: (0, i)),
        out_shape=jax.ShapeDtypeStruct((LANES, n), F32),
        compiler_params=_cparams(1),
        name="mlstm_gates",
    )(a, w_gate_pad, gate_b_pad)


def _mlstm_kernel(q_ref, k_ref, v_ref, gi_ref, gf_ref, o_ref, ct_ref, n_ref, m_ref):
    d = pl.program_id(1)
    c = pl.program_id(2)
    L = M_CHUNK

    @pl.when(c == 0)
    def _():
        ct_ref[...] = jnp.zeros_like(ct_ref)
        n_ref[...] = jnp.zeros_like(n_ref)
        m_ref[...] = jnp.zeros_like(m_ref)

    sgn = 1 - 2 * d
    diff = (lax.broadcasted_iota(jnp.int32, (L, L), 0) - lax.broadcasted_iota(jnp.int32, (L, L), 1)) * sgn
    mask = diff >= 0
    mask_t = diff <= 0
    eye = diff == 0
    nt = (((1,), (1,)), ((), ()))
    tn = (((0,), (0,)), ((), ()))
    for h in range(M_HEADS):
        qc = q_ref[:, h * M_QK_DIM:(h + 1) * M_QK_DIM]
        kc = k_ref[:, h * M_QK_DIM:(h + 1) * M_QK_DIM]
        vc = v_ref[:, h * M_V_DIM:(h + 1) * M_V_DIM]
        i_row = gi_ref[h:h + 1, :]
        f_row = gf_ref[h:h + 1, :]
        m_prev = m_ref[h, :, 0:1]
        n_prev = n_ref[h]
        ct_prev = ct_ref[h]
        f_col = jnp.sum(jnp.where(eye, f_row, 0.0), axis=1, keepdims=True)
        i_col = jnp.sum(jnp.where(eye, i_row, 0.0), axis=1, keepdims=True)
        b_col = jnp.sum(jnp.where(mask, f_row, 0.0), axis=1, keepdims=True)
        b_row = jnp.sum(jnp.where(mask_t, f_col, 0.0), axis=0, keepdims=True)
        log_inter = b_col + m_prev
        log_intra = jnp.where(mask, b_col - b_row + i_row, NEG_INF)
        m_t = jnp.maximum(log_inter, jnp.max(log_intra, axis=1, keepdims=True))
        w_inter = jnp.exp(log_inter - m_t)
        s_qk = lax.dot_general(qc, kc, nt, preferred_element_type=F32) * jnp.exp(log_intra - m_t)
        num = (jnp.dot(s_qk.astype(BF16), vc, preferred_element_type=F32)
               + w_inter * jnp.dot(qc, ct_prev.astype(BF16), preferred_element_type=F32))
        den = (jnp.sum(s_qk, axis=1, keepdims=True)
               + w_inter * jnp.sum(qc.astype(F32) * n_prev, axis=1, keepdims=True))
        h_out = num / jnp.maximum(jnp.abs(den), jnp.exp(-m_t))
        o_ref[0, :, h * M_V_DIM:(h + 1) * M_V_DIM] = h_out.astype(o_ref.dtype)
        b_tot = jnp.sum(f_row, axis=1, keepdims=True)
        log_w = b_tot - b_col + i_col
        m_new = jnp.maximum(b_tot + m_prev, jnp.max(log_w, axis=0, keepdims=True))
        decay = jnp.exp(b_tot + m_prev - m_new)
        w_k = jnp.exp(log_w - m_new)
        vw = (vc.astype(F32) * w_k).astype(BF16)
        ct_ref[h] = decay * ct_prev + lax.dot_general(kc, vw, tn, preferred_element_type=F32)
        n_ref[h] = decay * n_prev + jnp.sum(kc.astype(F32) * w_k, axis=0, keepdims=True)
        m_ref[h] = jnp.broadcast_to(m_new, (1, LANES))


def mlstm_scan(proj, gates_t, dims):
    bsz, seq, n_ctx = dims
    nt_rows = proj.shape[0]
    L = M_CHUNK
    ncc, nlc = n_ctx // L, seq // L
    lat_blocks = bsz * nlc
    qk_w = M_HEADS * M_QK_DIM

    def row_blk(b, d, c):
        cc = jnp.where(d == 0, c, ncc - 1 - c)
        lc = jnp.where(d == 0, c - ncc, nlc - 1 - (c - ncc))
        return jnp.where(c < ncc, lat_blocks + b * ncc + cc, b * nlc + lc)

    return pl.pallas_call(
        _mlstm_kernel,
        grid=(bsz, 2, ncc + nlc),
        in_specs=[
            pl.BlockSpec((L, qk_w), lambda b, d, c: (row_blk(b, d, c), 0)),
            pl.BlockSpec((L, qk_w), lambda b, d, c: (row_blk(b, d, c), 1)),
            pl.BlockSpec((L, D_MODEL), lambda b, d, c: (row_blk(b, d, c), 1)),
            pl.BlockSpec((M_HEADS, L), lambda b, d, c: (2 * d, row_blk(b, d, c))),
            pl.BlockSpec((M_HEADS, L), lambda b, d, c: (2 * d + 1, row_blk(b, d, c))),
        ],
        out_specs=pl.BlockSpec((1, L, D_MODEL), lambda b, d, c: (d, row_blk(b, d, c), 0)),
        out_shape=jax.ShapeDtypeStruct((2, nt_rows, D_MODEL), BF16),
        scratch_shapes=[pltpu.VMEM((M_HEADS, M_QK_DIM, M_V_DIM), F32),
                        pltpu.VMEM((M_HEADS, 1, M_QK_DIM), F32),
                        pltpu.VMEM((M_HEADS, 1, LANES), F32)],
        compiler_params=_cparams(3),
        name="mlstm_scan",
    )(proj, proj, proj, gates_t, gates_t)


def _mlstm_finish_kernel(hf_ref, hb_ref, og_ref, g_ref, o_ref):
    for h in range(M_HEADS):
        sl = slice(h * M_V_DIM, (h + 1) * M_V_DIM)
        x = hf_ref[0, :, sl].astype(F32) + hb_ref[0, :, sl].astype(F32)
        y = x * lax.rsqrt(jnp.mean(x * x, axis=-1, keepdims=True) + RMS_EPS) * g_ref[:, sl]
        o_ref[:, sl] = (y * jax.nn.sigmoid(og_ref[:, sl].astype(F32))).astype(o_ref.dtype)


def mlstm_finish(hdirs, proj, head_norm):
    _, n, d = hdirs.shape
    tm = _pick(n, 256)
    return pl.pallas_call(
        _mlstm_finish_kernel,
        grid=(n // tm,),
        in_specs=[pl.BlockSpec((1, tm, d), lambda i: (0, i, 0)),
                  pl.BlockSpec((1, tm, d), lambda i: (1, i, 0)),
                  pl.BlockSpec((tm, d), lambda i: (i, 2)),
                  pl.BlockSpec((1, d), lambda i: (0, 0))],
        out_specs=pl.BlockSpec((tm, d), lambda i: (i, 0)),
        out_shape=jax.ShapeDtypeStruct((n, d), BF16),
        compiler_params=_cparams(1),
        name="mlstm_finish",
    )(hdirs, hdirs, proj, head_norm.reshape(1, d))


def _pack_rows(x):
    half = x.shape[1] // 2
    lo = lax.bitcast_convert_type(x[:, :half].astype(BF16).astype(F32), U32)
    hi = lax.bitcast_convert_type(x[:, half:].astype(BF16).astype(F32), U32)
    return (hi & jnp.uint32(0xFFFF0000)) | (lo >> 16)


def _unpack_rows(p):
    lo = lax.bitcast_convert_type(p << 16, F32)
    hi = lax.bitcast_convert_type(p & jnp.uint32(0xFFFF0000), F32)
    return lo, hi


def _route(a, w_t, bias, counts):
    tm = a.shape[0]
    per = N_EXPERTS // N_GROUPS
    nt = (((1,), (1,)), ((), ()))
    w_hi = w_t.astype(BF16)
    w_lo = (w_t - w_hi.astype(F32)).astype(BF16)
    a_hi = a.astype(BF16)
    a_lo = (a - a_hi.astype(F32)).astype(BF16)
    logits = (lax.dot_general(w_hi, a_hi, nt, preferred_element_type=F32)
              + lax.dot_general(w_lo, a_hi, nt, preferred_element_type=F32)
              + lax.dot_general(w_hi, a_lo, nt, preferred_element_type=F32))
    scores = jax.nn.sigmoid(logits).reshape(N_GROUPS, per, tm)
    biased = scores + bias.reshape(N_GROUPS, per, 1)
    e_iota = lax.broadcasted_iota(jnp.int32, (N_GROUPS, per, tm), 1).astype(F32)
    g_iota = lax.broadcasted_iota(jnp.int32, (N_GROUPS, 1, tm), 0).astype(F32)
    lin_iota = lax.broadcasted_iota(jnp.int32, (N_GROUPS, per, tm), 0).astype(F32) * per + e_iota
    m1 = jnp.max(biased, axis=1, keepdims=True)
    i1 = jnp.min(jnp.where(biased == m1, e_iota, float(per)), axis=1, keepdims=True)
    m2 = jnp.max(jnp.where(e_iota == i1, NEG_INF, biased), axis=1, keepdims=True)
    gscore = m1 + m2
    gsel = jnp.zeros(gscore.shape, F32)
    for _ in range(TOPK_GROUPS):
        cur = jnp.where(gsel > 0.0, NEG_INF, gscore)
        gm = jnp.max(cur, axis=0, keepdims=True)
        gi = jnp.min(jnp.where(cur == gm, g_iota, float(N_GROUPS)), axis=0, keepdims=True)
        gsel = jnp.where(g_iota == gi, 1.0, gsel)
    cand = jnp.where(gsel > 0.0, biased, NEG_INF)
    sel = jnp.zeros(cand.shape, F32)
    picks = []
    for _ in range(TOP_K):
        cur = jnp.where(sel > 0.0, NEG_INF, cand)
        em = jnp.max(jnp.max(cur, axis=1, keepdims=True), axis=0, keepdims=True)
        hit = jnp.where(cur == em, lin_iota, float(N_EXPERTS))
        ei = jnp.min(jnp.min(hit, axis=1, keepdims=True), axis=0, keepdims=True)
        sel = jnp.where(lin_iota == ei, 1.0, sel)
        picks.append(ei)
    sel2 = sel.reshape(N_EXPERTS, tm)
    before = lax.broadcasted_iota(jnp.int32, (tm, tm), 0) < lax.broadcasted_iota(jnp.int32, (tm, tm), 1)
    rank = jnp.dot(sel2.astype(BF16), jnp.where(before, 1.0, 0.0).astype(BF16),
                   preferred_element_type=F32) + counts
    new_counts = counts + jnp.sum(sel2, axis=1, keepdims=True)
    rank3 = rank.reshape(N_GROUPS, per, tm)
    wsum = jnp.sum(jnp.sum(jnp.where(sel > 0.0, scores, 0.0), axis=1, keepdims=True), axis=0, keepdims=True)

    def pick(ei, table):
        v = jnp.where(lin_iota == ei, table, 0.0)
        return jnp.sum(jnp.sum(v, axis=1, keepdims=True), axis=0, keepdims=True).reshape(1, tm)

    ids = jnp.concatenate([ei.reshape(1, tm) for ei in picks], axis=0).astype(I32)
    ranks = jnp.concatenate([pick(ei, rank3) for ei in picks], axis=0).astype(I32)
    w = jnp.concatenate([pick(ei, scores) for ei in picks], axis=0) / wsum.reshape(1, tm) * ROUTED_SCALE
    wmat = jnp.transpose(jnp.concatenate([w, jnp.zeros((LANES - TOP_K, tm), F32)], axis=0))
    return ids, ranks, wmat, new_counts


def _dest_kernel(starts_ref, ids_ref, rank_ref, dest_ref):
    ids = ids_ref[...]
    acc = rank_ref[...]
    for e in range(N_EXPERTS):
        acc = acc + jnp.where(ids == e, starts_ref[e], 0)
    dest_ref[...] = acc


def dest_rows(starts, ids, ranks):
    return pl.pallas_call(
        _dest_kernel,
        in_specs=[pl.BlockSpec(memory_space=pltpu.SMEM), pl.BlockSpec(memory_space=pltpu.VMEM),
                  pl.BlockSpec(memory_space=pltpu.VMEM)],
        out_specs=pl.BlockSpec(memory_space=pltpu.VMEM),
        out_shape=jax.ShapeDtypeStruct(ids.shape, I32),
        compiler_params=pltpu.CompilerParams(vmem_limit_bytes=VMEM_LIMIT),
        name="dest_rows",
    )(starts, ids, ranks)


def _dispatch_kernel(starts_ref, ends_ref, dest_ref, f_ref, xs_ref, zero_ref, sem):
    tm = f_ref.shape[0]

    def zero_copy(e):
        row0 = pl.multiple_of(ends_ref[e] - MOE_TM, MOE_TM)
        return pltpu.make_async_copy(zero_ref, xs_ref.at[pl.ds(row0, MOE_TM)], sem)

    @pl.when(pl.program_id(0) == 0)
    def _():
        zero_ref[...] = jnp.zeros_like(zero_ref)
        for e in range(N_EXPERTS):
            @pl.when(ends_ref[e] > starts_ref[e])
            def _():
                zero_copy(e).start()
        for e in range(N_EXPERTS):
            @pl.when(ends_ref[e] > starts_ref[e])
            def _():
                zero_copy(e).wait()

    def row_copy(t, d):
        return pltpu.make_async_copy(f_ref.at[pl.ds(t, 1)], xs_ref.at[pl.ds(d, 1)], sem)

    def issue(t, c):
        for k in range(TOP_K):
            row_copy(t, dest_ref[k, t]).start()
        return c

    def drain(t, c):
        for k in range(TOP_K):
            row_copy(0, 0).wait()
        return c

    lax.fori_loop(0, tm, issue, 0)
    lax.fori_loop(0, tm, drain, 0)


def dispatch(starts, ends, dest, packed, n_rows):
    n, half = packed.shape
    tm = _pick(n, 256)
    return pl.pallas_call(
        _dispatch_kernel,
        grid_spec=pltpu.PrefetchScalarGridSpec(
            num_scalar_prefetch=2, grid=(n // tm,),
            in_specs=[pl.BlockSpec((TOP_K, tm), lambda i, s, e: (0, i), memory_space=pltpu.SMEM),
                      pl.BlockSpec((tm, half), lambda i, s, e: (i, 0))],
            out_specs=pl.BlockSpec(memory_space=pl.ANY),
            scratch_shapes=[pltpu.VMEM((MOE_TM, half), U32), pltpu.SemaphoreType.DMA(())]),
        out_shape=jax.ShapeDtypeStruct((n_rows, half), U32),
        compiler_params=_cparams(1),
        name="dispatch",
    )(starts, ends, dest, packed)


def _swiglu_expert(x_packed, wg, wu, wd):
    half = x_packed.shape[1]
    lo, hi = _unpack_rows(x_packed)
    lo = lo.astype(BF16)
    hi = hi.astype(BF16)
    g = (jnp.dot(lo, wg[:half, :], preferred_element_type=F32)
         + jnp.dot(hi, wg[half:, :], preferred_element_type=F32))
    u = (jnp.dot(lo, wu[:half, :], preferred_element_type=F32)
         + jnp.dot(hi, wu[half:, :], preferred_element_type=F32))
    act = (g * jax.nn.sigmoid(g) * u).astype(BF16)
    return jnp.dot(act, wd, preferred_element_type=F32)


def _grouped_kernel(te_ref, nt_ref, x_ref, wg_ref, wu_ref, wd_ref, o_ref):
    @pl.when(pl.program_id(0) < nt_ref[0])
    def _():
        o_ref[...] = _pack_rows(_swiglu_expert(x_ref[...], wg_ref[0], wu_ref[0], wd_ref[0]))


def grouped_experts(tile_expert, n_tiles, xs, wg, wu, wd):
    rows, half = xs.shape
    d = 2 * half

    def tile(j, nt):
        return jnp.minimum(j, nt[0] - 1)

    return pl.pallas_call(
        _grouped_kernel,
        grid_spec=pltpu.PrefetchScalarGridSpec(
            num_scalar_prefetch=2, grid=(rows // MOE_TM,),
            in_specs=[pl.BlockSpec((MOE_TM, half), lambda j, te, nt: (tile(j, nt), 0)),
                      pl.BlockSpec((1, d, D_EXPERT), lambda j, te, nt: (te[tile(j, nt)], 0, 0)),
                      pl.BlockSpec((1, d, D_EXPERT), lambda j, te, nt: (te[tile(j, nt)], 0, 0)),
                      pl.BlockSpec((1, D_EXPERT, d), lambda j, te, nt: (te[tile(j, nt)], 0, 0))],
            out_specs=pl.BlockSpec((MOE_TM, half), lambda j, te, nt: (tile(j, nt), 0))),
        out_shape=jax.ShapeDtypeStruct((rows, half), U32),
        compiler_params=_cparams(1),
        name="grouped_experts",
    )(tile_expert, n_tiles, xs, wg, wu, wd)


def _shared_kernel(x_ref, wg_ref, wu_ref, wd_ref, o_ref):
    o_ref[...] = _swiglu_expert(x_ref[...], wg_ref[...], wu_ref[...], wd_ref[...]).astype(o_ref.dtype)


def shared_expert(packed, wg, wu, wd):
    n, half = packed.shape
    tm = _pick(n, 512)

    def whole(shape):
        return pl.BlockSpec(shape, lambda i: (0,) * len(shape))

    return pl.pallas_call(
        _shared_kernel,
        grid=(n // tm,),
        in_specs=[pl.BlockSpec((tm, half), lambda i: (i, 0)), whole(wg.shape), whole(wu.shape), whole(wd.shape)],
        out_specs=pl.BlockSpec((tm, 2 * half), lambda i: (i, 0)),
        out_shape=jax.ShapeDtypeStruct((n, 2 * half), BF16),
        compiler_params=_cparams(1),
        name="shared_expert",
    )(packed, wg, wu, wd)


def _combine_kernel(dest_ref, ys_ref, w_ref, sh_ref, o_ref, buf_ref, sem):
    tm, d = sh_ref.shape
    half = d // 2

    def row_copy(k, t, src):
        return pltpu.make_async_copy(ys_ref.at[pl.ds(src, 1)], buf_ref.at[k, pl.ds(t, 1)], sem)

    def issue(t, c):
        for k in range(TOP_K):
            row_copy(k, t, dest_ref[k, t]).start()
        return c

    def drain(t, c):
        for k in range(TOP_K):
            row_copy(0, 0, 0).wait()
        return c

    lax.fori_loop(0, tm, issue, 0)
    lax.fori_loop(0, tm, drain, 0)
    w = w_ref[...]
    y_lo = sh_ref[:, :half].astype(F32)
    y_hi = sh_ref[:, half:].astype(F32)
    for k in range(TOP_K):
        lo, hi = _unpack_rows(buf_ref[k])
        y_lo = y_lo + w[:, k:k + 1] * lo
        y_hi = y_hi + w[:, k:k + 1] * hi
    o_ref[:, :half] = y_lo
    o_ref[:, half:] = y_hi


def combine(dest, ys, wmat, shared):
    n, d = shared.shape
    tm = _pick(n, 128)
    return pl.pallas_call(
        _combine_kernel,
        grid=(n // tm,),
        in_specs=[pl.BlockSpec((TOP_K, tm), lambda i: (0, i), memory_space=pltpu.SMEM),
                  pl.BlockSpec(memory_space=pl.ANY),
                  pl.BlockSpec((tm, LANES), lambda i: (i, 0)),
                  pl.BlockSpec((tm, d), lambda i: (i, 0))],
        out_specs=pl.BlockSpec((tm, d), lambda i: (i, 0)),
        out_shape=jax.ShapeDtypeStruct((n, d), F32),
        scratch_shapes=[pltpu.VMEM((TOP_K, tm, d // 2), U32), pltpu.SemaphoreType.DMA(())],
        compiler_params=_cparams(1),
        name="combine",
    )(dest, ys, wmat, shared)


def moe_experts(packed, ids, ranks, wmat, counts, moe):
    _, _, exp_gu, exp_down, shared_gu, shared_down = moe
    n = packed.shape[0]
    cnt = counts[:, 0].astype(I32)
    padded = (cnt + MOE_TM - 1) // MOE_TM * MOE_TM
    ends = jnp.cumsum(padded)
    starts = ends - padded
    n_max = n * TOP_K // MOE_TM + N_EXPERTS
    n_tiles = (ends[-1] // MOE_TM).reshape(1)
    tile_expert = jnp.minimum(
        jnp.searchsorted(ends, jnp.arange(n_max, dtype=I32) * MOE_TM, side="right"), N_EXPERTS - 1).astype(I32)
    dest = dest_rows(starts, ids, ranks)
    xs = dispatch(starts, ends, dest, packed, n_max * MOE_TM)
    ys = grouped_experts(tile_expert, n_tiles, xs, exp_gu[:, :, :D_EXPERT].astype(BF16),
                         exp_gu[:, :, D_EXPERT:].astype(BF16), exp_down.astype(BF16))
    shared = shared_expert(packed, shared_gu[:, :D_EXPERT].astype(BF16), shared_gu[:, D_EXPERT:].astype(BF16),
                           shared_down.astype(BF16))
    return combine(dest, ys, wmat, shared)


def _rope_tables(seq, n_rows):
    pos = jnp.arange(seq)
    row = (pos // GRID_W).astype(F32)
    col = (pos % GRID_W).astype(F32)
    n_freq = HEAD_DIM // 4
    inv_freq = ROPE_THETA ** (-jnp.arange(n_freq, dtype=F32) / n_freq)
    ang = jnp.concatenate([row[:, None] * inv_freq, col[:, None] * inv_freq], axis=-1)
    cos, sin = jnp.cos(ang), jnp.sin(ang)
    cos_full = jnp.concatenate([cos, cos], axis=-1)
    sin_full = jnp.concatenate([-sin, sin], axis=-1)
    return cos_full, sin_full


def _trunk(x, c, ctx, c_ctx, layers, final_norm):
    bsz, seq, d = x.shape
    n_ctx = ctx.shape[1]
    dims = (bsz, seq, n_ctx)
    n_lat = bsz * seq
    n_all = n_lat + bsz * n_ctx
    depth = len(layers)

    seg = _Segments(seq, bsz, bsz * n_ctx)

    h = jnp.concatenate([x.reshape(n_lat, d), ctx.reshape(bsz * n_ctx, d)], axis=0)
    mod_rows = -(-(bsz + 1) // 8) * 8
    cond = jnp.zeros((mod_rows, d), F32).at[:bsz].set(c).at[bsz].set(c_ctx)

    cos1, sin1 = _rope_tables(seq, n_all)
    cos_t = jnp.concatenate([jnp.tile(cos1, (bsz, 1)), jnp.ones((bsz * n_ctx, HEAD_DIM), F32)], axis=0)
    sin_t = jnp.concatenate([jnp.tile(sin1, (bsz, 1)), jnp.zeros((bsz * n_ctx, HEAD_DIM), F32)], axis=0)
    ones_hd = jnp.ones((HEAD_DIM,), F32)

    tables = [ada_table(cond, *layer[1]) for layer in layers]
    a, = rowwise(h, layers[0][2], seg, shift=(tables[0], 0))
    for li, (kind, ada, norm1, mixer, norm2, moe) in enumerate(layers):
        need_ctx = li < depth - 1
        rows_out = n_all if need_ctx else n_lat
        mods = tables[li]
        resid = (h, mods, 2, seg)
        if kind == "mlstm":
            w_in, gate_b, head_norm, w_o = mixer
            n_main = 2 * M_HEADS * M_QK_DIM + M_HEADS * M_V_DIM + D_MODEL
            proj = matmul(a, w_in[:, :n_main].astype(BF16), scale_tiles=M_HEADS * M_QK_DIM // 512,
                          scale=M_QK_DIM ** -0.5)
            n_gate = 4 * M_HEADS
            w_gate = jnp.zeros((d, LANES), BF16).at[:, :n_gate].set(w_in[:, n_main:].astype(BF16))
            gate_b_pad = jnp.zeros((1, LANES), F32).at[0, :n_gate].set(gate_b.astype(F32))
            gates_t = mlstm_gates(a, w_gate, gate_b_pad)
            hdirs = mlstm_scan(proj, gates_t, dims)
            mixed = mlstm_finish(hdirs, proj, head_norm)
            h = matmul(mixed, w_o.astype(BF16), resid=resid, rows=rows_out)
        else:
            if kind == "global":
                w_qkv, q_norm, k_norm, w_o = mixer
                qkv = matmul(a, w_qkv.astype(BF16), qkv=(cos_t, sin_t, q_norm, k_norm, True))
                sink = None
            else:
                w_qkv, sink, w_o = mixer
                qkv = matmul(a, w_qkv.astype(BF16), qkv=(cos_t, sin_t, ones_hd, ones_hd, False))
            mixed = attention(qkv, dims, lat_queries=True, window=kind == "swa", sink=sink)
            if need_ctx:
                mixed_ctx = attention(qkv, dims, lat_queries=False, window=False, sink=sink)
                mixed = jnp.concatenate([mixed, mixed_ctx], axis=0)
            h = matmul(mixed, w_o.astype(BF16), resid=resid, rows=rows_out)
        router_w, router_b, exp_gu, exp_down, shared_gu, shared_down = moe
        routed = rowwise(h, norm2, seg, shift=(mods, 3), rows=rows_out, route=(router_w.T, router_b))
        y = moe_experts(*routed, moe)
        if li + 1 < depth:
            h, a = rowwise(h, layers[li + 1][2], seg, resid=(y, mods, 5), shift=(tables[li + 1], 0), rows=rows_out)
        else:
            _h, out = rowwise(h, final_norm, seg, resid=(y, mods, 5), rows=rows_out, out_dtype=F32)
    return out.reshape(bsz, seq, d)


def kernel(x, c, ctx, c_ctx, l0_ada_down, l0_ada_up, l0_ada_b, l0_norm1, l0_attn_qkv, l0_q_norm, l0_k_norm, l0_attn_o, l0_norm2, l0_router_w, l0_router_b, l0_exp_gu, l0_exp_down, l0_shared_gu, l0_shared_down, l1_ada_down, l1_ada_up, l1_ada_b, l1_norm1, l1_mlstm_in, l1_mlstm_gate_b, l1_mlstm_head_norm, l1_mlstm_o, l1_norm2, l1_router_w, l1_router_b, l1_exp_gu, l1_exp_down, l1_shared_gu, l1_shared_down, l2_ada_down, l2_ada_up, l2_ada_b, l2_norm1, l2_swa_qkv, l2_swa_sink, l2_swa_o, l2_norm2, l2_router_w, l2_router_b, l2_exp_gu, l2_exp_down, l2_shared_gu, l2_shared_down, l3_ada_down, l3_ada_up, l3_ada_b, l3_norm1, l3_attn_qkv, l3_q_norm, l3_k_norm, l3_attn_o, l3_norm2, l3_router_w, l3_router_b, l3_exp_gu, l3_exp_down, l3_shared_gu, l3_shared_down, final_norm):
    layers = [
        ("global", (l0_ada_down, l0_ada_up, l0_ada_b), l0_norm1, (l0_attn_qkv, l0_q_norm, l0_k_norm, l0_attn_o), l0_norm2,
         (l0_router_w, l0_router_b, l0_exp_gu, l0_exp_down, l0_shared_gu, l0_shared_down)),
        ("mlstm", (l1_ada_down, l1_ada_up, l1_ada_b), l1_norm1, (l1_mlstm_in, l1_mlstm_gate_b, l1_mlstm_head_norm, l1_mlstm_o), l1_norm2,
         (l1_router_w, l1_router_b, l1_exp_gu, l1_exp_down, l1_shared_gu, l1_shared_down)),
        ("swa", (l2_ada_down, l2_ada_up, l2_ada_b), l2_norm1, (l2_swa_qkv, l2_swa_sink, l2_swa_o), l2_norm2,
         (l2_router_w, l2_router_b, l2_exp_gu, l2_exp_down, l2_shared_gu, l2_shared_down)),
        ("global", (l3_ada_down, l3_ada_up, l3_ada_b), l3_norm1, (l3_attn_qkv, l3_q_norm, l3_k_norm, l3_attn_o), l3_norm2,
         (l3_router_w, l3_router_b, l3_exp_gu, l3_exp_down, l3_shared_gu, l3_shared_down)),
    ]
    return _trunk(x, c, ctx, c_ctx, layers, final_norm)
```

```python
import functools
import math
from typing import NamedTuple

import jax
import jax.numpy as jnp
from jax import lax
from jax.experimental import pallas as pl
from jax.experimental.pallas import tpu as pltpu

F32 = jnp.float32
BF16 = jnp.bfloat16
U32 = jnp.uint32
I32 = jnp.int32

D_MODEL = 4096
GRID_W = 64
RMS_EPS = 1e-6
N_MOD = 6
N_HEADS = 32
N_KV_HEADS = 8
HEAD_DIM = D_MODEL // N_HEADS
KV_GROUP = N_HEADS // N_KV_HEADS
ROPE_THETA = 10000.0
WINDOW = 128
M_HEADS = 8
M_V_DIM = D_MODEL // M_HEADS
M_QK_DIM = M_V_DIM // 2
M_CHUNK = 128
GATE_CAP = 15.0
N_EXPERTS = 64
TOP_K = 8
N_GROUPS = 8
TOPK_GROUPS = 4
D_EXPERT = 192
ROUTED_SCALE = 2.5

MOE_TM = 256
LANES = 128
VMEM_LIMIT = 56 * 1024 * 1024
NEG_INF = float("-inf")


def _cparams(n_axes):
    return pltpu.CompilerParams(dimension_semantics=("arbitrary",) * n_axes,
                                vmem_limit_bytes=VMEM_LIMIT)


def _pick(n, pref):
    t = pref
    while n % t:
        t //= 2
    return t


class _Segments(NamedTuple):
    seq: int
    bsz: int
    n_ctx_rows: int

    @property
    def tile_unit(self):
        return math.gcd(self.seq, self.n_ctx_rows)

    def of_row(self, row):
        return jnp.minimum(row // self.seq, self.bsz)


def _ada_kernel(cond_ref, down_ref, up_ref, b_ref, out_ref):
    c = cond_ref[...]
    t = jnp.dot(c * jax.nn.sigmoid(c), down_ref[...], precision=lax.Precision.HIGHEST,
                preferred_element_type=F32)
    out_ref[...] = jnp.dot(t, up_ref[...], precision=lax.Precision.HIGHEST,
                           preferred_element_type=F32) + b_ref[...]


def ada_table(cond_pad, down, up, bias):
    r, d = cond_pad.shape
    rank = down.shape[1]
    n = up.shape[1]
    tn = 2048
    out = pl.pallas_call(
        _ada_kernel,
        grid=(n // tn,),
        in_specs=[pl.BlockSpec((r, d), lambda j: (0, 0)),
                  pl.BlockSpec((d, rank), lambda j: (0, 0)),
                  pl.BlockSpec((rank, tn), lambda j: (0, j)),
                  pl.BlockSpec((1, tn), lambda j: (0, j))],
        out_specs=pl.BlockSpec((r, tn), lambda j: (0, j)),
        out_shape=jax.ShapeDtypeStruct((r, n), F32),
        compiler_params=_cparams(1),
        name="ada_table",
    )(cond_pad, down, up, bias.reshape(1, n))
    return out.reshape(r, N_MOD, d)


def _rowwise_kernel(*refs, gate_idx, shift_idx, route, moe_resid):
    it = iter(refs)
    h_ref = next(it)
    if moe_resid:
        y = _gather_combine(next(it), next(it), next(it), next(it), refs[-2], refs[-1])
    elif gate_idx is not None:
        y = next(it)[...].astype(F32)
    gmod_ref = next(it) if gate_idx is not None else None
    g_ref = next(it)
    smod_ref = next(it) if shift_idx is not None else None
    wt_ref, rb_ref = (next(it), next(it)) if route else (None, None)
    h = h_ref[...]
    if gate_idx is not None:
        h = h + gmod_ref[0, gate_idx:gate_idx + 1, :] * y
        next(it)[...] = h
    a = h * lax.rsqrt(jnp.mean(h * h, axis=-1, keepdims=True) + RMS_EPS) * g_ref[...]
    if shift_idx is not None:
        a = a * (1.0 + smod_ref[0, shift_idx + 1:shift_idx + 2, :]) + smod_ref[0, shift_idx:shift_idx + 1, :]
    if not route:
        a_ref = next(it)
        a_ref[...] = a.astype(a_ref.dtype)
        return
    packed_ref, ids_ref, rank_ref, w_ref, cnt_ref = (next(it) for _ in range(5))

    @pl.when(pl.program_id(0) == 0)
    def _():
        cnt_ref[...] = jnp.zeros_like(cnt_ref)

    packed_ref[...] = _pack_rows(a)
    ids, ranks, wmat, counts = _route(a, wt_ref[...], rb_ref[...], cnt_ref[:, 0:1])
    ids_ref[...] = ids
    rank_ref[...] = ranks
    w_ref[...] = wmat
    cnt_ref[...] = jnp.broadcast_to(counts, cnt_ref.shape)


def rowwise(h, gain, seg, *, resid=None, shift=None, out_dtype=BF16, rows=None, route=None):
    n, d = h.shape
    rows = n if rows is None else rows
    moe_resid = resid is not None and isinstance(resid[0], MoeOut)
    tm = _pick(seg.tile_unit, 128 if moe_resid else 256)
    row_spec = pl.BlockSpec((tm, d), lambda i: (i, 0))
    mod_spec = pl.BlockSpec((1, N_MOD, d), lambda i: (seg.of_row(i * tm), 0, 0))
    in_specs, args, scratch = [row_spec], [h], []
    if moe_resid:
        in_specs += [pl.BlockSpec((TOP_K, tm), lambda i: (0, i), memory_space=pltpu.SMEM),
                     pl.BlockSpec(memory_space=pl.ANY),
                     pl.BlockSpec((tm, LANES), lambda i: (i, 0)),
                     row_spec, mod_spec]
        args += [*resid[0], resid[1]]
        scratch = [pltpu.VMEM((TOP_K, tm, d // 2), U32), pltpu.SemaphoreType.DMA(())]
    elif resid is not None:
        in_specs += [row_spec, mod_spec]
        args += [resid[0], resid[1]]
    in_specs.append(pl.BlockSpec((1, d), lambda i: (0, 0)))
    args.append(gain.reshape(1, d))
    if shift is not None:
        in_specs.append(mod_spec)
        args.append(shift[0])
    if route is not None:
        in_specs += [pl.BlockSpec((N_EXPERTS, d), lambda i: (0, 0)),
                     pl.BlockSpec((N_EXPERTS, 1), lambda i: (0, 0))]
        args += [route[0], route[1].reshape(N_EXPERTS, 1)]
    out_specs, out_shape = [], []
    if resid is not None:
        out_specs.append(row_spec)
        out_shape.append(jax.ShapeDtypeStruct((rows, d), F32))
    if route is None:
        out_specs.append(row_spec)
        out_shape.append(jax.ShapeDtypeStruct((rows, d), out_dtype))
    else:
        out_specs += [pl.BlockSpec((tm, d // 2), lambda i: (i, 0)),
                      pl.BlockSpec((TOP_K, tm), lambda i: (0, i)),
                      pl.BlockSpec((TOP_K, tm), lambda i: (0, i)),
                      pl.BlockSpec((tm, LANES), lambda i: (i, 0)),
                      pl.BlockSpec((N_EXPERTS, LANES), lambda i: (0, 0))]
        out_shape += [jax.ShapeDtypeStruct((rows, d // 2), U32),
                      jax.ShapeDtypeStruct((TOP_K, rows), I32),
                      jax.ShapeDtypeStruct((TOP_K, rows), I32),
                      jax.ShapeDtypeStruct((rows, LANES), F32),
                      jax.ShapeDtypeStruct((N_EXPERTS, LANES), F32)]
    return pl.pallas_call(
        functools.partial(_rowwise_kernel, gate_idx=None if resid is None else resid[2],
                          shift_idx=None if shift is None else shift[1], route=route is not None,
                          moe_resid=moe_resid),
        grid=(rows // tm,),
        in_specs=in_specs, out_specs=out_specs, out_shape=out_shape, scratch_shapes=scratch,
        compiler_params=_cparams(1),
        name="rowwise",
    )(*args)


def _mm_plain_kernel(x_ref, w_ref, o_ref, *, scale_tiles, scale):
    acc = jnp.dot(x_ref[...], w_ref[...], preferred_element_type=F32)
    if scale_tiles:
        acc = acc * jnp.where(pl.program_id(1) < scale_tiles, scale, 1.0)
    o_ref[...] = acc.astype(o_ref.dtype)


def _mm_resid_kernel(x_ref, w_ref, h_ref, mod_ref, o_ref, *, gate_idx):
    acc = jnp.dot(x_ref[...], w_ref[...], preferred_element_type=F32)
    o_ref[...] = h_ref[...] + mod_ref[0, gate_idx:gate_idx + 1, :] * acc


def _mm_qkv_kernel(x_ref, w_ref, cos_ref, sin_ref, qn_ref, kn_ref, o_ref, *, nq_tiles, nk_tiles, qk_norm):
    acc = jnp.dot(x_ref[...], w_ref[...], preferred_element_type=F32)
    j = pl.program_id(1)

    @pl.when(j >= nq_tiles + nk_tiles)
    def _():
        o_ref[...] = acc.astype(o_ref.dtype)

    @pl.when(j < nq_tiles + nk_tiles)
    def _():
        is_q = j < nq_tiles
        post = jnp.where(is_q, HEAD_DIM ** -0.5, 1.0)
        gain = jnp.where(is_q, qn_ref[...], kn_ref[...])
        cos = cos_ref[...]
        sin = sin_ref[...]
        for s in range(0, acc.shape[1], HEAD_DIM):
            xh = acc[:, s:s + HEAD_DIM]
            if qk_norm:
                xh = xh * lax.rsqrt(jnp.mean(xh * xh, axis=-1, keepdims=True) + RMS_EPS) * gain
            xh = xh * cos + pltpu.roll(xh, HEAD_DIM // 2, axis=1) * sin
            o_ref[:, s:s + HEAD_DIM] = (xh * post).astype(o_ref.dtype)


def matmul(x, w, *, tm=1024, tn=512, out_dtype=BF16, rows=None, scale_tiles=0, scale=1.0,
           resid=None, qkv=None):
    m, k = x.shape
    m = m if rows is None else rows
    n = w.shape[1]
    tm = _pick(m if resid is None else math.gcd(m, resid[3].tile_unit), tm)
    tn = _pick(n, tn)
    grid = (m // tm, n // tn)
    x_spec = pl.BlockSpec((tm, k), lambda i, j: (i, 0))
    w_spec = pl.BlockSpec((k, tn), lambda i, j: (0, j))
    o_spec = pl.BlockSpec((tm, tn), lambda i, j: (i, j))
    if resid is not None:
        h, mods, gate_idx, seg = resid
        kern = functools.partial(_mm_resid_kernel, gate_idx=gate_idx)
        in_specs = [x_spec, w_spec, o_spec,
                    pl.BlockSpec((1, N_MOD, tn), lambda i, j: (seg.of_row(i * tm), 0, j))]
        args = (x, w, h, mods)
        out_dtype = F32
    elif qkv is not None:
        cos, sin, qn, kn, qk_norm = qkv
        kern = functools.partial(_mm_qkv_kernel, nq_tiles=N_HEADS * HEAD_DIM // tn,
                                 nk_tiles=N_KV_HEADS * HEAD_DIM // tn, qk_norm=qk_norm)
        tab_spec = pl.BlockSpec((tm, HEAD_DIM), lambda i, j: (i, 0))
        vec_spec = pl.BlockSpec((1, HEAD_DIM), lambda i, j: (0, 0))
        in_specs = [x_spec, w_spec, tab_spec, tab_spec, vec_spec, vec_spec]
        args = (x, w, cos, sin, qn.reshape(1, HEAD_DIM), kn.reshape(1, HEAD_DIM))
    else:
        kern = functools.partial(_mm_plain_kernel, scale_tiles=scale_tiles, scale=scale)
        in_specs = [x_spec, w_spec]
        args = (x, w)
    return pl.pallas_call(
        kern, grid=grid, in_specs=in_specs, out_specs=o_spec,
        out_shape=jax.ShapeDtypeStruct((m, n), out_dtype),
        compiler_params=_cparams(2),
        name="matmul",
    )(*args)


def _attn_kernel(*refs, tq, lat_keys, window, has_sink):
    it = iter(refs)
    sink_ref = next(it) if has_sink else None
    q_ref = next(it)
    kc_ref, vc_ref = next(it), next(it)
    kl_ref, vl_ref = (next(it), next(it)) if lat_keys else (None, None)
    o_ref = refs[-1]
    kvh = pl.program_id(1)
    qi = pl.program_id(2)
    rows = KV_GROUP * tq
    q = q_ref[...]
    qs = jnp.concatenate([q[:, g * HEAD_DIM:(g + 1) * HEAD_DIM] for g in range(KV_GROUP)], axis=0)
    nt = (((1,), (1,)), ((), ()))
    s_c = lax.dot_general(qs, kc_ref[...], nt, preferred_element_type=F32)
    m = jnp.max(s_c, axis=-1, keepdims=True)
    if lat_keys:
        if window:
            seq = kl_ref.shape[0]
            band = min(tq + 2 * WINDOW, seq)
            start = pl.multiple_of(jnp.clip(qi * tq - WINDOW, 0, seq - band), LANES)
            kl = kl_ref[pl.ds(start, band), :]
            vl = vl_ref[pl.ds(start, band), :]
            s_l = lax.dot_general(qs, kl, nt, preferred_element_type=F32)
            q_pos = qi * tq + lax.broadcasted_iota(jnp.int32, (rows, band), 0) % tq
            k_pos = start + lax.broadcasted_iota(jnp.int32, (rows, band), 1)
            s_l = jnp.where(jnp.abs(q_pos - k_pos) <= WINDOW, s_l, NEG_INF)
        else:
            kl = kl_ref[...]
            vl = vl_ref[...]
            s_l = lax.dot_general(qs, kl, nt, preferred_element_type=F32)
        m = jnp.maximum(m, jnp.max(s_l, axis=-1, keepdims=True))
    if has_sink:
        sink = jnp.concatenate(
            [jnp.full((tq, 1), sink_ref[kvh * KV_GROUP + g], F32) for g in range(KV_GROUP)], axis=0)
        m = jnp.maximum(m, sink)
    p_c = jnp.exp(s_c - m)
    den = jnp.sum(p_c, axis=-1, keepdims=True)
    acc = jnp.dot(p_c.astype(BF16), vc_ref[...], preferred_element_type=F32)
    if lat_keys:
        p_l = jnp.exp(s_l - m)
        den = den + jnp.sum(p_l, axis=-1, keepdims=True)
        acc = acc + jnp.dot(p_l.astype(BF16), vl, preferred_element_type=F32)
    if has_sink:
        den = den + jnp.exp(sink - m)
    out = acc / den
    for g in range(KV_GROUP):
        o_ref[:, g * HEAD_DIM:(g + 1) * HEAD_DIM] = out[g * tq:(g + 1) * tq].astype(o_ref.dtype)


def attention(qkv, dims, *, lat_queries, window, sink, out=None, out_rows=None):
    bsz, seq, n_ctx = dims
    n_lat = bsz * seq
    g_cols = KV_GROUP * HEAD_DIM
    k_col0 = N_HEADS * HEAD_DIM // HEAD_DIM
    v_col0 = k_col0 + N_KV_HEADS
    has_sink = sink is not None
    if lat_queries:
        tq = _pick(seq, 256)
        q_tiles = seq // tq
        q_row0 = 0
    else:
        tq = n_ctx
        q_tiles = 1
        q_row0 = n_lat // tq
    aliases = {}
    if out is not None:
        out_rows = out.shape[0]
    ctx_blk0 = n_lat // n_ctx
    in_specs = []
    args = []
    if has_sink:
        in_specs.append(pl.BlockSpec(memory_space=pltpu.SMEM))
        args.append(sink.astype(F32))
    in_specs += [
        pl.BlockSpec((tq, g_cols), lambda b, h, i: (q_row0 + b * q_tiles + i, h)),
        pl.BlockSpec((n_ctx, HEAD_DIM), lambda b, h, i: (ctx_blk0 + b, k_col0 + h)),
        pl.BlockSpec((n_ctx, HEAD_DIM), lambda b, h, i: (ctx_blk0 + b, v_col0 + h)),
    ]
    args += [qkv, qkv, qkv]
    if lat_queries:
        in_specs += [
            pl.BlockSpec((seq, HEAD_DIM), lambda b, h, i: (b, k_col0 + h)),
            pl.BlockSpec((seq, HEAD_DIM), lambda b, h, i: (b, v_col0 + h)),
        ]
        args += [qkv, qkv]
    if out is not None:
        aliases = {len(args): 0}
        in_specs.append(pl.BlockSpec(memory_space=pl.ANY))
        args.append(out)
    return pl.pallas_call(
        functools.partial(_attn_kernel, tq=tq, lat_keys=lat_queries, window=window, has_sink=has_sink),
        grid=(bsz, N_KV_HEADS, q_tiles),
        in_specs=in_specs,
        out_specs=pl.BlockSpec((tq, g_cols), lambda b, h, i: (q_row0 + b * q_tiles + i, h)),
        out_shape=jax.ShapeDtypeStruct((out_rows, N_HEADS * HEAD_DIM), BF16),
        input_output_aliases=aliases,
        compiler_params=_cparams(3),
        name="attention",
    )(*args)


def _mlstm_gate_kernel(x_ref, w_ref, b_ref, o_ref):
    g = jnp.dot(x_ref[...], w_ref[...], preferred_element_type=F32) + b_ref[...]
    g = GATE_CAP * jnp.tanh(g / GATE_CAP)
    log_f = jnp.minimum(g, 0.0) - jnp.log(1.0 + jnp.exp(-jnp.abs(g)))
    col = lax.broadcasted_iota(jnp.int32, g.shape, 1)
    is_f = (col // M_HEADS) % 2 == 1
    o_ref[...] = jnp.transpose(jnp.where(is_f, log_f, g))


def mlstm_gates(a, w_gate_pad, gate_b_pad):
    n, k = a.shape
    tm = _pick(n, 1024)
    return pl.pallas_call(
        _mlstm_gate_kernel,
        grid=(n // tm,),
        in_specs=[pl.BlockSpec((tm, k), lambda i: (i, 0)),
                  pl.BlockSpec((k, LANES), lambda i: (0, 0)),
                  pl.BlockSpec((1, LANES), lambda i: (0, 0))],
        out_specs=pl.BlockSpec((LANES, tm), lambda i: (0, i)),
        out_shape=jax.ShapeDtypeStruct((LANES, n), F32),
        compiler_params=_cparams(1),
        name="mlstm_gates",
    )(a, w_gate_pad, gate_b_pad)


def _mlstm_kernel(q_ref, k_ref, v_ref, gi_ref, gf_ref, o_ref, ct_ref, n_ref, m_ref):
    d = pl.program_id(1)
    c = pl.program_id(2)
    L = M_CHUNK

    @pl.when(c == 0)
    def _():
        ct_ref[...] = jnp.zeros_like(ct_ref)
        n_ref[...] = jnp.zeros_like(n_ref)
        m_ref[...] = jnp.zeros_like(m_ref)

    sgn = 1 - 2 * d
    diff = (lax.broadcasted_iota(jnp.int32, (L, L), 0) - lax.broadcasted_iota(jnp.int32, (L, L), 1)) * sgn
    mask = diff >= 0
    mask_t = diff <= 0
    eye = diff == 0
    nt = (((1,), (1,)), ((), ()))
    tn = (((0,), (0,)), ((), ()))
    for h in range(M_HEADS):
        qc = q_ref[:, h * M_QK_DIM:(h + 1) * M_QK_DIM]
        kc = k_ref[:, h * M_QK_DIM:(h + 1) * M_QK_DIM]
        vc = v_ref[:, h * M_V_DIM:(h + 1) * M_V_DIM]
        i_row = gi_ref[h:h + 1, :]
        f_row = gf_ref[h:h + 1, :]
        m_prev = m_ref[h, :, 0:1]
        n_prev = n_ref[h]
        ct_prev = ct_ref[h]
        f_col = jnp.sum(jnp.where(eye, f_row, 0.0), axis=1, keepdims=True)
        i_col = jnp.sum(jnp.where(eye, i_row, 0.0), axis=1, keepdims=True)
        b_col = jnp.sum(jnp.where(mask, f_row, 0.0), axis=1, keepdims=True)
        b_row = jnp.sum(jnp.where(mask_t, f_col, 0.0), axis=0, keepdims=True)
        log_inter = b_col + m_prev
        log_intra = jnp.where(mask, b_col - b_row + i_row, NEG_INF)
        m_t = jnp.maximum(log_inter, jnp.max(log_intra, axis=1, keepdims=True))
        w_inter = jnp.exp(log_inter - m_t)
        s_qk = lax.dot_general(qc, kc, nt, preferred_element_type=F32) * jnp.exp(log_intra - m_t)
        num = (jnp.dot(s_qk.astype(BF16), vc, preferred_element_type=F32)
               + w_inter * jnp.dot(qc, ct_prev.astype(BF16), preferred_element_type=F32))
        den = (jnp.sum(s_qk, axis=1, keepdims=True)
               + w_inter * jnp.sum(qc.astype(F32) * n_prev, axis=1, keepdims=True))
        h_out = num / jnp.maximum(jnp.abs(den), jnp.exp(-m_t))
        o_ref[0, :, h * M_V_DIM:(h + 1) * M_V_DIM] = h_out.astype(o_ref.dtype)
        b_tot = jnp.sum(f_row, axis=1, keepdims=True)
        log_w = b_tot - b_col + i_col
        m_new = jnp.maximum(b_tot + m_prev, jnp.max(log_w, axis=0, keepdims=True))
        decay = jnp.exp(b_tot + m_prev - m_new)
        w_k = jnp.exp(log_w - m_new)
        vw = (vc.astype(F32) * w_k).astype(BF16)
        ct_ref[h] = decay * ct_prev + lax.dot_general(kc, vw, tn, preferred_element_type=F32)
        n_ref[h] = decay * n_prev + jnp.sum(kc.astype(F32) * w_k, axis=0, keepdims=True)
        m_ref[h] = jnp.broadcast_to(m_new, (1, LANES))


def mlstm_scan(proj, gates_t, dims):
    bsz, seq, n_ctx = dims
    nt_rows = proj.shape[0]
    L = M_CHUNK
    ncc, nlc = n_ctx // L, seq // L
    lat_blocks = bsz * nlc
    qk_w = M_HEADS * M_QK_DIM

    def row_blk(b, d, c):
        cc = jnp.where(d == 0, c, ncc - 1 - c)
        lc = jnp.where(d == 0, c - ncc, nlc - 1 - (c - ncc))
        return jnp.where(c < ncc, lat_blocks + b * ncc + cc, b * nlc + lc)

    return pl.pallas_call(
        _mlstm_kernel,
        grid=(bsz, 2, ncc + nlc),
        in_specs=[
            pl.BlockSpec((L, qk_w), lambda b, d, c: (row_blk(b, d, c), 0)),
            pl.BlockSpec((L, qk_w), lambda b, d, c: (row_blk(b, d, c), 1)),
            pl.BlockSpec((L, D_MODEL), lambda b, d, c: (row_blk(b, d, c), 1)),
            pl.BlockSpec((M_HEADS, L), lambda b, d, c: (2 * d, row_blk(b, d, c))),
            pl.BlockSpec((M_HEADS, L), lambda b, d, c: (2 * d + 1, row_blk(b, d, c))),
        ],
        out_specs=pl.BlockSpec((1, L, D_MODEL), lambda b, d, c: (d, row_blk(b, d, c), 0)),
        out_shape=jax.ShapeDtypeStruct((2, nt_rows, D_MODEL), BF16),
        scratch_shapes=[pltpu.VMEM((M_HEADS, M_QK_DIM, M_V_DIM), F32),
                        pltpu.VMEM((M_HEADS, 1, M_QK_DIM), F32),
                        pltpu.VMEM((M_HEADS, 1, LANES), F32)],
        compiler_params=_cparams(3),
        name="mlstm_scan",
    )(proj, proj, proj, gates_t, gates_t)


def _mlstm_finish_kernel(hf_ref, hb_ref, og_ref, g_ref, o_ref):
    for h in range(M_HEADS):
        sl = slice(h * M_V_DIM, (h + 1) * M_V_DIM)
        x = hf_ref[0, :, sl].astype(F32) + hb_ref[0, :, sl].astype(F32)
        y = x * lax.rsqrt(jnp.mean(x * x, axis=-1, keepdims=True) + RMS_EPS) * g_ref[:, sl]
        o_ref[:, sl] = (y * jax.nn.sigmoid(og_ref[:, sl].astype(F32))).astype(o_ref.dtype)


def mlstm_finish(hdirs, proj, head_norm):
    _, n, d = hdirs.shape
    tm = _pick(n, 256)
    return pl.pallas_call(
        _mlstm_finish_kernel,
        grid=(n // tm,),
        in_specs=[pl.BlockSpec((1, tm, d), lambda i: (0, i, 0)),
                  pl.BlockSpec((1, tm, d), lambda i: (1, i, 0)),
                  pl.BlockSpec((tm, d), lambda i: (i, 2)),
                  pl.BlockSpec((1, d), lambda i: (0, 0))],
        out_specs=pl.BlockSpec((tm, d), lambda i: (i, 0)),
        out_shape=jax.ShapeDtypeStruct((n, d), BF16),
        compiler_params=_cparams(1),
        name="mlstm_finish",
    )(hdirs, hdirs, proj, head_norm.reshape(1, d))


def _pack_rows(x):
    half = x.shape[1] // 2
    lo = lax.bitcast_convert_type(x[:, :half].astype(BF16).astype(F32), U32)
    hi = lax.bitcast_convert_type(x[:, half:].astype(BF16).astype(F32), U32)
    return (hi & jnp.uint32(0xFFFF0000)) | (lo >> 16)


def _unpack_rows(p):
    lo = lax.bitcast_convert_type(p << 16, F32)
    hi = lax.bitcast_convert_type(p & jnp.uint32(0xFFFF0000), F32)
    return lo, hi


def _route(a, w_t, bias, counts):
    tm = a.shape[0]
    per = N_EXPERTS // N_GROUPS
    nt = (((1,), (1,)), ((), ()))
    w_hi = w_t.astype(BF16)
    w_lo = (w_t - w_hi.astype(F32)).astype(BF16)
    a_hi = a.astype(BF16)
    a_lo = (a - a_hi.astype(F32)).astype(BF16)
    logits = (lax.dot_general(w_hi, a_hi, nt, preferred_element_type=F32)
              + lax.dot_general(w_lo, a_hi, nt, preferred_element_type=F32)
              + lax.dot_general(w_hi, a_lo, nt, preferred_element_type=F32))
    scores = jax.nn.sigmoid(logits).reshape(N_GROUPS, per, tm)
    biased = scores + bias.reshape(N_GROUPS, per, 1)
    e_iota = lax.broadcasted_iota(jnp.int32, (N_GROUPS, per, tm), 1).astype(F32)
    g_iota = lax.broadcasted_iota(jnp.int32, (N_GROUPS, 1, tm), 0).astype(F32)
    lin_iota = lax.broadcasted_iota(jnp.int32, (N_GROUPS, per, tm), 0).astype(F32) * per + e_iota
    m1 = jnp.max(biased, axis=1, keepdims=True)
    i1 = jnp.min(jnp.where(biased == m1, e_iota, float(per)), axis=1, keepdims=True)
    m2 = jnp.max(jnp.where(e_iota == i1, NEG_INF, biased), axis=1, keepdims=True)
    gscore = m1 + m2
    gsel = jnp.zeros(gscore.shape, F32)
    for _ in range(TOPK_GROUPS):
        cur = jnp.where(gsel > 0.0, NEG_INF, gscore)
        gm = jnp.max(cur, axis=0, keepdims=True)
        gi = jnp.min(jnp.where(cur == gm, g_iota, float(N_GROUPS)), axis=0, keepdims=True)
        gsel = jnp.where(g_iota == gi, 1.0, gsel)
    cand = jnp.where(gsel > 0.0, biased, NEG_INF)
    sel = jnp.zeros(cand.shape, F32)
    picks = []
    for _ in range(TOP_K):
        cur = jnp.where(sel > 0.0, NEG_INF, cand)
        em = jnp.max(jnp.max(cur, axis=1, keepdims=True), axis=0, keepdims=True)
        hit = jnp.where(cur == em, lin_iota, float(N_EXPERTS))
        ei = jnp.min(jnp.min(hit, axis=1, keepdims=True), axis=0, keepdims=True)
        sel = jnp.where(lin_iota == ei, 1.0, sel)
        picks.append(ei)
    sel2 = sel.reshape(N_EXPERTS, tm)
    before = lax.broadcasted_iota(jnp.int32, (tm, tm), 0) < lax.broadcasted_iota(jnp.int32, (tm, tm), 1)
    rank = jnp.dot(sel2.astype(BF16), jnp.where(before, 1.0, 0.0).astype(BF16),
                   preferred_element_type=F32) + counts
    new_counts = counts + jnp.sum(sel2, axis=1, keepdims=True)
    rank3 = rank.reshape(N_GROUPS, per, tm)
    wsum = jnp.sum(jnp.sum(jnp.where(sel > 0.0, scores, 0.0), axis=1, keepdims=True), axis=0, keepdims=True)

    def pick(ei, table):
        v = jnp.where(lin_iota == ei, table, 0.0)
        return jnp.sum(jnp.sum(v, axis=1, keepdims=True), axis=0, keepdims=True).reshape(1, tm)

    ids = jnp.concatenate([ei.reshape(1, tm) for ei in picks], axis=0).astype(I32)
    ranks = jnp.concatenate([pick(ei, rank3) for ei in picks], axis=0).astype(I32)
    w = jnp.concatenate([pick(ei, scores) for ei in picks], axis=0) / wsum.reshape(1, tm) * ROUTED_SCALE
    wmat = jnp.transpose(jnp.concatenate([w, jnp.zeros((LANES - TOP_K, tm), F32)], axis=0))
    return ids, ranks, wmat, new_counts


def _dest_kernel(starts_ref, ids_ref, rank_ref, dest_ref):
    ids = ids_ref[...]
    acc = rank_ref[...]
    for e in range(N_EXPERTS):
        acc = acc + jnp.where(ids == e, starts_ref[e], 0)
    dest_ref[...] = acc


def dest_rows(starts, ids, ranks):
    return pl.pallas_call(
        _dest_kernel,
        in_specs=[pl.BlockSpec(memory_space=pltpu.SMEM), pl.BlockSpec(memory_space=pltpu.VMEM),
                  pl.BlockSpec(memory_space=pltpu.VMEM)],
        out_specs=pl.BlockSpec(memory_space=pltpu.VMEM),
        out_shape=jax.ShapeDtypeStruct(ids.shape, I32),
        compiler_params=pltpu.CompilerParams(vmem_limit_bytes=VMEM_LIMIT),
        name="dest_rows",
    )(starts, ids, ranks)


def _dispatch_kernel(starts_ref, ends_ref, dest_ref, f_ref, xs_ref, zero_ref, sem):
    tm = f_ref.shape[0]

    def zero_copy(e):
        row0 = pl.multiple_of(ends_ref[e] - MOE_TM, MOE_TM)
        return pltpu.make_async_copy(zero_ref, xs_ref.at[pl.ds(row0, MOE_TM)], sem)

    @pl.when(pl.program_id(0) == 0)
    def _():
        zero_ref[...] = jnp.zeros_like(zero_ref)
        for e in range(N_EXPERTS):
            @pl.when(ends_ref[e] > starts_ref[e])
            def _():
                zero_copy(e).start()
        for e in range(N_EXPERTS):
            @pl.when(ends_ref[e] > starts_ref[e])
            def _():
                zero_copy(e).wait()

    def row_copy(t, d):
        return pltpu.make_async_copy(f_ref.at[pl.ds(t, 1)], xs_ref.at[pl.ds(d, 1)], sem)

    def issue(t, c):
        for k in range(TOP_K):
            row_copy(t, dest_ref[k, t]).start()
        return c

    def drain(t, c):
        for k in range(TOP_K):
            row_copy(0, 0).wait()
        return c

    lax.fori_loop(0, tm, issue, 0)
    lax.fori_loop(0, tm, drain, 0)


def dispatch(starts, ends, dest, packed, n_rows):
    n, half = packed.shape
    tm = _pick(n, 256)
    return pl.pallas_call(
        _dispatch_kernel,
        grid_spec=pltpu.PrefetchScalarGridSpec(
            num_scalar_prefetch=2, grid=(n // tm,),
            in_specs=[pl.BlockSpec((TOP_K, tm), lambda i, s, e: (0, i), memory_space=pltpu.SMEM),
                      pl.BlockSpec((tm, half), lambda i, s, e: (i, 0))],
            out_specs=pl.BlockSpec(memory_space=pl.ANY),
            scratch_shapes=[pltpu.VMEM((MOE_TM, half), U32), pltpu.SemaphoreType.DMA(())]),
        out_shape=jax.ShapeDtypeStruct((n_rows, half), U32),
        compiler_params=_cparams(1),
        name="dispatch",
    )(starts, ends, dest, packed)


def _swiglu_expert(x_packed, w_gu, w_dn):
    half = x_packed.shape[1]
    lo, hi = _unpack_rows(x_packed)
    gu = (jnp.dot(lo.astype(BF16), w_gu[:half, :], preferred_element_type=F32)
          + jnp.dot(hi.astype(BF16), w_gu[half:, :], preferred_element_type=F32))
    g = gu[:, :D_EXPERT]
    act = (g * jax.nn.sigmoid(g) * gu[:, D_EXPERT:]).astype(BF16)
    return jnp.dot(act, w_dn, preferred_element_type=F32)


def _grouped_kernel(te_ref, nt_ref, x_ref, gu_ref, dn_ref, o_ref, gu_bf, dn_bf):
    j = pl.program_id(0)

    @pl.when(j < nt_ref[0])
    def _():
        @pl.when(jnp.logical_or(j == 0, te_ref[j] != te_ref[jnp.maximum(j - 1, 0)]))
        def _():
            gu_bf[...] = gu_ref[0].astype(BF16)
            dn_bf[...] = dn_ref[0].astype(BF16)

        o_ref[...] = _pack_rows(_swiglu_expert(x_ref[...], gu_bf[...], dn_bf[...]))


def grouped_experts(tile_expert, n_tiles, xs, exp_gu, exp_down):
    rows, half = xs.shape
    d = 2 * half

    def tile(j, nt):
        return jnp.minimum(j, nt[0] - 1)

    return pl.pallas_call(
        _grouped_kernel,
        grid_spec=pltpu.PrefetchScalarGridSpec(
            num_scalar_prefetch=2, grid=(rows // MOE_TM,),
            in_specs=[pl.BlockSpec((MOE_TM, half), lambda j, te, nt: (tile(j, nt), 0)),
                      pl.BlockSpec((1, d, 2 * D_EXPERT), lambda j, te, nt: (te[tile(j, nt)], 0, 0)),
                      pl.BlockSpec((1, D_EXPERT, d), lambda j, te, nt: (te[tile(j, nt)], 0, 0))],
            out_specs=pl.BlockSpec((MOE_TM, half), lambda j, te, nt: (tile(j, nt), 0)),
            scratch_shapes=[pltpu.VMEM((d, 2 * D_EXPERT), BF16), pltpu.VMEM((D_EXPERT, d), BF16)]),
        out_shape=jax.ShapeDtypeStruct((rows, half), U32),
        compiler_params=_cparams(1),
        name="grouped_experts",
    )(tile_expert, n_tiles, xs, exp_gu, exp_down)


def _shared_kernel(x_ref, gu_ref, dn_ref, o_ref, gu_bf, dn_bf):
    @pl.when(pl.program_id(0) == 0)
    def _():
        gu_bf[...] = gu_ref[...].astype(BF16)
        dn_bf[...] = dn_ref[...].astype(BF16)

    o_ref[...] = _swiglu_expert(x_ref[...], gu_bf[...], dn_bf[...]).astype(o_ref.dtype)


def shared_expert(packed, w_gu, w_dn):
    n, half = packed.shape
    tm = _pick(n, 512)

    def whole(shape):
        return pl.BlockSpec(shape, lambda i: (0,) * len(shape))

    return pl.pallas_call(
        _shared_kernel,
        grid=(n // tm,),
        in_specs=[pl.BlockSpec((tm, half), lambda i: (i, 0)), whole(w_gu.shape), whole(w_dn.shape)],
        out_specs=pl.BlockSpec((tm, 2 * half), lambda i: (i, 0)),
        out_shape=jax.ShapeDtypeStruct((n, 2 * half), BF16),
        scratch_shapes=[pltpu.VMEM(w_gu.shape, BF16), pltpu.VMEM(w_dn.shape, BF16)],
        compiler_params=_cparams(1),
        name="shared_expert",
    )(packed, w_gu, w_dn)


def _gather_combine(dest_ref, ys_ref, w_ref, sh_ref, buf_ref, sem):
    tm, d = sh_ref.shape
    half = d // 2

    def row_copy(k, t, src):
        return pltpu.make_async_copy(ys_ref.at[pl.ds(src, 1)], buf_ref.at[k, pl.ds(t, 1)], sem)

    def issue(t, c):
        for k in range(TOP_K):
            row_copy(k, t, dest_ref[k, t]).start()
        return c

    def drain(t, c):
        for k in range(TOP_K):
            row_copy(0, 0, 0).wait()
        return c

    lax.fori_loop(0, tm, issue, 0)
    lax.fori_loop(0, tm, drain, 0)
    w = w_ref[...]
    y_lo = sh_ref[:, :half].astype(F32)
    y_hi = sh_ref[:, half:].astype(F32)
    for k in range(TOP_K):
        lo, hi = _unpack_rows(buf_ref[k])
        y_lo = y_lo + w[:, k:k + 1] * lo
        y_hi = y_hi + w[:, k:k + 1] * hi
    return jnp.concatenate([y_lo, y_hi], axis=1)


class MoeOut(NamedTuple):
    dest: jax.Array
    ys: jax.Array
    wmat: jax.Array
    shared: jax.Array


def moe_experts(packed, ids, ranks, wmat, counts, moe):
    _, _, exp_gu, exp_down, shared_gu, shared_down = moe
    n = packed.shape[0]
    cnt = counts[:, 0].astype(I32)
    padded = (cnt + MOE_TM - 1) // MOE_TM * MOE_TM
    ends = jnp.cumsum(padded)
    starts = ends - padded
    n_max = n * TOP_K // MOE_TM + N_EXPERTS
    n_tiles = (ends[-1] // MOE_TM).reshape(1)
    tile_row0 = jnp.arange(n_max, dtype=I32) * MOE_TM
    tile_expert = jnp.minimum(jnp.sum(ends[None, :] <= tile_row0[:, None], axis=1), N_EXPERTS - 1).astype(I32)
    dest = dest_rows(starts, ids, ranks)
    xs = dispatch(starts, ends, dest, packed, n_max * MOE_TM)
    ys = grouped_experts(tile_expert, n_tiles, xs, exp_gu, exp_down)
    shared = shared_expert(packed, shared_gu, shared_down)
    return MoeOut(dest, ys, wmat, shared)


def _rope_tables(seq, n_rows):
    pos = jnp.arange(seq)
    row = (pos // GRID_W).astype(F32)
    col = (pos % GRID_W).astype(F32)
    n_freq = HEAD_DIM // 4
    inv_freq = ROPE_THETA ** (-jnp.arange(n_freq, dtype=F32) / n_freq)
    ang = jnp.concatenate([row[:, None] * inv_freq, col[:, None] * inv_freq], axis=-1)
    cos, sin = jnp.cos(ang), jnp.sin(ang)
    cos_full = jnp.concatenate([cos, cos], axis=-1)
    sin_full = jnp.concatenate([-sin, sin], axis=-1)
    return cos_full, sin_full


def _trunk(x, c, ctx, c_ctx, layers, final_norm):
    bsz, seq, d = x.shape
    n_ctx = ctx.shape[1]
    dims = (bsz, seq, n_ctx)
    n_lat = bsz * seq
    n_all = n_lat + bsz * n_ctx
    depth = len(layers)

    seg = _Segments(seq, bsz, bsz * n_ctx)

    h = jnp.concatenate([x.reshape(n_lat, d), ctx.reshape(bsz * n_ctx, d)], axis=0)
    mod_rows = -(-(bsz + 1) // 8) * 8
    cond = jnp.zeros((mod_rows, d), F32).at[:bsz].set(c).at[bsz].set(c_ctx)

    cos1, sin1 = _rope_tables(seq, n_all)
    cos_t = jnp.concatenate([jnp.tile(cos1, (bsz, 1)), jnp.ones((bsz * n_ctx, HEAD_DIM), F32)], axis=0)
    sin_t = jnp.concatenate([jnp.tile(sin1, (bsz, 1)), jnp.zeros((bsz * n_ctx, HEAD_DIM), F32)], axis=0)
    ones_hd = jnp.ones((HEAD_DIM,), F32)

    tables = [ada_table(cond, *layer[1]) for layer in layers]
    a, = rowwise(h, layers[0][2], seg, shift=(tables[0], 0))
    for li, (kind, ada, norm1, mixer, norm2, moe) in enumerate(layers):
        need_ctx = li < depth - 1
        rows_out = n_all if need_ctx else n_lat
        mods = tables[li]
        resid = (h, mods, 2, seg)
        if kind == "mlstm":
            w_in, gate_b, head_norm, w_o = mixer
            n_main = 2 * M_HEADS * M_QK_DIM + M_HEADS * M_V_DIM + D_MODEL
            proj = matmul(a, w_in[:, :n_main].astype(BF16), scale_tiles=M_HEADS * M_QK_DIM // 512,
                          scale=M_QK_DIM ** -0.5)
            n_gate = 4 * M_HEADS
            w_gate = jnp.zeros((d, LANES), BF16).at[:, :n_gate].set(w_in[:, n_main:].astype(BF16))
            gate_b_pad = jnp.zeros((1, LANES), F32).at[0, :n_gate].set(gate_b.astype(F32))
            gates_t = mlstm_gates(a, w_gate, gate_b_pad)
            hdirs = mlstm_scan(proj, gates_t, dims)
            mixed = mlstm_finish(hdirs, proj, head_norm)
            h = matmul(mixed, w_o.astype(BF16), resid=resid, rows=rows_out)
        else:
            if kind == "global":
                w_qkv, q_norm, k_norm, w_o = mixer
                qkv = matmul(a, w_qkv.astype(BF16), qkv=(cos_t, sin_t, q_norm, k_norm, True))
                sink = None
            else:
                w_qkv, sink, w_o = mixer
                qkv = matmul(a, w_qkv.astype(BF16), qkv=(cos_t, sin_t, ones_hd, ones_hd, False))
            mixed = attention(qkv, dims, lat_queries=True, window=kind == "swa", sink=sink, out_rows=rows_out)
            if need_ctx:
                mixed = attention(qkv, dims, lat_queries=False, window=False, sink=sink, out=mixed)
            h = matmul(mixed, w_o.astype(BF16), resid=resid, rows=rows_out)
        router_w, router_b, exp_gu, exp_down, shared_gu, shared_down = moe
        routed = rowwise(h, norm2, seg, shift=(mods, 3), rows=rows_out, route=(router_w.T, router_b))
        y = moe_experts(*routed, moe)
        if li + 1 < depth:
            h, a = rowwise(h, layers[li + 1][2], seg, resid=(y, mods, 5), shift=(tables[li + 1], 0), rows=rows_out)
        else:
            _h, out = rowwise(h, final_norm, seg, resid=(y, mods, 5), rows=rows_out, out_dtype=F32)
    return out.reshape(bsz, seq, d)


def kernel(x, c, ctx, c_ctx, l0_ada_down, l0_ada_up, l0_ada_b, l0_norm1, l0_attn_qkv, l0_q_norm, l0_k_norm, l0_attn_o, l0_norm2, l0_router_w, l0_router_b, l0_exp_gu, l0_exp_down, l0_shared_gu, l0_shared_down, l1_ada_down, l1_ada_up, l1_ada_b, l1_norm1, l1_mlstm_in, l1_mlstm_gate_b, l1_mlstm_head_norm, l1_mlstm_o, l1_norm2, l1_router_w, l1_router_b, l1_exp_gu, l1_exp_down, l1_shared_gu, l1_shared_down, l2_ada_down, l2_ada_up, l2_ada_b, l2_norm1, l2_swa_qkv, l2_swa_sink, l2_swa_o, l2_norm2, l2_router_w, l2_router_b, l2_exp_gu, l2_exp_down, l2_shared_gu, l2_shared_down, l3_ada_down, l3_ada_up, l3_ada_b, l3_norm1, l3_attn_qkv, l3_q_norm, l3_k_norm, l3_attn_o, l3_norm2, l3_router_w, l3_router_b, l3_exp_gu, l3_exp_down, l3_shared_gu, l3_shared_down, final_norm):
    layers = [
        ("global", (l0_ada_down, l0_ada_up, l0_ada_b), l0_norm1, (l0_attn_qkv, l0_q_norm, l0_k_norm, l0_attn_o), l0_norm2,
         (l0_router_w, l0_router_b, l0_exp_gu, l0_exp_down, l0_shared_gu, l0_shared_down)),
        ("mlstm", (l1_ada_down, l1_ada_up, l1_ada_b), l1_norm1, (l1_mlstm_in, l1_mlstm_gate_b, l1_mlstm_head_norm, l1_mlstm_o), l1_norm2,
         (l1_router_w, l1_router_b, l1_exp_gu, l1_exp_down, l1_shared_gu, l1_shared_down)),
        ("swa", (l2_ada_down, l2_ada_up, l2_ada_b), l2_norm1, (l2_swa_qkv, l2_swa_sink, l2_swa_o), l2_norm2,
         (l2_router_w, l2_router_b, l2_exp_gu, l2_exp_down, l2_shared_gu, l2_shared_down)),
        ("global", (l3_ada_down, l3_ada_up, l3_ada_b), l3_norm1, (l3_attn_qkv, l3_q_norm, l3_k_norm, l3_attn_o), l3_norm2,
         (l3_router_w, l3_router_b, l3_exp_gu, l3_exp_down, l3_shared_gu, l3_shared_down)),
    ]
    return _trunk(x, c, ctx, c_ctx, layers, final_norm)
```

```python
import functools
import math
from typing import NamedTuple

import jax
import jax.numpy as jnp
from jax import lax
from jax.experimental import pallas as pl
from jax.experimental.pallas import tpu as pltpu

F32 = jnp.float32
BF16 = jnp.bfloat16
U32 = jnp.uint32
I32 = jnp.int32

D_MODEL = 4096
GRID_W = 64
RMS_EPS = 1e-6
N_MOD = 6
N_HEADS = 32
N_KV_HEADS = 8
HEAD_DIM = D_MODEL // N_HEADS
KV_GROUP = N_HEADS // N_KV_HEADS
ROPE_THETA = 10000.0
WINDOW = 128
M_HEADS = 8
M_V_DIM = D_MODEL // M_HEADS
M_QK_DIM = M_V_DIM // 2
M_CHUNK = 128
GATE_CAP = 15.0
N_EXPERTS = 64
TOP_K = 8
N_GROUPS = 8
TOPK_GROUPS = 4
D_EXPERT = 192
ROUTED_SCALE = 2.5

MOE_TM = 512
MOE_CHUNK = 512
LANES = 128
VMEM_LIMIT = 56 * 1024 * 1024
NEG_INF = float("-inf")


def _cparams(n_axes):
    return pltpu.CompilerParams(dimension_semantics=("arbitrary",) * n_axes,
                                vmem_limit_bytes=VMEM_LIMIT)


def _pick(n, pref):
    t = pref
    while n % t:
        t //= 2
    return t


class _Segments(NamedTuple):
    seq: int
    bsz: int
    n_ctx_rows: int

    @property
    def tile_unit(self):
        return math.gcd(self.seq, self.n_ctx_rows)

    def of_row(self, row):
        return jnp.minimum(row // self.seq, self.bsz)


def _ada_kernel(cond_ref, down_ref, up_ref, b_ref, out_ref):
    c = cond_ref[...]
    t = jnp.dot(c * jax.nn.sigmoid(c), down_ref[...], precision=lax.Precision.HIGHEST,
                preferred_element_type=F32)
    out_ref[...] = jnp.dot(t, up_ref[...], precision=lax.Precision.HIGHEST,
                           preferred_element_type=F32) + b_ref[...]


def ada_table(cond_pad, down, up, bias):
    r, d = cond_pad.shape
    rank = down.shape[1]
    n = up.shape[1]
    tn = 2048
    out = pl.pallas_call(
        _ada_kernel,
        grid=(n // tn,),
        in_specs=[pl.BlockSpec((r, d), lambda j: (0, 0)),
                  pl.BlockSpec((d, rank), lambda j: (0, 0)),
                  pl.BlockSpec((rank, tn), lambda j: (0, j)),
                  pl.BlockSpec((1, tn), lambda j: (0, j))],
        out_specs=pl.BlockSpec((r, tn), lambda j: (0, j)),
        out_shape=jax.ShapeDtypeStruct((r, n), F32),
        compiler_params=_cparams(1),
        name="ada_table",
    )(cond_pad, down, up, bias.reshape(1, n))
    return out.reshape(r, N_MOD, d)


def _rowwise_kernel(*refs, gate_idx, shift_idx, route, moe_resid):
    it = iter(refs)
    h_ref = next(it)
    if moe_resid:
        y = _gather_combine(next(it), next(it), next(it), next(it), refs[-2], refs[-1])
    elif gate_idx is not None:
        y = next(it)[...].astype(F32)
    gmod_ref = next(it) if gate_idx is not None else None
    g_ref = next(it)
    smod_ref = next(it) if shift_idx is not None else None
    wt_ref, rb_ref = (next(it), next(it)) if route else (None, None)
    h = h_ref[...]
    if gate_idx is not None:
        h = h + gmod_ref[0, gate_idx:gate_idx + 1, :] * y
        next(it)[...] = h
    a = h * lax.rsqrt(jnp.mean(h * h, axis=-1, keepdims=True) + RMS_EPS) * g_ref[...]
    if shift_idx is not None:
        a = a * (1.0 + smod_ref[0, shift_idx + 1:shift_idx + 2, :]) + smod_ref[0, shift_idx:shift_idx + 1, :]
    if not route:
        a_ref = next(it)
        a_ref[...] = a.astype(a_ref.dtype)
        return
    packed_ref, ids_ref, rank_ref, w_ref, cnt_ref = (next(it) for _ in range(5))

    @pl.when(pl.program_id(0) == 0)
    def _():
        cnt_ref[...] = jnp.zeros_like(cnt_ref)

    packed_ref[...] = _pack_rows(a)
    ids, ranks, wmat, counts = _route(a, wt_ref[...], rb_ref[...], cnt_ref[:, 0:1])
    ids_ref[...] = ids
    rank_ref[...] = ranks
    w_ref[...] = wmat
    cnt_ref[...] = jnp.broadcast_to(counts, cnt_ref.shape)


def rowwise(h, gain, seg, *, resid=None, shift=None, out_dtype=BF16, rows=None, route=None):
    n, d = h.shape
    rows = n if rows is None else rows
    moe_resid = resid is not None and isinstance(resid[0], MoeOut)
    tm = _pick(seg.tile_unit, 128 if moe_resid else 256)
    row_spec = pl.BlockSpec((tm, d), lambda i: (i, 0))
    mod_spec = pl.BlockSpec((1, N_MOD, d), lambda i: (seg.of_row(i * tm), 0, 0))
    in_specs, args, scratch = [row_spec], [h], []
    if moe_resid:
        in_specs += [pl.BlockSpec((TOP_K, tm), lambda i: (0, i), memory_space=pltpu.SMEM),
                     pl.BlockSpec(memory_space=pl.ANY),
                     pl.BlockSpec((tm, LANES), lambda i: (i, 0)),
                     row_spec, mod_spec]
        args += [*resid[0], resid[1]]
        scratch = [pltpu.VMEM((TOP_K, tm, d // 2), U32), pltpu.SemaphoreType.DMA(())]
    elif resid is not None:
        in_specs += [row_spec, mod_spec]
        args += [resid[0], resid[1]]
    in_specs.append(pl.BlockSpec((1, d), lambda i: (0, 0)))
    args.append(gain.reshape(1, d))
    if shift is not None:
        in_specs.append(mod_spec)
        args.append(shift[0])
    if route is not None:
        in_specs += [pl.BlockSpec((N_EXPERTS, d), lambda i: (0, 0)),
                     pl.BlockSpec((N_EXPERTS, 1), lambda i: (0, 0))]
        args += [route[0], route[1].reshape(N_EXPERTS, 1)]
    out_specs, out_shape = [], []
    if resid is not None:
        out_specs.append(row_spec)
        out_shape.append(jax.ShapeDtypeStruct((rows, d), F32))
    if route is None:
        out_specs.append(row_spec)
        out_shape.append(jax.ShapeDtypeStruct((rows, d), out_dtype))
    else:
        out_specs += [pl.BlockSpec((tm, d // 2), lambda i: (i, 0)),
                      pl.BlockSpec((TOP_K, tm), lambda i: (0, i)),
                      pl.BlockSpec((TOP_K, tm), lambda i: (0, i)),
                      pl.BlockSpec((tm, LANES), lambda i: (i, 0)),
                      pl.BlockSpec((N_EXPERTS, LANES), lambda i: (0, 0))]
        out_shape += [jax.ShapeDtypeStruct((rows, d // 2), U32),
                      jax.ShapeDtypeStruct((TOP_K, rows), I32),
                      jax.ShapeDtypeStruct((TOP_K, rows), I32),
                      jax.ShapeDtypeStruct((rows, LANES), F32),
                      jax.ShapeDtypeStruct((N_EXPERTS, LANES), F32)]
    return pl.pallas_call(
        functools.partial(_rowwise_kernel, gate_idx=None if resid is None else resid[2],
                          shift_idx=None if shift is None else shift[1], route=route is not None,
                          moe_resid=moe_resid),
        grid=(rows // tm,),
        in_specs=in_specs, out_specs=out_specs, out_shape=out_shape, scratch_shapes=scratch,
        compiler_params=_cparams(1),
        name="rowwise",
    )(*args)


def _mm_plain_kernel(x_ref, w_ref, o_ref, *, scale_tiles, scale):
    acc = jnp.dot(x_ref[...], w_ref[...], preferred_element_type=F32)
    if scale_tiles:
        acc = acc * jnp.where(pl.program_id(1) < scale_tiles, scale, 1.0)
    o_ref[...] = acc.astype(o_ref.dtype)


def _mm_resid_kernel(x_ref, w_ref, h_ref, mod_ref, o_ref, *, gate_idx):
    acc = jnp.dot(x_ref[...], w_ref[...], preferred_element_type=F32)
    o_ref[...] = h_ref[...] + mod_ref[0, gate_idx:gate_idx + 1, :] * acc


def _mm_qkv_kernel(x_ref, w_ref, cos_ref, sin_ref, qn_ref, kn_ref, o_ref, *, nq_tiles, nk_tiles, qk_norm):
    acc = jnp.dot(x_ref[...], w_ref[...], preferred_element_type=F32)
    j = pl.program_id(1)

    @pl.when(j >= nq_tiles + nk_tiles)
    def _():
        o_ref[...] = acc.astype(o_ref.dtype)

    @pl.when(j < nq_tiles + nk_tiles)
    def _():
        is_q = j < nq_tiles
        post = jnp.where(is_q, HEAD_DIM ** -0.5, 1.0)
        gain = jnp.where(is_q, qn_ref[...], kn_ref[...])
        cos = cos_ref[...]
        sin = sin_ref[...]
        for s in range(0, acc.shape[1], HEAD_DIM):
            xh = acc[:, s:s + HEAD_DIM]
            if qk_norm:
                xh = xh * lax.rsqrt(jnp.mean(xh * xh, axis=-1, keepdims=True) + RMS_EPS) * gain
            xh = xh * cos + pltpu.roll(xh, HEAD_DIM // 2, axis=1) * sin
            o_ref[:, s:s + HEAD_DIM] = (xh * post).astype(o_ref.dtype)


def matmul(x, w, *, tm=1024, tn=512, out_dtype=BF16, rows=None, scale_tiles=0, scale=1.0,
           resid=None, qkv=None):
    m, k = x.shape
    m = m if rows is None else rows
    n = w.shape[1]
    tm = _pick(m if resid is None else math.gcd(m, resid[3].tile_unit), tm)
    tn = _pick(n, tn)
    grid = (m // tm, n // tn)
    x_spec = pl.BlockSpec((tm, k), lambda i, j: (i, 0))
    w_spec = pl.BlockSpec((k, tn), lambda i, j: (0, j))
    o_spec = pl.BlockSpec((tm, tn), lambda i, j: (i, j))
    if resid is not None:
        h, mods, gate_idx, seg = resid
        kern = functools.partial(_mm_resid_kernel, gate_idx=gate_idx)
        in_specs = [x_spec, w_spec, o_spec,
                    pl.BlockSpec((1, N_MOD, tn), lambda i, j: (seg.of_row(i * tm), 0, j))]
        args = (x, w, h, mods)
        out_dtype = F32
    elif qkv is not None:
        cos, sin, qn, kn, qk_norm = qkv
        kern = functools.partial(_mm_qkv_kernel, nq_tiles=N_HEADS * HEAD_DIM // tn,
                                 nk_tiles=N_KV_HEADS * HEAD_DIM // tn, qk_norm=qk_norm)
        tab_spec = pl.BlockSpec((tm, HEAD_DIM), lambda i, j: (i, 0))
        vec_spec = pl.BlockSpec((1, HEAD_DIM), lambda i, j: (0, 0))
        in_specs = [x_spec, w_spec, tab_spec, tab_spec, vec_spec, vec_spec]
        args = (x, w, cos, sin, qn.reshape(1, HEAD_DIM), kn.reshape(1, HEAD_DIM))
    else:
        kern = functools.partial(_mm_plain_kernel, scale_tiles=scale_tiles, scale=scale)
        in_specs = [x_spec, w_spec]
        args = (x, w)
    return pl.pallas_call(
        kern, grid=grid, in_specs=in_specs, out_specs=o_spec,
        out_shape=jax.ShapeDtypeStruct((m, n), out_dtype),
        compiler_params=_cparams(2),
        name="matmul",
    )(*args)


def _attn_kernel(*refs, tq, lat_keys, window, has_sink):
    it = iter(refs)
    sink_ref = next(it) if has_sink else None
    q_ref = next(it)
    kc_ref, vc_ref = next(it), next(it)
    kl_ref, vl_ref = (next(it), next(it)) if lat_keys else (None, None)
    o_ref = refs[-1]
    kvh = pl.program_id(1)
    qi = pl.program_id(2)
    nt = (((1,), (1,)), ((), ()))
    kc = kc_ref[...]
    vc = vc_ref[...]
    if lat_keys and window:
        seq = kl_ref.shape[0]
        band = min(tq + 2 * WINDOW, seq)
        start = pl.multiple_of(jnp.clip(qi * tq - WINDOW, 0, seq - band), LANES)
        kl = kl_ref[pl.ds(start, band), :]
        vl = vl_ref[pl.ds(start, band), :]
        q_pos = qi * tq + lax.broadcasted_iota(jnp.int32, (tq, band), 0)
        k_pos = start + lax.broadcasted_iota(jnp.int32, (tq, band), 1)
        in_window = jnp.abs(q_pos - k_pos) <= WINDOW
    elif lat_keys:
        kl = kl_ref[...]
        vl = vl_ref[...]
    for g in range(KV_GROUP):
        cols = slice(g * HEAD_DIM, (g + 1) * HEAD_DIM)
        qg = q_ref[:, cols]
        s_c = lax.dot_general(qg, kc, nt, preferred_element_type=F32)
        m = jnp.max(s_c, axis=-1, keepdims=True)
        if lat_keys:
            s_l = lax.dot_general(qg, kl, nt, preferred_element_type=F32)
            if window:
                s_l = jnp.where(in_window, s_l, NEG_INF)
            m = jnp.maximum(m, jnp.max(s_l, axis=-1, keepdims=True))
        if has_sink:
            sink = sink_ref[kvh * KV_GROUP + g]
            m = jnp.maximum(m, sink)
        p_c = jnp.exp(s_c - m)
        den = jnp.sum(p_c, axis=-1, keepdims=True)
        acc = jnp.dot(p_c.astype(BF16), vc, preferred_element_type=F32)
        if lat_keys:
            p_l = jnp.exp(s_l - m)
            den = den + jnp.sum(p_l, axis=-1, keepdims=True)
            acc = acc + jnp.dot(p_l.astype(BF16), vl, preferred_element_type=F32)
        if has_sink:
            den = den + jnp.exp(sink - m)
        o_ref[:, cols] = (acc / den).astype(o_ref.dtype)


def attention(qkv, dims, *, lat_queries, window, sink, out=None, out_rows=None):
    bsz, seq, n_ctx = dims
    n_lat = bsz * seq
    g_cols = KV_GROUP * HEAD_DIM
    k_col0 = N_HEADS * HEAD_DIM // HEAD_DIM
    v_col0 = k_col0 + N_KV_HEADS
    has_sink = sink is not None
    if lat_queries:
        tq = _pick(seq, 512)
        q_tiles = seq // tq
        q_row0 = 0
    else:
        tq = n_ctx
        q_tiles = 1
        q_row0 = n_lat // tq
    aliases = {}
    if out is not None:
        out_rows = out.shape[0]
    ctx_blk0 = n_lat // n_ctx
    in_specs = []
    args = []
    if has_sink:
        in_specs.append(pl.BlockSpec(memory_space=pltpu.SMEM))
        args.append(sink.astype(F32))
    in_specs += [
        pl.BlockSpec((tq, g_cols), lambda b, h, i: (q_row0 + b * q_tiles + i, h)),
        pl.BlockSpec((n_ctx, HEAD_DIM), lambda b, h, i: (ctx_blk0 + b, k_col0 + h)),
        pl.BlockSpec((n_ctx, HEAD_DIM), lambda b, h, i: (ctx_blk0 + b, v_col0 + h)),
    ]
    args += [qkv, qkv, qkv]
    if lat_queries:
        in_specs += [
            pl.BlockSpec((seq, HEAD_DIM), lambda b, h, i: (b, k_col0 + h)),
            pl.BlockSpec((seq, HEAD_DIM), lambda b, h, i: (b, v_col0 + h)),
        ]
        args += [qkv, qkv]
    if out is not None:
        aliases = {len(args): 0}
        in_specs.append(pl.BlockSpec(memory_space=pl.ANY))
        args.append(out)
    return pl.pallas_call(
        functools.partial(_attn_kernel, tq=tq, lat_keys=lat_queries, window=window, has_sink=has_sink),
        grid=(bsz, N_KV_HEADS, q_tiles),
        in_specs=in_specs,
        out_specs=pl.BlockSpec((tq, g_cols), lambda b, h, i: (q_row0 + b * q_tiles + i, h)),
        out_shape=jax.ShapeDtypeStruct((out_rows, N_HEADS * HEAD_DIM), BF16),
        input_output_aliases=aliases,
        compiler_params=_cparams(3),
        name="attention",
    )(*args)


def _mlstm_gate_kernel(x_ref, w_ref, b_ref, o_ref):
    g = jnp.dot(x_ref[...], w_ref[...], preferred_element_type=F32) + b_ref[...]
    g = GATE_CAP * jnp.tanh(g / GATE_CAP)
    log_f = jnp.minimum(g, 0.0) - jnp.log(1.0 + jnp.exp(-jnp.abs(g)))
    col = lax.broadcasted_iota(jnp.int32, g.shape, 1)
    is_f = (col // M_HEADS) % 2 == 1
    o_ref[...] = jnp.transpose(jnp.where(is_f, log_f, g))


def mlstm_gates(a, w_gate_pad, gate_b_pad):
    n, k = a.shape
    tm = _pick(n, 1024)
    return pl.pallas_call(
        _mlstm_gate_kernel,
        grid=(n // tm,),
        in_specs=[pl.BlockSpec((tm, k), lambda i: (i, 0)),
                  pl.BlockSpec((k, LANES), lambda i: (0, 0)),
                  pl.BlockSpec((1, LANES), lambda i: (0, 0))],
        out_specs=pl.BlockSpec((LANES, tm), lambda i: (0, i)),
        out_shape=jax.ShapeDtypeStruct((LANES, n), F32),
        compiler_params=_cparams(1),
        name="mlstm_gates",
    )(a, w_gate_pad, gate_b_pad)


def _mlstm_kernel(q_ref, k_ref, v_ref, gi_ref, gf_ref, o_ref, ct_ref, n_ref, m_ref):
    d = pl.program_id(1)
    c = pl.program_id(2)
    L = M_CHUNK

    @pl.when(c == 0)
    def _():
        ct_ref[...] = jnp.zeros_like(ct_ref)
        n_ref[...] = jnp.zeros_like(n_ref)
        m_ref[...] = jnp.zeros_like(m_ref)

    sgn = 1 - 2 * d
    diff = (lax.broadcasted_iota(jnp.int32, (L, L), 0) - lax.broadcasted_iota(jnp.int32, (L, L), 1)) * sgn
    mask = diff >= 0
    mask_t = diff <= 0
    eye = diff == 0
    nt = (((1,), (1,)), ((), ()))
    tn = (((0,), (0,)), ((), ()))
    for h in range(M_HEADS):
        qc = q_ref[:, h * M_QK_DIM:(h + 1) * M_QK_DIM]
        kc = k_ref[:, h * M_QK_DIM:(h + 1) * M_QK_DIM]
        vc = v_ref[:, h * M_V_DIM:(h + 1) * M_V_DIM]
        i_row = gi_ref[h:h + 1, :]
        f_row = gf_ref[h:h + 1, :]
        m_prev = m_ref[h, :, 0:1]
        n_prev = n_ref[h]
        ct_prev = ct_ref[h]
        f_col = jnp.sum(jnp.where(eye, f_row, 0.0), axis=1, keepdims=True)
        i_col = jnp.sum(jnp.where(eye, i_row, 0.0), axis=1, keepdims=True)
        b_col = jnp.sum(jnp.where(mask, f_row, 0.0), axis=1, keepdims=True)
        b_row = jnp.sum(jnp.where(mask_t, f_col, 0.0), axis=0, keepdims=True)
        log_inter = b_col + m_prev
        log_intra = jnp.where(mask, b_col - b_row + i_row, NEG_INF)
        m_t = jnp.maximum(log_inter, jnp.max(log_intra, axis=1, keepdims=True))
        w_inter = jnp.exp(log_inter - m_t)
        s_qk = lax.dot_general(qc, kc, nt, preferred_element_type=F32) * jnp.exp(log_intra - m_t)
        num = (jnp.dot(s_qk.astype(BF16), vc, preferred_element_type=F32)
               + w_inter * jnp.dot(qc, ct_prev.astype(BF16), preferred_element_type=F32))
        den = (jnp.sum(s_qk, axis=1, keepdims=True)
               + w_inter * jnp.sum(qc.astype(F32) * n_prev, axis=1, keepdims=True))
        h_out = num / jnp.maximum(jnp.abs(den), jnp.exp(-m_t))
        o_ref[0, :, h * M_V_DIM:(h + 1) * M_V_DIM] = h_out.astype(o_ref.dtype)
        b_tot = jnp.sum(f_row, axis=1, keepdims=True)
        log_w = b_tot - b_col + i_col
        m_new = jnp.maximum(b_tot + m_prev, jnp.max(log_w, axis=0, keepdims=True))
        decay = jnp.exp(b_tot + m_prev - m_new)
        w_k = jnp.exp(log_w - m_new)
        vw = (vc.astype(F32) * w_k).astype(BF16)
        ct_ref[h] = decay * ct_prev + lax.dot_general(kc, vw, tn, preferred_element_type=F32)
        n_ref[h] = decay * n_prev + jnp.sum(kc.astype(F32) * w_k, axis=0, keepdims=True)
        m_ref[h] = jnp.broadcast_to(m_new, (1, LANES))


def mlstm_scan(proj, gates_t, dims):
    bsz, seq, n_ctx = dims
    nt_rows = proj.shape[0]
    L = M_CHUNK
    ncc, nlc = n_ctx // L, seq // L
    lat_blocks = bsz * nlc
    qk_w = M_HEADS * M_QK_DIM

    def row_blk(b, d, c):
        cc = jnp.where(d == 0, c, ncc - 1 - c)
        lc = jnp.where(d == 0, c - ncc, nlc - 1 - (c - ncc))
        return jnp.where(c < ncc, lat_blocks + b * ncc + cc, b * nlc + lc)

    return pl.pallas_call(
        _mlstm_kernel,
        grid=(bsz, 2, ncc + nlc),
        in_specs=[
            pl.BlockSpec((L, qk_w), lambda b, d, c: (row_blk(b, d, c), 0)),
            pl.BlockSpec((L, qk_w), lambda b, d, c: (row_blk(b, d, c), 1)),
            pl.BlockSpec((L, D_MODEL), lambda b, d, c: (row_blk(b, d, c), 1)),
            pl.BlockSpec((M_HEADS, L), lambda b, d, c: (2 * d, row_blk(b, d, c))),
            pl.BlockSpec((M_HEADS, L), lambda b, d, c: (2 * d + 1, row_blk(b, d, c))),
        ],
        out_specs=pl.BlockSpec((1, L, D_MODEL), lambda b, d, c: (d, row_blk(b, d, c), 0)),
        out_shape=jax.ShapeDtypeStruct((2, nt_rows, D_MODEL), BF16),
        scratch_shapes=[pltpu.VMEM((M_HEADS, M_QK_DIM, M_V_DIM), F32),
                        pltpu.VMEM((M_HEADS, 1, M_QK_DIM), F32),
                        pltpu.VMEM((M_HEADS, 1, LANES), F32)],
        compiler_params=_cparams(3),
        name="mlstm_scan",
    )(proj, proj, proj, gates_t, gates_t)


def _mlstm_finish_kernel(hf_ref, hb_ref, og_ref, g_ref, o_ref):
    for h in range(M_HEADS):
        sl = slice(h * M_V_DIM, (h + 1) * M_V_DIM)
        x = hf_ref[0, :, sl].astype(F32) + hb_ref[0, :, sl].astype(F32)
        y = x * lax.rsqrt(jnp.mean(x * x, axis=-1, keepdims=True) + RMS_EPS) * g_ref[:, sl]
        o_ref[:, sl] = (y * jax.nn.sigmoid(og_ref[:, sl].astype(F32))).astype(o_ref.dtype)


def mlstm_finish(hdirs, proj, head_norm):
    _, n, d = hdirs.shape
    tm = _pick(n, 256)
    return pl.pallas_call(
        _mlstm_finish_kernel,
        grid=(n // tm,),
        in_specs=[pl.BlockSpec((1, tm, d), lambda i: (0, i, 0)),
                  pl.BlockSpec((1, tm, d), lambda i: (1, i, 0)),
                  pl.BlockSpec((tm, d), lambda i: (i, 2)),
                  pl.BlockSpec((1, d), lambda i: (0, 0))],
        out_specs=pl.BlockSpec((tm, d), lambda i: (i, 0)),
        out_shape=jax.ShapeDtypeStruct((n, d), BF16),
        compiler_params=_cparams(1),
        name="mlstm_finish",
    )(hdirs, hdirs, proj, head_norm.reshape(1, d))


def _pack_rows(x):
    half = x.shape[1] // 2
    return _pack_pair(x[:, :half], x[:, half:])


def _pack_pair(lo, hi):
    lo = lax.bitcast_convert_type(lo.astype(BF16).astype(F32), U32)
    hi = lax.bitcast_convert_type(hi.astype(BF16).astype(F32), U32)
    return (hi & jnp.uint32(0xFFFF0000)) | (lo >> 16)


def _unpack_rows(p):
    lo = lax.bitcast_convert_type(p << 16, F32)
    hi = lax.bitcast_convert_type(p & jnp.uint32(0xFFFF0000), F32)
    return lo, hi


def _route(a, w_t, bias, counts):
    tm = a.shape[0]
    per = N_EXPERTS // N_GROUPS
    nt = (((1,), (1,)), ((), ()))
    w_hi = w_t.astype(BF16)
    w_lo = (w_t - w_hi.astype(F32)).astype(BF16)
    a_hi = a.astype(BF16)
    a_lo = (a - a_hi.astype(F32)).astype(BF16)
    logits = (lax.dot_general(w_hi, a_hi, nt, preferred_element_type=F32)
              + lax.dot_general(w_lo, a_hi, nt, preferred_element_type=F32)
              + lax.dot_general(w_hi, a_lo, nt, preferred_element_type=F32))
    scores = jax.nn.sigmoid(logits).reshape(N_GROUPS, per, tm)
    biased = scores + bias.reshape(N_GROUPS, per, 1)
    e_iota = lax.broadcasted_iota(jnp.int32, (N_GROUPS, per, tm), 1).astype(F32)
    g_iota = lax.broadcasted_iota(jnp.int32, (N_GROUPS, 1, tm), 0).astype(F32)
    lin_iota = lax.broadcasted_iota(jnp.int32, (N_GROUPS, per, tm), 0).astype(F32) * per + e_iota
    m1 = jnp.max(biased, axis=1, keepdims=True)
    i1 = jnp.min(jnp.where(biased == m1, e_iota, float(per)), axis=1, keepdims=True)
    m2 = jnp.max(jnp.where(e_iota == i1, NEG_INF, biased), axis=1, keepdims=True)
    gscore = m1 + m2
    gsel = jnp.zeros(gscore.shape, F32)
    for _ in range(TOPK_GROUPS):
        cur = jnp.where(gsel > 0.0, NEG_INF, gscore)
        gm = jnp.max(cur, axis=0, keepdims=True)
        gi = jnp.min(jnp.where(cur == gm, g_iota, float(N_GROUPS)), axis=0, keepdims=True)
        gsel = jnp.where(g_iota == gi, 1.0, gsel)
    cand = jnp.where(gsel > 0.0, biased, NEG_INF)
    sel = jnp.zeros(cand.shape, F32)
    picks = []
    for _ in range(TOP_K):
        cur = jnp.where(sel > 0.0, NEG_INF, cand)
        em = jnp.max(jnp.max(cur, axis=1, keepdims=True), axis=0, keepdims=True)
        hit = jnp.where(cur == em, lin_iota, float(N_EXPERTS))
        ei = jnp.min(jnp.min(hit, axis=1, keepdims=True), axis=0, keepdims=True)
        sel = jnp.where(lin_iota == ei, 1.0, sel)
        picks.append(ei)
    sel2 = sel.reshape(N_EXPERTS, tm)
    before = lax.broadcasted_iota(jnp.int32, (tm, tm), 0) < lax.broadcasted_iota(jnp.int32, (tm, tm), 1)
    rank = jnp.dot(sel2.astype(BF16), jnp.where(before, 1.0, 0.0).astype(BF16),
                   preferred_element_type=F32) + counts
    new_counts = counts + jnp.sum(sel2, axis=1, keepdims=True)
    rank3 = rank.reshape(N_GROUPS, per, tm)
    wsum = jnp.sum(jnp.sum(jnp.where(sel > 0.0, scores, 0.0), axis=1, keepdims=True), axis=0, keepdims=True)

    def pick(ei, table):
        v = jnp.where(lin_iota == ei, table, 0.0)
        return jnp.sum(jnp.sum(v, axis=1, keepdims=True), axis=0, keepdims=True).reshape(1, tm)

    ids = jnp.concatenate([ei.reshape(1, tm) for ei in picks], axis=0).astype(I32)
    ranks = jnp.concatenate([pick(ei, rank3) for ei in picks], axis=0).astype(I32)
    w = jnp.concatenate([pick(ei, scores) for ei in picks], axis=0) / wsum.reshape(1, tm) * ROUTED_SCALE
    wmat = jnp.transpose(jnp.concatenate([w, jnp.zeros((LANES - TOP_K, tm), F32)], axis=0))
    return ids, ranks, wmat, new_counts


def _dest_kernel(starts_ref, ids_ref, rank_ref, dest_ref):
    ids = ids_ref[...]
    acc = rank_ref[...]
    for e in range(N_EXPERTS):
        acc = acc + jnp.where(ids == e, starts_ref[e], 0)
    dest_ref[...] = acc


def dest_rows(starts, ids, ranks):
    return pl.pallas_call(
        _dest_kernel,
        in_specs=[pl.BlockSpec(memory_space=pltpu.SMEM), pl.BlockSpec(memory_space=pltpu.VMEM),
                  pl.BlockSpec(memory_space=pltpu.VMEM)],
        out_specs=pl.BlockSpec(memory_space=pltpu.VMEM),
        out_shape=jax.ShapeDtypeStruct(ids.shape, I32),
        compiler_params=pltpu.CompilerParams(vmem_limit_bytes=VMEM_LIMIT),
        name="dest_rows",
    )(starts, ids, ranks)


def _dispatch_kernel(starts_ref, ends_ref, dest_ref, f_ref, xs_ref, zero_ref, sem):
    tm = f_ref.shape[0]

    def zero_copy(e):
        row0 = pl.multiple_of(ends_ref[e] - MOE_TM, MOE_TM)
        return pltpu.make_async_copy(zero_ref, xs_ref.at[pl.ds(row0, MOE_TM)], sem)

    @pl.when(pl.program_id(0) == 0)
    def _():
        zero_ref[...] = jnp.zeros_like(zero_ref)
        for e in range(N_EXPERTS):
            @pl.when(ends_ref[e] > starts_ref[e])
            def _():
                zero_copy(e).start()
        for e in range(N_EXPERTS):
            @pl.when(ends_ref[e] > starts_ref[e])
            def _():
                zero_copy(e).wait()

    def row_copy(t, d):
        return pltpu.make_async_copy(f_ref.at[pl.ds(t, 1)], xs_ref.at[pl.ds(d, 1)], sem)

    def issue(t, c):
        for k in range(TOP_K):
            row_copy(t, dest_ref[k, t]).start()
        return c

    def drain(t, c):
        for k in range(TOP_K):
            row_copy(0, 0).wait()
        return c

    lax.fori_loop(0, tm, issue, 0)
    lax.fori_loop(0, tm, drain, 0)


def dispatch(starts, ends, dest, packed, n_rows):
    n, half = packed.shape
    tm = _pick(n, 256)
    return pl.pallas_call(
        _dispatch_kernel,
        grid_spec=pltpu.PrefetchScalarGridSpec(
            num_scalar_prefetch=2, grid=(n // tm,),
            in_specs=[pl.BlockSpec((TOP_K, tm), lambda i, s, e: (0, i), memory_space=pltpu.SMEM),
                      pl.BlockSpec((tm, half), lambda i, s, e: (i, 0))],
            out_specs=pl.BlockSpec(memory_space=pl.ANY),
            scratch_shapes=[pltpu.VMEM((MOE_TM, half), U32), pltpu.SemaphoreType.DMA(())]),
        out_shape=jax.ShapeDtypeStruct((n_rows, half), U32),
        compiler_params=_cparams(1),
        name="dispatch",
    )(starts, ends, dest, packed)


def _swiglu_expert(x_ref, gu_bf, dn_bf, o_ref, pack_out):
    half = x_ref.shape[1]
    gu = None
    for c in range(0, half, MOE_CHUNK):
        lo, hi = _unpack_rows(x_ref[:, c:c + MOE_CHUNK])
        part = (jnp.dot(lo.astype(BF16), gu_bf[c:c + MOE_CHUNK, :], preferred_element_type=F32)
                + jnp.dot(hi.astype(BF16), gu_bf[half + c:half + c + MOE_CHUNK, :], preferred_element_type=F32))
        gu = part if gu is None else gu + part
    g = gu[:, :D_EXPERT]
    act = (g * jax.nn.sigmoid(g) * gu[:, D_EXPERT:]).astype(BF16)
    for c in range(0, half, MOE_CHUNK):
        y_lo = jnp.dot(act, dn_bf[:, c:c + MOE_CHUNK], preferred_element_type=F32)
        y_hi = jnp.dot(act, dn_bf[:, half + c:half + c + MOE_CHUNK], preferred_element_type=F32)
        if pack_out:
            o_ref[:, c:c + MOE_CHUNK] = _pack_pair(y_lo, y_hi)
        else:
            o_ref[:, c:c + MOE_CHUNK] = y_lo.astype(o_ref.dtype)
            o_ref[:, half + c:half + c + MOE_CHUNK] = y_hi.astype(o_ref.dtype)


def _grouped_kernel(te_ref, nt_ref, x_ref, gu_ref, dn_ref, o_ref, gu_bf, dn_bf):
    j = pl.program_id(0)

    @pl.when(j < nt_ref[0])
    def _():
        @pl.when(jnp.logical_or(j == 0, te_ref[j] != te_ref[jnp.maximum(j - 1, 0)]))
        def _():
            gu_bf[...] = gu_ref[0].astype(BF16)
            dn_bf[...] = dn_ref[0].astype(BF16)

        _swiglu_expert(x_ref, gu_bf, dn_bf, o_ref, pack_out=True)


def grouped_experts(tile_expert, n_tiles, xs, exp_gu, exp_down):
    rows, half = xs.shape
    d = 2 * half

    def tile(j, nt):
        return jnp.minimum(j, nt[0] - 1)

    return pl.pallas_call(
        _grouped_kernel,
        grid_spec=pltpu.PrefetchScalarGridSpec(
            num_scalar_prefetch=2, grid=(rows // MOE_TM,),
            in_specs=[pl.BlockSpec((MOE_TM, half), lambda j, te, nt: (tile(j, nt), 0)),
                      pl.BlockSpec((1, d, 2 * D_EXPERT), lambda j, te, nt: (te[tile(j, nt)], 0, 0)),
                      pl.BlockSpec((1, D_EXPERT, d), lambda j, te, nt: (te[tile(j, nt)], 0, 0))],
            out_specs=pl.BlockSpec((MOE_TM, half), lambda j, te, nt: (tile(j, nt), 0)),
            scratch_shapes=[pltpu.VMEM((d, 2 * D_EXPERT), BF16), pltpu.VMEM((D_EXPERT, d), BF16)]),
        out_shape=jax.ShapeDtypeStruct((rows, half), U32),
        compiler_params=_cparams(1),
        name="grouped_experts",
    )(tile_expert, n_tiles, xs, exp_gu, exp_down)


def _shared_kernel(x_ref, gu_ref, dn_ref, o_ref, gu_bf, dn_bf):
    @pl.when(pl.program_id(0) == 0)
    def _():
        gu_bf[...] = gu_ref[...].astype(BF16)
        dn_bf[...] = dn_ref[...].astype(BF16)

    _swiglu_expert(x_ref, gu_bf, dn_bf, o_ref, pack_out=False)


def shared_expert(packed, w_gu, w_dn):
    n, half = packed.shape
    tm = _pick(n, 512)

    def whole(shape):
        return pl.BlockSpec(shape, lambda i: (0,) * len(shape))

    return pl.pallas_call(
        _shared_kernel,
        grid=(n // tm,),
        in_specs=[pl.BlockSpec((tm, half), lambda i: (i, 0)), whole(w_gu.shape), whole(w_dn.shape)],
        out_specs=pl.BlockSpec((tm, 2 * half), lambda i: (i, 0)),
        out_shape=jax.ShapeDtypeStruct((n, 2 * half), BF16),
        scratch_shapes=[pltpu.VMEM(w_gu.shape, BF16), pltpu.VMEM(w_dn.shape, BF16)],
        compiler_params=_cparams(1),
        name="shared_expert",
    )(packed, w_gu, w_dn)


def _gather_combine(dest_ref, ys_ref, w_ref, sh_ref, buf_ref, sem):
    tm, d = sh_ref.shape
    half = d // 2

    def row_copy(k, t, src):
        return pltpu.make_async_copy(ys_ref.at[pl.ds(src, 1)], buf_ref.at[k, pl.ds(t, 1)], sem)

    def issue(t, c):
        for k in range(TOP_K):
            row_copy(k, t, dest_ref[k, t]).start()
        return c

    def drain(t, c):
        for k in range(TOP_K):
            row_copy(0, 0, 0).wait()
        return c

    lax.fori_loop(0, tm, issue, 0)
    lax.fori_loop(0, tm, drain, 0)
    w = w_ref[...]
    y_lo = sh_ref[:, :half].astype(F32)
    y_hi = sh_ref[:, half:].astype(F32)
    for k in range(TOP_K):
        lo, hi = _unpack_rows(buf_ref[k])
        y_lo = y_lo + w[:, k:k + 1] * lo
        y_hi = y_hi + w[:, k:k + 1] * hi
    return jnp.concatenate([y_lo, y_hi], axis=1)


class MoeOut(NamedTuple):
    dest: jax.Array
    ys: jax.Array
    wmat: jax.Array
    shared: jax.Array


def moe_experts(packed, ids, ranks, wmat, counts, moe):
    _, _, exp_gu, exp_down, shared_gu, shared_down = moe
    n = packed.shape[0]
    cnt = counts[:, 0].astype(I32)
    padded = (cnt + MOE_TM - 1) // MOE_TM * MOE_TM
    ends = jnp.cumsum(padded)
    starts = ends - padded
    n_max = n * TOP_K // MOE_TM + N_EXPERTS
    n_tiles = (ends[-1] // MOE_TM).reshape(1)
    tile_row0 = jnp.arange(n_max, dtype=I32) * MOE_TM
    tile_expert = jnp.minimum(jnp.sum(ends[None, :] <= tile_row0[:, None], axis=1), N_EXPERTS - 1).astype(I32)
    dest = dest_rows(starts, ids, ranks)
    xs = dispatch(starts, ends, dest, packed, n_max * MOE_TM)
    ys = grouped_experts(tile_expert, n_tiles, xs, exp_gu, exp_down)
    shared = shared_expert(packed, shared_gu, shared_down)
    return MoeOut(dest, ys, wmat, shared)


def _rope_tables(seq, n_rows):
    pos = jnp.arange(seq)
    row = (pos // GRID_W).astype(F32)
    col = (pos % GRID_W).astype(F32)
    n_freq = HEAD_DIM // 4
    inv_freq = ROPE_THETA ** (-jnp.arange(n_freq, dtype=F32) / n_freq)
    ang = jnp.concatenate([row[:, None] * inv_freq, col[:, None] * inv_freq], axis=-1)
    cos, sin = jnp.cos(ang), jnp.sin(ang)
    cos_full = jnp.concatenate([cos, cos], axis=-1)
    sin_full = jnp.concatenate([-sin, sin], axis=-1)
    return cos_full, sin_full


def _trunk(x, c, ctx, c_ctx, layers, final_norm):
    bsz, seq, d = x.shape
    n_ctx = ctx.shape[1]
    dims = (bsz, seq, n_ctx)
    n_lat = bsz * seq
    n_all = n_lat + bsz * n_ctx
    depth = len(layers)

    seg = _Segments(seq, bsz, bsz * n_ctx)

    h = jnp.concatenate([x.reshape(n_lat, d), ctx.reshape(bsz * n_ctx, d)], axis=0)
    mod_rows = -(-(bsz + 1) // 8) * 8
    cond = jnp.zeros((mod_rows, d), F32).at[:bsz].set(c).at[bsz].set(c_ctx)

    cos1, sin1 = _rope_tables(seq, n_all)
    cos_t = jnp.concatenate([jnp.tile(cos1, (bsz, 1)), jnp.ones((bsz * n_ctx, HEAD_DIM), F32)], axis=0)
    sin_t = jnp.concatenate([jnp.tile(sin1, (bsz, 1)), jnp.zeros((bsz * n_ctx, HEAD_DIM), F32)], axis=0)
    ones_hd = jnp.ones((HEAD_DIM,), F32)

    tables = [ada_table(cond, *layer[1]) for layer in layers]
    a, = rowwise(h, layers[0][2], seg, shift=(tables[0], 0))
    for li, (kind, ada, norm1, mixer, norm2, moe) in enumerate(layers):
        need_ctx = li < depth - 1
        rows_out = n_all if need_ctx else n_lat
        mods = tables[li]
        resid = (h, mods, 2, seg)
        if kind == "mlstm":
            w_in, gate_b, head_norm, w_o = mixer
            n_main = 2 * M_HEADS * M_QK_DIM + M_HEADS * M_V_DIM + D_MODEL
            proj = matmul(a, w_in[:, :n_main].astype(BF16), scale_tiles=M_HEADS * M_QK_DIM // 512,
                          scale=M_QK_DIM ** -0.5)
            n_gate = 4 * M_HEADS
            w_gate = jnp.zeros((d, LANES), BF16).at[:, :n_gate].set(w_in[:, n_main:].astype(BF16))
            gate_b_pad = jnp.zeros((1, LANES), F32).at[0, :n_gate].set(gate_b.astype(F32))
            gates_t = mlstm_gates(a, w_gate, gate_b_pad)
            hdirs = mlstm_scan(proj, gates_t, dims)
            mixed = mlstm_finish(hdirs, proj, head_norm)
            h = matmul(mixed, w_o.astype(BF16), resid=resid, rows=rows_out)
        else:
            if kind == "global":
                w_qkv, q_norm, k_norm, w_o = mixer
                qkv = matmul(a, w_qkv.astype(BF16), qkv=(cos_t, sin_t, q_norm, k_norm, True))
                sink = None
            else:
                w_qkv, sink, w_o = mixer
                qkv = matmul(a, w_qkv.astype(BF16), qkv=(cos_t, sin_t, ones_hd, ones_hd, False))
            mixed = attention(qkv, dims, lat_queries=True, window=kind == "swa", sink=sink, out_rows=rows_out)
            if need_ctx:
                mixed = attention(qkv, dims, lat_queries=False, window=False, sink=sink, out=mixed)
            h = matmul(mixed, w_o.astype(BF16), resid=resid, rows=rows_out)
        router_w, router_b, exp_gu, exp_down, shared_gu, shared_down = moe
        routed = rowwise(h, norm2, seg, shift=(mods, 3), rows=rows_out, route=(router_w.T, router_b))
        y = moe_experts(*routed, moe)
        if li + 1 < depth:
            h, a = rowwise(h, layers[li + 1][2], seg, resid=(y, mods, 5), shift=(tables[li + 1], 0), rows=rows_out)
        else:
            _h, out = rowwise(h, final_norm, seg, resid=(y, mods, 5), rows=rows_out, out_dtype=F32)
    return out.reshape(bsz, seq, d)


def kernel(x, c, ctx, c_ctx, l0_ada_down, l0_ada_up, l0_ada_b, l0_norm1, l0_attn_qkv, l0_q_norm, l0_k_norm, l0_attn_o, l0_norm2, l0_router_w, l0_router_b, l0_exp_gu, l0_exp_down, l0_shared_gu, l0_shared_down, l1_ada_down, l1_ada_up, l1_ada_b, l1_norm1, l1_mlstm_in, l1_mlstm_gate_b, l1_mlstm_head_norm, l1_mlstm_o, l1_norm2, l1_router_w, l1_router_b, l1_exp_gu, l1_exp_down, l1_shared_gu, l1_shared_down, l2_ada_down, l2_ada_up, l2_ada_b, l2_norm1, l2_swa_qkv, l2_swa_sink, l2_swa_o, l2_norm2, l2_router_w, l2_router_b, l2_exp_gu, l2_exp_down, l2_shared_gu, l2_shared_down, l3_ada_down, l3_ada_up, l3_ada_b, l3_norm1, l3_attn_qkv, l3_q_norm, l3_k_norm, l3_attn_o, l3_norm2, l3_router_w, l3_router_b, l3_exp_gu, l3_exp_down, l3_shared_gu, l3_shared_down, final_norm):
    layers = [
        ("global", (l0_ada_down, l0_ada_up, l0_ada_b), l0_norm1, (l0_attn_qkv, l0_q_norm, l0_k_norm, l0_attn_o), l0_norm2,
         (l0_router_w, l0_router_b, l0_exp_gu, l0_exp_down, l0_shared_gu, l0_shared_down)),
        ("mlstm", (l1_ada_down, l1_ada_up, l1_ada_b), l1_norm1, (l1_mlstm_in, l1_mlstm_gate_b, l1_mlstm_head_norm, l1_mlstm_o), l1_norm2,
         (l1_router_w, l1_router_b, l1_exp_gu, l1_exp_down, l1_shared_gu, l1_shared_down)),
        ("swa", (l2_ada_down, l2_ada_up, l2_ada_b), l2_norm1, (l2_swa_qkv, l2_swa_sink, l2_swa_o), l2_norm2,
         (l2_router_w, l2_router_b, l2_exp_gu, l2_exp_down, l2_shared_gu, l2_shared_down)),
        ("global", (l3_ada_down, l3_ada_up, l3_ada_b), l3_norm1, (l3_attn_qkv, l3_q_norm, l3_k_norm, l3_attn_o), l3_norm2,
         (l3_router_w, l3_router_b, l3_exp_gu, l3_exp_down, l3_shared_gu, l3_shared_down)),
    ]
    return _trunk(x, c, ctx, c_ctx, layers, final_norm)
```

```python
import functools
import math
from typing import NamedTuple

import jax
import jax.numpy as jnp
from jax import lax
from jax.experimental import pallas as pl
from jax.experimental.pallas import tpu as pltpu

F32 = jnp.float32
BF16 = jnp.bfloat16
U32 = jnp.uint32
I32 = jnp.int32

D_MODEL = 4096
GRID_W = 64
RMS_EPS = 1e-6
N_MOD = 6
N_HEADS = 32
N_KV_HEADS = 8
HEAD_DIM = D_MODEL // N_HEADS
KV_GROUP = N_HEADS // N_KV_HEADS
ROPE_THETA = 10000.0
WINDOW = 128
M_HEADS = 8
M_V_DIM = D_MODEL // M_HEADS
M_QK_DIM = M_V_DIM // 2
M_CHUNK = 128
GATE_CAP = 15.0
N_EXPERTS = 64
TOP_K = 8
N_GROUPS = 8
TOPK_GROUPS = 4
D_EXPERT = 192
ROUTED_SCALE = 2.5

MOE_TM = 512
MOE_CHUNK = 512
QKV_ROWS = 256
LANES = 128
VMEM_LIMIT = 56 * 1024 * 1024
NEG_INF = float("-inf")


def _cparams(n_axes):
    return pltpu.CompilerParams(dimension_semantics=("arbitrary",) * n_axes,
                                vmem_limit_bytes=VMEM_LIMIT)


def _pick(n, pref):
    t = pref
    while n % t:
        t //= 2
    return t


class _Segments(NamedTuple):
    seq: int
    bsz: int
    n_ctx_rows: int

    @property
    def tile_unit(self):
        return math.gcd(self.seq, self.n_ctx_rows)

    def of_row(self, row):
        return jnp.minimum(row // self.seq, self.bsz)


def _ada_kernel(cond_ref, down_ref, up_ref, b_ref, out_ref):
    c = cond_ref[...]
    t = jnp.dot(c * jax.nn.sigmoid(c), down_ref[...], precision=lax.Precision.HIGHEST,
                preferred_element_type=F32)
    out_ref[...] = jnp.dot(t, up_ref[...], precision=lax.Precision.HIGHEST,
                           preferred_element_type=F32) + b_ref[...]


def ada_table(cond_pad, down, up, bias):
    r, d = cond_pad.shape
    rank = down.shape[1]
    n = up.shape[1]
    tn = 2048
    out = pl.pallas_call(
        _ada_kernel,
        grid=(n // tn,),
        in_specs=[pl.BlockSpec((r, d), lambda j: (0, 0)),
                  pl.BlockSpec((d, rank), lambda j: (0, 0)),
                  pl.BlockSpec((rank, tn), lambda j: (0, j)),
                  pl.BlockSpec((1, tn), lambda j: (0, j))],
        out_specs=pl.BlockSpec((r, tn), lambda j: (0, j)),
        out_shape=jax.ShapeDtypeStruct((r, n), F32),
        compiler_params=_cparams(1),
        name="ada_table",
    )(cond_pad, down, up, bias.reshape(1, n))
    return out.reshape(r, N_MOD, d)


def _rowwise_kernel(*refs, gate_idx, shift_idx, route, moe_resid):
    it = iter(refs)
    h_ref = next(it)
    if moe_resid:
        y = _gather_combine(next(it), next(it), next(it), next(it), next(it), refs[-2], refs[-1])
    elif gate_idx is not None:
        y = next(it)[...].astype(F32)
    gmod_ref = next(it) if gate_idx is not None else None
    g_ref = next(it)
    smod_ref = next(it) if shift_idx is not None else None
    wt_ref, rb_ref = (next(it), next(it)) if route else (None, None)
    h = h_ref[...]
    if gate_idx is not None:
        h = h + gmod_ref[0, gate_idx:gate_idx + 1, :] * y
        next(it)[...] = h
    a = h * lax.rsqrt(jnp.mean(h * h, axis=-1, keepdims=True) + RMS_EPS) * g_ref[...]
    if shift_idx is not None:
        a = a * (1.0 + smod_ref[0, shift_idx + 1:shift_idx + 2, :]) + smod_ref[0, shift_idx:shift_idx + 1, :]
    if not route:
        a_ref = next(it)
        a_ref[...] = a.astype(a_ref.dtype)
        return
    packed_ref, ids_ref, rank_ref, w_ref, cnt_ref = (next(it) for _ in range(5))

    @pl.when(pl.program_id(0) == 0)
    def _():
        cnt_ref[...] = jnp.zeros_like(cnt_ref)

    packed_ref[...] = _pack_rows(a)
    ids, ranks, wmat, counts = _route(a, wt_ref[...], rb_ref[...], cnt_ref[:, 0:1])
    ids_ref[...] = ids
    rank_ref[...] = ranks
    w_ref[...] = wmat
    cnt_ref[...] = jnp.broadcast_to(counts, cnt_ref.shape)


def rowwise(h, gain, seg, *, resid=None, shift=None, out_dtype=BF16, rows=None, route=None):
    n, d = h.shape
    rows = n if rows is None else rows
    moe_resid = resid is not None and isinstance(resid[0], MoeOut)
    tm = _pick(seg.tile_unit, 128 if moe_resid else 256)
    row_spec = pl.BlockSpec((tm, d), lambda i: (i, 0))
    mod_spec = pl.BlockSpec((1, N_MOD, d), lambda i: (seg.of_row(i * tm), 0, 0))
    in_specs, args, scratch = [row_spec], [h], []
    if moe_resid:
        last = rows // tm - 1
        in_specs += [pl.BlockSpec((TOP_K, tm), lambda i: (0, i), memory_space=pltpu.SMEM),
                     pl.BlockSpec((TOP_K, tm), lambda i: (0, jnp.minimum(i + 1, last)), memory_space=pltpu.SMEM),
                     pl.BlockSpec(memory_space=pl.ANY),
                     pl.BlockSpec((tm, LANES), lambda i: (i, 0)),
                     row_spec, mod_spec]
        moe_out = resid[0]
        args += [moe_out.dest, moe_out.dest, moe_out.ys, moe_out.wmat, moe_out.shared, resid[1]]
        scratch = [pltpu.VMEM((2, TOP_K, tm, d // 2), U32), pltpu.SemaphoreType.DMA((2,))]
    elif resid is not None:
        in_specs += [row_spec, mod_spec]
        args += [resid[0], resid[1]]
    in_specs.append(pl.BlockSpec((1, d), lambda i: (0, 0)))
    args.append(gain.reshape(1, d))
    if shift is not None:
        in_specs.append(mod_spec)
        args.append(shift[0])
    if route is not None:
        in_specs += [pl.BlockSpec((N_EXPERTS, d), lambda i: (0, 0)),
                     pl.BlockSpec((N_EXPERTS, 1), lambda i: (0, 0))]
        args += [route[0], route[1].reshape(N_EXPERTS, 1)]
    out_specs, out_shape = [], []
    if resid is not None:
        out_specs.append(row_spec)
        out_shape.append(jax.ShapeDtypeStruct((rows, d), F32))
    if route is None:
        out_specs.append(row_spec)
        out_shape.append(jax.ShapeDtypeStruct((rows, d), out_dtype))
    else:
        out_specs += [pl.BlockSpec((tm, d // 2), lambda i: (i, 0)),
                      pl.BlockSpec((TOP_K, tm), lambda i: (0, i)),
                      pl.BlockSpec((TOP_K, tm), lambda i: (0, i)),
                      pl.BlockSpec((tm, LANES), lambda i: (i, 0)),
                      pl.BlockSpec((N_EXPERTS, LANES), lambda i: (0, 0))]
        out_shape += [jax.ShapeDtypeStruct((rows, d // 2), U32),
                      jax.ShapeDtypeStruct((TOP_K, rows), I32),
                      jax.ShapeDtypeStruct((TOP_K, rows), I32),
                      jax.ShapeDtypeStruct((rows, LANES), F32),
                      jax.ShapeDtypeStruct((N_EXPERTS, LANES), F32)]
    return pl.pallas_call(
        functools.partial(_rowwise_kernel, gate_idx=None if resid is None else resid[2],
                          shift_idx=None if shift is None else shift[1], route=route is not None,
                          moe_resid=moe_resid),
        grid=(rows // tm,),
        in_specs=in_specs, out_specs=out_specs, out_shape=out_shape, scratch_shapes=scratch,
        compiler_params=_cparams(1),
        name="rowwise",
    )(*args)


def _mm_plain_kernel(x_ref, w_ref, o_ref, *, scale_tiles, scale):
    acc = jnp.dot(x_ref[...], w_ref[...], preferred_element_type=F32)
    if scale_tiles:
        acc = acc * jnp.where(pl.program_id(1) < scale_tiles, scale, 1.0)
    o_ref[...] = acc.astype(o_ref.dtype)


def _mm_resid_kernel(x_ref, w_ref, h_ref, mod_ref, o_ref, *, gate_idx):
    acc = jnp.dot(x_ref[...], w_ref[...], preferred_element_type=F32)
    o_ref[...] = h_ref[...] + mod_ref[0, gate_idx:gate_idx + 1, :] * acc


def _mm_qkv_kernel(x_ref, w_ref, cos_ref, sin_ref, qn_ref, kn_ref, o_ref, *, nq_tiles, nk_tiles, qk_norm):
    j = pl.program_id(1)
    is_q = j < nq_tiles
    is_qk = j < nq_tiles + nk_tiles
    post = jnp.where(is_q, HEAD_DIM ** -0.5, 1.0)
    gain = jnp.where(is_q, qn_ref[...], kn_ref[...])
    w = w_ref[...]
    tm, tn = o_ref.shape
    for r in range(0, tm, QKV_ROWS):
        acc = jnp.dot(x_ref[r:r + QKV_ROWS, :], w, preferred_element_type=F32)
        cos = cos_ref[r:r + QKV_ROWS, :]
        sin = sin_ref[r:r + QKV_ROWS, :]
        for s in range(0, tn, HEAD_DIM):
            raw = acc[:, s:s + HEAD_DIM]
            xh = raw
            if qk_norm:
                xh = xh * lax.rsqrt(jnp.mean(xh * xh, axis=-1, keepdims=True) + RMS_EPS) * gain
            xh = (xh * cos + pltpu.roll(xh, HEAD_DIM // 2, axis=1) * sin) * post
            o_ref[r:r + QKV_ROWS, s:s + HEAD_DIM] = jnp.where(is_qk, xh, raw).astype(o_ref.dtype)


def matmul(x, w, *, tm=1024, tn=512, out_dtype=BF16, rows=None, scale_tiles=0, scale=1.0,
           resid=None, qkv=None):
    m, k = x.shape
    m = m if rows is None else rows
    n = w.shape[1]
    tm = _pick(m if resid is None else math.gcd(m, resid[3].tile_unit), tm)
    tn = _pick(n, tn)
    grid = (m // tm, n // tn)
    x_spec = pl.BlockSpec((tm, k), lambda i, j: (i, 0))
    w_spec = pl.BlockSpec((k, tn), lambda i, j: (0, j))
    o_spec = pl.BlockSpec((tm, tn), lambda i, j: (i, j))
    if resid is not None:
        h, mods, gate_idx, seg = resid
        kern = functools.partial(_mm_resid_kernel, gate_idx=gate_idx)
        in_specs = [x_spec, w_spec, o_spec,
                    pl.BlockSpec((1, N_MOD, tn), lambda i, j: (seg.of_row(i * tm), 0, j))]
        args = (x, w, h, mods)
        out_dtype = F32
    elif qkv is not None:
        cos, sin, qn, kn, qk_norm = qkv
        kern = functools.partial(_mm_qkv_kernel, nq_tiles=N_HEADS * HEAD_DIM // tn,
                                 nk_tiles=N_KV_HEADS * HEAD_DIM // tn, qk_norm=qk_norm)
        tab_spec = pl.BlockSpec((tm, HEAD_DIM), lambda i, j: (i, 0))
        vec_spec = pl.BlockSpec((1, HEAD_DIM), lambda i, j: (0, 0))
        in_specs = [x_spec, w_spec, tab_spec, tab_spec, vec_spec, vec_spec]
        args = (x, w, cos, sin, qn.reshape(1, HEAD_DIM), kn.reshape(1, HEAD_DIM))
    else:
        kern = functools.partial(_mm_plain_kernel, scale_tiles=scale_tiles, scale=scale)
        in_specs = [x_spec, w_spec]
        args = (x, w)
    return pl.pallas_call(
        kern, grid=grid, in_specs=in_specs, out_specs=o_spec,
        out_shape=jax.ShapeDtypeStruct((m, n), out_dtype),
        compiler_params=_cparams(2),
        name="matmul",
    )(*args)


def _attn_kernel(*refs, tq, lat_keys, window, has_sink):
    it = iter(refs)
    sink_ref = next(it) if has_sink else None
    q_ref = next(it)
    kc_ref, vc_ref = next(it), next(it)
    kl_ref, vl_ref = (next(it), next(it)) if lat_keys else (None, None)
    o_ref = refs[-1]
    kvh = pl.program_id(1)
    qi = pl.program_id(2)
    nt = (((1,), (1,)), ((), ()))
    kc = kc_ref[...]
    vc = vc_ref[...]
    if lat_keys and window:
        seq = kl_ref.shape[0]
        band = min(tq + 2 * WINDOW, seq)
        start = pl.multiple_of(jnp.clip(qi * tq - WINDOW, 0, seq - band), LANES)
        kl = kl_ref[pl.ds(start, band), :]
        vl = vl_ref[pl.ds(start, band), :]
        q_pos = qi * tq + lax.broadcasted_iota(jnp.int32, (tq, band), 0)
        k_pos = start + lax.broadcasted_iota(jnp.int32, (tq, band), 1)
        in_window = jnp.abs(q_pos - k_pos) <= WINDOW
    elif lat_keys:
        kl = kl_ref[...]
        vl = vl_ref[...]
    for g in range(KV_GROUP):
        cols = slice(g * HEAD_DIM, (g + 1) * HEAD_DIM)
        qg = q_ref[:, cols]
        s_c = lax.dot_general(qg, kc, nt, preferred_element_type=F32)
        m = jnp.max(s_c, axis=-1, keepdims=True)
        if lat_keys:
            s_l = lax.dot_general(qg, kl, nt, preferred_element_type=F32)
            if window:
                s_l = jnp.where(in_window, s_l, NEG_INF)
            m = jnp.maximum(m, jnp.max(s_l, axis=-1, keepdims=True))
        if has_sink:
            sink = sink_ref[kvh * KV_GROUP + g]
            m = jnp.maximum(m, sink)
        p_c = jnp.exp(s_c - m)
        den = jnp.sum(p_c, axis=-1, keepdims=True)
        acc = jnp.dot(p_c.astype(BF16), vc, preferred_element_type=F32)
        if lat_keys:
            p_l = jnp.exp(s_l - m)
            den = den + jnp.sum(p_l, axis=-1, keepdims=True)
            acc = acc + jnp.dot(p_l.astype(BF16), vl, preferred_element_type=F32)
        if has_sink:
            den = den + jnp.exp(sink - m)
        o_ref[:, cols] = (acc / den).astype(o_ref.dtype)


def attention(qkv, dims, *, lat_queries, window, sink, out=None, out_rows=None):
    bsz, seq, n_ctx = dims
    n_lat = bsz * seq
    g_cols = KV_GROUP * HEAD_DIM
    k_col0 = N_HEADS * HEAD_DIM // HEAD_DIM
    v_col0 = k_col0 + N_KV_HEADS
    has_sink = sink is not None
    if lat_queries:
        tq = _pick(seq, 512)
        q_tiles = seq // tq
        q_row0 = 0
    else:
        tq = n_ctx
        q_tiles = 1
        q_row0 = n_lat // tq
    aliases = {}
    if out is not None:
        out_rows = out.shape[0]
    ctx_blk0 = n_lat // n_ctx
    in_specs = []
    args = []
    if has_sink:
        in_specs.append(pl.BlockSpec(memory_space=pltpu.SMEM))
        args.append(sink.astype(F32))
    in_specs += [
        pl.BlockSpec((tq, g_cols), lambda b, h, i: (q_row0 + b * q_tiles + i, h)),
        pl.BlockSpec((n_ctx, HEAD_DIM), lambda b, h, i: (ctx_blk0 + b, k_col0 + h)),
        pl.BlockSpec((n_ctx, HEAD_DIM), lambda b, h, i: (ctx_blk0 + b, v_col0 + h)),
    ]
    args += [qkv, qkv, qkv]
    if lat_queries:
        in_specs += [
            pl.BlockSpec((seq, HEAD_DIM), lambda b, h, i: (b, k_col0 + h)),
            pl.BlockSpec((seq, HEAD_DIM), lambda b, h, i: (b, v_col0 + h)),
        ]
        args += [qkv, qkv]
    if out is not None:
        aliases = {len(args): 0}
        in_specs.append(pl.BlockSpec(memory_space=pl.ANY))
        args.append(out)
    return pl.pallas_call(
        functools.partial(_attn_kernel, tq=tq, lat_keys=lat_queries, window=window, has_sink=has_sink),
        grid=(bsz, N_KV_HEADS, q_tiles),
        in_specs=in_specs,
        out_specs=pl.BlockSpec((tq, g_cols), lambda b, h, i: (q_row0 + b * q_tiles + i, h)),
        out_shape=jax.ShapeDtypeStruct((out_rows, N_HEADS * HEAD_DIM), BF16),
        input_output_aliases=aliases,
        compiler_params=_cparams(3),
        name="attention",
    )(*args)


def _mlstm_gate_kernel(x_ref, w_ref, b_ref, o_ref):
    g = jnp.dot(x_ref[...], w_ref[...], preferred_element_type=F32) + b_ref[...]
    g = GATE_CAP * jnp.tanh(g / GATE_CAP)
    log_f = jnp.minimum(g, 0.0) - jnp.log(1.0 + jnp.exp(-jnp.abs(g)))
    col = lax.broadcasted_iota(jnp.int32, g.shape, 1)
    is_f = (col // M_HEADS) % 2 == 1
    o_ref[...] = jnp.transpose(jnp.where(is_f, log_f, g))


def mlstm_gates(a, w_gate_pad, gate_b_pad):
    n, k = a.shape
    tm = _pick(n, 1024)
    return pl.pallas_call(
        _mlstm_gate_kernel,
        grid=(n // tm,),
        in_specs=[pl.BlockSpec((tm, k), lambda i: (i, 0)),
                  pl.BlockSpec((k, LANES), lambda i: (0, 0)),
                  pl.BlockSpec((1, LANES), lambda i: (0, 0))],
        out_specs=pl.BlockSpec((LANES, tm), lambda i: (0, i)),
        out_shape=jax.ShapeDtypeStruct((LANES, n), F32),
        compiler_params=_cparams(1),
        name="mlstm_gates",
    )(a, w_gate_pad, gate_b_pad)


def _mlstm_kernel(q_ref, k_ref, v_ref, gi_ref, gf_ref, o_ref, ct_ref, n_ref, m_ref):
    d = pl.program_id(1)
    c = pl.program_id(2)
    L = M_CHUNK

    @pl.when(c == 0)
    def _():
        ct_ref[...] = jnp.zeros_like(ct_ref)
        n_ref[...] = jnp.zeros_like(n_ref)
        m_ref[...] = jnp.zeros_like(m_ref)

    sgn = 1 - 2 * d
    diff = (lax.broadcasted_iota(jnp.int32, (L, L), 0) - lax.broadcasted_iota(jnp.int32, (L, L), 1)) * sgn
    mask = diff >= 0
    mask_t = diff <= 0
    eye = diff == 0
    nt = (((1,), (1,)), ((), ()))
    tn = (((0,), (0,)), ((), ()))
    for h in range(M_HEADS):
        qc = q_ref[:, h * M_QK_DIM:(h + 1) * M_QK_DIM]
        kc = k_ref[:, h * M_QK_DIM:(h + 1) * M_QK_DIM]
        vc = v_ref[:, h * M_V_DIM:(h + 1) * M_V_DIM]
        i_row = gi_ref[h:h + 1, :]
        f_row = gf_ref[h:h + 1, :]
        m_prev = m_ref[h, :, 0:1]
        n_prev = n_ref[h]
        ct_prev = ct_ref[h]
        f_col = jnp.sum(jnp.where(eye, f_row, 0.0), axis=1, keepdims=True)
        i_col = jnp.sum(jnp.where(eye, i_row, 0.0), axis=1, keepdims=True)
        b_col = jnp.sum(jnp.where(mask, f_row, 0.0), axis=1, keepdims=True)
        b_row = jnp.sum(jnp.where(mask_t, f_col, 0.0), axis=0, keepdims=True)
        log_inter = b_col + m_prev
        log_intra = jnp.where(mask, b_col - b_row + i_row, NEG_INF)
        m_t = jnp.maximum(log_inter, jnp.max(log_intra, axis=1, keepdims=True))
        w_inter = jnp.exp(log_inter - m_t)
        s_qk = lax.dot_general(qc, kc, nt, preferred_element_type=F32) * jnp.exp(log_intra - m_t)
        num = (jnp.dot(s_qk.astype(BF16), vc, preferred_element_type=F32)
               + w_inter * jnp.dot(qc, ct_prev.astype(BF16), preferred_element_type=F32))
        den = (jnp.sum(s_qk, axis=1, keepdims=True)
               + w_inter * jnp.sum(qc.astype(F32) * n_prev, axis=1, keepdims=True))
        h_out = num / jnp.maximum(jnp.abs(den), jnp.exp(-m_t))
        o_ref[0, :, h * M_V_DIM:(h + 1) * M_V_DIM] = h_out.astype(o_ref.dtype)
        b_tot = jnp.sum(f_row, axis=1, keepdims=True)
        log_w = b_tot - b_col + i_col
        m_new = jnp.maximum(b_tot + m_prev, jnp.max(log_w, axis=0, keepdims=True))
        decay = jnp.exp(b_tot + m_prev - m_new)
        w_k = jnp.exp(log_w - m_new)
        vw = (vc.astype(F32) * w_k).astype(BF16)
        ct_ref[h] = decay * ct_prev + lax.dot_general(kc, vw, tn, preferred_element_type=F32)
        n_ref[h] = decay * n_prev + jnp.sum(kc.astype(F32) * w_k, axis=0, keepdims=True)
        m_ref[h] = jnp.broadcast_to(m_new, (1, LANES))


def mlstm_scan(proj, gates_t, dims):
    bsz, seq, n_ctx = dims
    nt_rows = proj.shape[0]
    L = M_CHUNK
    ncc, nlc = n_ctx // L, seq // L
    lat_blocks = bsz * nlc
    qk_w = M_HEADS * M_QK_DIM

    def row_blk(b, d, c):
        cc = jnp.where(d == 0, c, ncc - 1 - c)
        lc = jnp.where(d == 0, c - ncc, nlc - 1 - (c - ncc))
        return jnp.where(c < ncc, lat_blocks + b * ncc + cc, b * nlc + lc)

    return pl.pallas_call(
        _mlstm_kernel,
        grid=(bsz, 2, ncc + nlc),
        in_specs=[
            pl.BlockSpec((L, qk_w), lambda b, d, c: (row_blk(b, d, c), 0)),
            pl.BlockSpec((L, qk_w), lambda b, d, c: (row_blk(b, d, c), 1)),
            pl.BlockSpec((L, D_MODEL), lambda b, d, c: (row_blk(b, d, c), 1)),
            pl.BlockSpec((M_HEADS, L), lambda b, d, c: (2 * d, row_blk(b, d, c))),
            pl.BlockSpec((M_HEADS, L), lambda b, d, c: (2 * d + 1, row_blk(b, d, c))),
        ],
        out_specs=pl.BlockSpec((1, L, D_MODEL), lambda b, d, c: (d, row_blk(b, d, c), 0)),
        out_shape=jax.ShapeDtypeStruct((2, nt_rows, D_MODEL), BF16),
        scratch_shapes=[pltpu.VMEM((M_HEADS, M_QK_DIM, M_V_DIM), F32),
                        pltpu.VMEM((M_HEADS, 1, M_QK_DIM), F32),
                        pltpu.VMEM((M_HEADS, 1, LANES), F32)],
        compiler_params=_cparams(3),
        name="mlstm_scan",
    )(proj, proj, proj, gates_t, gates_t)


def _mlstm_finish_kernel(hf_ref, hb_ref, og_ref, g_ref, o_ref):
    for h in range(M_HEADS):
        sl = slice(h * M_V_DIM, (h + 1) * M_V_DIM)
        x = hf_ref[0, :, sl].astype(F32) + hb_ref[0, :, sl].astype(F32)
        y = x * lax.rsqrt(jnp.mean(x * x, axis=-1, keepdims=True) + RMS_EPS) * g_ref[:, sl]
        o_ref[:, sl] = (y * jax.nn.sigmoid(og_ref[:, sl].astype(F32))).astype(o_ref.dtype)


def mlstm_finish(hdirs, proj, head_norm):
    _, n, d = hdirs.shape
    tm = _pick(n, 256)
    return pl.pallas_call(
        _mlstm_finish_kernel,
        grid=(n // tm,),
        in_specs=[pl.BlockSpec((1, tm, d), lambda i: (0, i, 0)),
                  pl.BlockSpec((1, tm, d), lambda i: (1, i, 0)),
                  pl.BlockSpec((tm, d), lambda i: (i, 2)),
                  pl.BlockSpec((1, d), lambda i: (0, 0))],
        out_specs=pl.BlockSpec((tm, d), lambda i: (i, 0)),
        out_shape=jax.ShapeDtypeStruct((n, d), BF16),
        compiler_params=_cparams(1),
        name="mlstm_finish",
    )(hdirs, hdirs, proj, head_norm.reshape(1, d))


def _pack_rows(x):
    half = x.shape[1] // 2
    return _pack_pair(x[:, :half], x[:, half:])


def _pack_pair(lo, hi):
    lo = lax.bitcast_convert_type(lo.astype(BF16).astype(F32), U32)
    hi = lax.bitcast_convert_type(hi.astype(BF16).astype(F32), U32)
    return (hi & jnp.uint32(0xFFFF0000)) | (lo >> 16)


def _unpack_rows(p):
    lo = lax.bitcast_convert_type(p << 16, F32)
    hi = lax.bitcast_convert_type(p & jnp.uint32(0xFFFF0000), F32)
    return lo, hi


def _route(a, w_t, bias, counts):
    tm = a.shape[0]
    per = N_EXPERTS // N_GROUPS
    nt = (((1,), (1,)), ((), ()))
    w_hi = w_t.astype(BF16)
    w_lo = (w_t - w_hi.astype(F32)).astype(BF16)
    a_hi = a.astype(BF16)
    a_lo = (a - a_hi.astype(F32)).astype(BF16)
    logits = (lax.dot_general(w_hi, a_hi, nt, preferred_element_type=F32)
              + lax.dot_general(w_lo, a_hi, nt, preferred_element_type=F32)
              + lax.dot_general(w_hi, a_lo, nt, preferred_element_type=F32))
    scores = jax.nn.sigmoid(logits).reshape(N_GROUPS, per, tm)
    biased = scores + bias.reshape(N_GROUPS, per, 1)
    e_iota = lax.broadcasted_iota(jnp.int32, (N_GROUPS, per, tm), 1).astype(F32)
    g_iota = lax.broadcasted_iota(jnp.int32, (N_GROUPS, 1, tm), 0).astype(F32)
    lin_iota = lax.broadcasted_iota(jnp.int32, (N_GROUPS, per, tm), 0).astype(F32) * per + e_iota
    m1 = jnp.max(biased, axis=1, keepdims=True)
    i1 = jnp.min(jnp.where(biased == m1, e_iota, float(per)), axis=1, keepdims=True)
    m2 = jnp.max(jnp.where(e_iota == i1, NEG_INF, biased), axis=1, keepdims=True)
    gscore = m1 + m2
    gsel = jnp.zeros(gscore.shape, F32)
    for _ in range(TOPK_GROUPS):
        cur = jnp.where(gsel > 0.0, NEG_INF, gscore)
        gm = jnp.max(cur, axis=0, keepdims=True)
        gi = jnp.min(jnp.where(cur == gm, g_iota, float(N_GROUPS)), axis=0, keepdims=True)
        gsel = jnp.where(g_iota == gi, 1.0, gsel)
    cand = jnp.where(gsel > 0.0, biased, NEG_INF)
    sel = jnp.zeros(cand.shape, F32)
    picks = []
    for _ in range(TOP_K):
        cur = jnp.where(sel > 0.0, NEG_INF, cand)
        em = jnp.max(jnp.max(cur, axis=1, keepdims=True), axis=0, keepdims=True)
        hit = jnp.where(cur == em, lin_iota, float(N_EXPERTS))
        ei = jnp.min(jnp.min(hit, axis=1, keepdims=True), axis=0, keepdims=True)
        sel = jnp.where(lin_iota == ei, 1.0, sel)
        picks.append(ei)
    sel2 = sel.reshape(N_EXPERTS, tm)
    before = lax.broadcasted_iota(jnp.int32, (tm, tm), 0) < lax.broadcasted_iota(jnp.int32, (tm, tm), 1)
    rank = jnp.dot(sel2.astype(BF16), jnp.where(before, 1.0, 0.0).astype(BF16),
                   preferred_element_type=F32) + counts
    new_counts = counts + jnp.sum(sel2, axis=1, keepdims=True)
    rank3 = rank.reshape(N_GROUPS, per, tm)
    wsum = jnp.sum(jnp.sum(jnp.where(sel > 0.0, scores, 0.0), axis=1, keepdims=True), axis=0, keepdims=True)

    def pick(ei, table):
        v = jnp.where(lin_iota == ei, table, 0.0)
        return jnp.sum(jnp.sum(v, axis=1, keepdims=True), axis=0, keepdims=True).reshape(1, tm)

    ids = jnp.concatenate([ei.reshape(1, tm) for ei in picks], axis=0).astype(I32)
    ranks = jnp.concatenate([pick(ei, rank3) for ei in picks], axis=0).astype(I32)
    w = jnp.concatenate([pick(ei, scores) for ei in picks], axis=0) / wsum.reshape(1, tm) * ROUTED_SCALE
    wmat = jnp.transpose(jnp.concatenate([w, jnp.zeros((LANES - TOP_K, tm), F32)], axis=0))
    return ids, ranks, wmat, new_counts


def _dest_kernel(starts_ref, ids_ref, rank_ref, dest_ref):
    ids = ids_ref[...]
    acc = rank_ref[...]
    for e in range(N_EXPERTS):
        acc = acc + jnp.where(ids == e, starts_ref[e], 0)
    dest_ref[...] = acc


def dest_rows(starts, ids, ranks):
    return pl.pallas_call(
        _dest_kernel,
        in_specs=[pl.BlockSpec(memory_space=pltpu.SMEM), pl.BlockSpec(memory_space=pltpu.VMEM),
                  pl.BlockSpec(memory_space=pltpu.VMEM)],
        out_specs=pl.BlockSpec(memory_space=pltpu.VMEM),
        out_shape=jax.ShapeDtypeStruct(ids.shape, I32),
        compiler_params=pltpu.CompilerParams(vmem_limit_bytes=VMEM_LIMIT),
        name="dest_rows",
    )(starts, ids, ranks)


def _dispatch_kernel(starts_ref, ends_ref, dest_ref, f_ref, xs_ref, zero_ref, sem):
    tm = f_ref.shape[0]

    def zero_copy(e):
        row0 = pl.multiple_of(ends_ref[e] - MOE_TM, MOE_TM)
        return pltpu.make_async_copy(zero_ref, xs_ref.at[pl.ds(row0, MOE_TM)], sem)

    @pl.when(pl.program_id(0) == 0)
    def _():
        zero_ref[...] = jnp.zeros_like(zero_ref)
        for e in range(N_EXPERTS):
            @pl.when(ends_ref[e] > starts_ref[e])
            def _():
                zero_copy(e).start()
        for e in range(N_EXPERTS):
            @pl.when(ends_ref[e] > starts_ref[e])
            def _():
                zero_copy(e).wait()

    def row_copy(t, d):
        return pltpu.make_async_copy(f_ref.at[pl.ds(t, 1)], xs_ref.at[pl.ds(d, 1)], sem)

    def issue(t, c):
        for k in range(TOP_K):
            row_copy(t, dest_ref[k, t]).start()
        return c

    def drain(t, c):
        for k in range(TOP_K):
            row_copy(0, 0).wait()
        return c

    lax.fori_loop(0, tm, issue, 0)
    lax.fori_loop(0, tm, drain, 0)


def dispatch(starts, ends, dest, packed, n_rows):
    n, half = packed.shape
    tm = _pick(n, 256)
    return pl.pallas_call(
        _dispatch_kernel,
        grid_spec=pltpu.PrefetchScalarGridSpec(
            num_scalar_prefetch=2, grid=(n // tm,),
            in_specs=[pl.BlockSpec((TOP_K, tm), lambda i, s, e: (0, i), memory_space=pltpu.SMEM),
                      pl.BlockSpec((tm, half), lambda i, s, e: (i, 0))],
            out_specs=pl.BlockSpec(memory_space=pl.ANY),
            scratch_shapes=[pltpu.VMEM((MOE_TM, half), U32), pltpu.SemaphoreType.DMA(())]),
        out_shape=jax.ShapeDtypeStruct((n_rows, half), U32),
        compiler_params=_cparams(1),
        name="dispatch",
    )(starts, ends, dest, packed)


def _swiglu_expert(x_ref, gu_bf, dn_bf, o_ref, pack_out):
    half = x_ref.shape[1]
    gu = None
    for c in range(0, half, MOE_CHUNK):
        lo, hi = _unpack_rows(x_ref[:, c:c + MOE_CHUNK])
        part = (jnp.dot(lo.astype(BF16), gu_bf[c:c + MOE_CHUNK, :], preferred_element_type=F32)
                + jnp.dot(hi.astype(BF16), gu_bf[half + c:half + c + MOE_CHUNK, :], preferred_element_type=F32))
        gu = part if gu is None else gu + part
    g = gu[:, :D_EXPERT]
    act = (g * jax.nn.sigmoid(g) * gu[:, D_EXPERT:]).astype(BF16)
    for c in range(0, half, MOE_CHUNK):
        y_lo = jnp.dot(act, dn_bf[:, c:c + MOE_CHUNK], preferred_element_type=F32)
        y_hi = jnp.dot(act, dn_bf[:, half + c:half + c + MOE_CHUNK], preferred_element_type=F32)
        if pack_out:
            o_ref[:, c:c + MOE_CHUNK] = _pack_pair(y_lo, y_hi)
        else:
            o_ref[:, c:c + MOE_CHUNK] = y_lo.astype(o_ref.dtype)
            o_ref[:, half + c:half + c + MOE_CHUNK] = y_hi.astype(o_ref.dtype)


def _grouped_kernel(te_ref, nt_ref, x_ref, gu_ref, dn_ref, o_ref, gu_bf, dn_bf):
    j = pl.program_id(0)

    @pl.when(j < nt_ref[0])
    def _():
        @pl.when(jnp.logical_or(j == 0, te_ref[j] != te_ref[jnp.maximum(j - 1, 0)]))
        def _():
            gu_bf[...] = gu_ref[0].astype(BF16)
            dn_bf[...] = dn_ref[0].astype(BF16)

        _swiglu_expert(x_ref, gu_bf, dn_bf, o_ref, pack_out=True)


def grouped_experts(tile_expert, n_tiles, xs, exp_gu, exp_down):
    rows, half = xs.shape
    d = 2 * half

    def tile(j, nt):
        return jnp.minimum(j, nt[0] - 1)

    return pl.pallas_call(
        _grouped_kernel,
        grid_spec=pltpu.PrefetchScalarGridSpec(
            num_scalar_prefetch=2, grid=(rows // MOE_TM,),
            in_specs=[pl.BlockSpec((MOE_TM, half), lambda j, te, nt: (tile(j, nt), 0)),
                      pl.BlockSpec((1, d, 2 * D_EXPERT), lambda j, te, nt: (te[tile(j, nt)], 0, 0)),
                      pl.BlockSpec((1, D_EXPERT, d), lambda j, te, nt: (te[tile(j, nt)], 0, 0))],
            out_specs=pl.BlockSpec((MOE_TM, half), lambda j, te, nt: (tile(j, nt), 0)),
            scratch_shapes=[pltpu.VMEM((d, 2 * D_EXPERT), BF16), pltpu.VMEM((D_EXPERT, d), BF16)]),
        out_shape=jax.ShapeDtypeStruct((rows, half), U32),
        compiler_params=_cparams(1),
        name="grouped_experts",
    )(tile_expert, n_tiles, xs, exp_gu, exp_down)


def _shared_kernel(x_ref, gu_ref, dn_ref, o_ref, gu_bf, dn_bf):
    @pl.when(pl.program_id(0) == 0)
    def _():
        gu_bf[...] = gu_ref[...].astype(BF16)
        dn_bf[...] = dn_ref[...].astype(BF16)

    _swiglu_expert(x_ref, gu_bf, dn_bf, o_ref, pack_out=False)


def shared_expert(packed, w_gu, w_dn):
    n, half = packed.shape
    tm = _pick(n, 512)

    def whole(shape):
        return pl.BlockSpec(shape, lambda i: (0,) * len(shape))

    return pl.pallas_call(
        _shared_kernel,
        grid=(n // tm,),
        in_specs=[pl.BlockSpec((tm, half), lambda i: (i, 0)), whole(w_gu.shape), whole(w_dn.shape)],
        out_specs=pl.BlockSpec((tm, 2 * half), lambda i: (i, 0)),
        out_shape=jax.ShapeDtypeStruct((n, 2 * half), BF16),
        scratch_shapes=[pltpu.VMEM(w_gu.shape, BF16), pltpu.VMEM(w_dn.shape, BF16)],
        compiler_params=_cparams(1),
        name="shared_expert",
    )(packed, w_gu, w_dn)


def _gather_combine(dest_ref, next_dest_ref, ys_ref, w_ref, sh_ref, buf_ref, sems):
    tm, d = sh_ref.shape
    half = d // 2
    step = pl.program_id(0)
    slot = step % 2

    def row_copy(s, k, t, src):
        return pltpu.make_async_copy(ys_ref.at[pl.ds(src, 1)], buf_ref.at[s, k, pl.ds(t, 1)], sems.at[s])

    def request(idx_ref, s):
        def body(t, c):
            for k in range(TOP_K):
                row_copy(s, k, t, idx_ref[k, t]).start()
            return c
        lax.fori_loop(0, tm, body, 0)

    @pl.when(step == 0)
    def _():
        request(dest_ref, 0)

    @pl.when(step + 1 < pl.num_programs(0))
    def _():
        request(next_dest_ref, 1 - slot)

    def drain(t, c):
        for k in range(TOP_K):
            row_copy(slot, 0, 0, 0).wait()
        return c

    lax.fori_loop(0, tm, drain, 0)
    w = w_ref[...]
    y_lo = sh_ref[:, :half].astype(F32)
    y_hi = sh_ref[:, half:].astype(F32)
    for k in range(TOP_K):
        lo, hi = _unpack_rows(buf_ref[slot, k])
        y_lo = y_lo + w[:, k:k + 1] * lo
        y_hi = y_hi + w[:, k:k + 1] * hi
    return jnp.concatenate([y_lo, y_hi], axis=1)


class MoeOut(NamedTuple):
    dest: jax.Array
    ys: jax.Array
    wmat: jax.Array
    shared: jax.Array


def moe_experts(packed, ids, ranks, wmat, counts, moe):
    _, _, exp_gu, exp_down, shared_gu, shared_down = moe
    n = packed.shape[0]
    cnt = counts[:, 0].astype(I32)
    padded = (cnt + MOE_TM - 1) // MOE_TM * MOE_TM
    ends = jnp.cumsum(padded)
    starts = ends - padded
    n_max = n * TOP_K // MOE_TM + N_EXPERTS
    n_tiles = (ends[-1] // MOE_TM).reshape(1)
    tile_row0 = jnp.arange(n_max, dtype=I32) * MOE_TM
    tile_expert = jnp.minimum(jnp.sum(ends[None, :] <= tile_row0[:, None], axis=1), N_EXPERTS - 1).astype(I32)
    dest = dest_rows(starts, ids, ranks)
    xs = dispatch(starts, ends, dest, packed, n_max * MOE_TM)
    ys = grouped_experts(tile_expert, n_tiles, xs, exp_gu, exp_down)
    shared = shared_expert(packed, shared_gu, shared_down)
    return MoeOut(dest, ys, wmat, shared)


def _rope_tables(seq, n_rows):
    pos = jnp.arange(seq)
    row = (pos // GRID_W).astype(F32)
    col = (pos % GRID_W).astype(F32)
    n_freq = HEAD_DIM // 4
    inv_freq = ROPE_THETA ** (-jnp.arange(n_freq, dtype=F32) / n_freq)
    ang = jnp.concatenate([row[:, None] * inv_freq, col[:, None] * inv_freq], axis=-1)
    cos, sin = jnp.cos(ang), jnp.sin(ang)
    cos_full = jnp.concatenate([cos, cos], axis=-1)
    sin_full = jnp.concatenate([-sin, sin], axis=-1)
    return cos_full, sin_full


def _trunk(x, c, ctx, c_ctx, layers, final_norm):
    bsz, seq, d = x.shape
    n_ctx = ctx.shape[1]
    dims = (bsz, seq, n_ctx)
    n_lat = bsz * seq
    n_all = n_lat + bsz * n_ctx
    depth = len(layers)

    seg = _Segments(seq, bsz, bsz * n_ctx)

    h = jnp.concatenate([x.reshape(n_lat, d), ctx.reshape(bsz * n_ctx, d)], axis=0)
    mod_rows = -(-(bsz + 1) // 8) * 8
    cond = jnp.zeros((mod_rows, d), F32).at[:bsz].set(c).at[bsz].set(c_ctx)

    cos1, sin1 = _rope_tables(seq, n_all)
    cos_t = jnp.concatenate([jnp.tile(cos1, (bsz, 1)), jnp.ones((bsz * n_ctx, HEAD_DIM), F32)], axis=0)
    sin_t = jnp.concatenate([jnp.tile(sin1, (bsz, 1)), jnp.zeros((bsz * n_ctx, HEAD_DIM), F32)], axis=0)
    ones_hd = jnp.ones((HEAD_DIM,), F32)

    tables = [ada_table(cond, *layer[1]) for layer in layers]
    a, = rowwise(h, layers[0][2], seg, shift=(tables[0], 0))
    for li, (kind, ada, norm1, mixer, norm2, moe) in enumerate(layers):
        need_ctx = li < depth - 1
        rows_out = n_all if need_ctx else n_lat
        mods = tables[li]
        resid = (h, mods, 2, seg)
        if kind == "mlstm":
            w_in, gate_b, head_norm, w_o = mixer
            n_main = 2 * M_HEADS * M_QK_DIM + M_HEADS * M_V_DIM + D_MODEL
            proj = matmul(a, w_in[:, :n_main].astype(BF16), scale_tiles=M_HEADS * M_QK_DIM // 512,
                          scale=M_QK_DIM ** -0.5)
            n_gate = 4 * M_HEADS
            w_gate = jnp.zeros((d, LANES), BF16).at[:, :n_gate].set(w_in[:, n_main:].astype(BF16))
            gate_b_pad = jnp.zeros((1, LANES), F32).at[0, :n_gate].set(gate_b.astype(F32))
            gates_t = mlstm_gates(a, w_gate, gate_b_pad)
            hdirs = mlstm_scan(proj, gates_t, dims)
            mixed = mlstm_finish(hdirs, proj, head_norm)
            h = matmul(mixed, w_o.astype(BF16), resid=resid, rows=rows_out)
        else:
            if kind == "global":
                w_qkv, q_norm, k_norm, w_o = mixer
                qkv = matmul(a, w_qkv.astype(BF16), qkv=(cos_t, sin_t, q_norm, k_norm, True))
                sink = None
            else:
                w_qkv, sink, w_o = mixer
                qkv = matmul(a, w_qkv.astype(BF16), qkv=(cos_t, sin_t, ones_hd, ones_hd, False))
            mixed = attention(qkv, dims, lat_queries=True, window=kind == "swa", sink=sink, out_rows=rows_out)
            if need_ctx:
                mixed = attention(qkv, dims, lat_queries=False, window=False, sink=sink, out=mixed)
            h = matmul(mixed, w_o.astype(BF16), resid=resid, rows=rows_out)
        router_w, router_b, exp_gu, exp_down, shared_gu, shared_down = moe
        routed = rowwise(h, norm2, seg, shift=(mods, 3), rows=rows_out, route=(router_w.T, router_b))
        y = moe_experts(*routed, moe)
        if li + 1 < depth:
            h, a = rowwise(h, layers[li + 1][2], seg, resid=(y, mods, 5), shift=(tables[li + 1], 0), rows=rows_out)
        else:
            _h, out = rowwise(h, final_norm, seg, resid=(y, mods, 5), rows=rows_out, out_dtype=F32)
    return out.reshape(bsz, seq, d)


def kernel(x, c, ctx, c_ctx, l0_ada_down, l0_ada_up, l0_ada_b, l0_norm1, l0_attn_qkv, l0_q_norm, l0_k_norm, l0_attn_o, l0_norm2, l0_router_w, l0_router_b, l0_exp_gu, l0_exp_down, l0_shared_gu, l0_shared_down, l1_ada_down, l1_ada_up, l1_ada_b, l1_norm1, l1_mlstm_in, l1_mlstm_gate_b, l1_mlstm_head_norm, l1_mlstm_o, l1_norm2, l1_router_w, l1_router_b, l1_exp_gu, l1_exp_down, l1_shared_gu, l1_shared_down, l2_ada_down, l2_ada_up, l2_ada_b, l2_norm1, l2_swa_qkv, l2_swa_sink, l2_swa_o, l2_norm2, l2_router_w, l2_router_b, l2_exp_gu, l2_exp_down, l2_shared_gu, l2_shared_down, l3_ada_down, l3_ada_up, l3_ada_b, l3_norm1, l3_attn_qkv, l3_q_norm, l3_k_norm, l3_attn_o, l3_norm2, l3_router_w, l3_router_b, l3_exp_gu, l3_exp_down, l3_shared_gu, l3_shared_down, final_norm):
    layers = [
        ("global", (l0_ada_down, l0_ada_up, l0_ada_b), l0_norm1, (l0_attn_qkv, l0_q_norm, l0_k_norm, l0_attn_o), l0_norm2,
         (l0_router_w, l0_router_b, l0_exp_gu, l0_exp_down, l0_shared_gu, l0_shared_down)),
        ("mlstm", (l1_ada_down, l1_ada_up, l1_ada_b), l1_norm1, (l1_mlstm_in, l1_mlstm_gate_b, l1_mlstm_head_norm, l1_mlstm_o), l1_norm2,
         (l1_router_w, l1_router_b, l1_exp_gu, l1_exp_down, l1_shared_gu, l1_shared_down)),
        ("swa", (l2_ada_down, l2_ada_up, l2_ada_b), l2_norm1, (l2_swa_qkv, l2_swa_sink, l2_swa_o), l2_norm2,
         (l2_router_w, l2_router_b, l2_exp_gu, l2_exp_down, l2_shared_gu, l2_shared_down)),
        ("global", (l3_ada_down, l3_ada_up, l3_ada_b), l3_norm1, (l3_attn_qkv, l3_q_norm, l3_k_norm, l3_attn_o), l3_norm2,
         (l3_router_w, l3_router_b, l3_exp_gu, l3_exp_down, l3_shared_gu, l3_shared_down)),
    ]
    return _trunk(x, c, ctx, c_ctx, layers, final_norm)
```

```python
import functools
import math
from typing import NamedTuple

import jax
import jax.numpy as jnp
from jax import lax
from jax.experimental import pallas as pl
from jax.experimental.pallas import tpu as pltpu

F32 = jnp.float32
BF16 = jnp.bfloat16
U32 = jnp.uint32
I32 = jnp.int32

D_MODEL = 4096
GRID_W = 64
RMS_EPS = 1e-6
N_MOD = 6
N_HEADS = 32
N_KV_HEADS = 8
HEAD_DIM = D_MODEL // N_HEADS
KV_GROUP = N_HEADS // N_KV_HEADS
ROPE_THETA = 10000.0
WINDOW = 128
M_HEADS = 8
M_V_DIM = D_MODEL // M_HEADS
M_QK_DIM = M_V_DIM // 2
M_CHUNK = 128
GATE_CAP = 15.0
N_EXPERTS = 64
TOP_K = 8
N_GROUPS = 8
TOPK_GROUPS = 4
D_EXPERT = 192
ROUTED_SCALE = 2.5

MOE_TM = 512
MOE_CHUNK = 512
QKV_ROWS = 256
LANES = 128
VMEM_LIMIT = 56 * 1024 * 1024
NEG_INF = float("-inf")


def _cparams(n_axes):
    return pltpu.CompilerParams(dimension_semantics=("arbitrary",) * n_axes,
                                vmem_limit_bytes=VMEM_LIMIT)


def _pick(n, pref):
    t = pref
    while n % t:
        t //= 2
    return t


class _Segments(NamedTuple):
    seq: int
    bsz: int
    n_ctx_rows: int

    @property
    def tile_unit(self):
        return math.gcd(self.seq, self.n_ctx_rows)

    def of_row(self, row):
        return jnp.minimum(row // self.seq, self.bsz)


def _ada_kernel(cond_ref, down_ref, up_ref, b_ref, out_ref):
    c = cond_ref[...]
    t = jnp.dot(c * jax.nn.sigmoid(c), down_ref[...], precision=lax.Precision.HIGHEST,
                preferred_element_type=F32)
    out_ref[...] = jnp.dot(t, up_ref[...], precision=lax.Precision.HIGHEST,
                           preferred_element_type=F32) + b_ref[...]


def ada_table(cond_pad, down, up, bias):
    r, d = cond_pad.shape
    rank = down.shape[1]
    n = up.shape[1]
    tn = 2048
    out = pl.pallas_call(
        _ada_kernel,
        grid=(n // tn,),
        in_specs=[pl.BlockSpec((r, d), lambda j: (0, 0)),
                  pl.BlockSpec((d, rank), lambda j: (0, 0)),
                  pl.BlockSpec((rank, tn), lambda j: (0, j)),
                  pl.BlockSpec((1, tn), lambda j: (0, j))],
        out_specs=pl.BlockSpec((r, tn), lambda j: (0, j)),
        out_shape=jax.ShapeDtypeStruct((r, n), F32),
        compiler_params=_cparams(1),
        name="ada_table",
    )(cond_pad, down, up, bias.reshape(1, n))
    return out.reshape(r, N_MOD, d)


def _rowwise_kernel(*refs, gate_idx, shift_idx, route, moe_resid):
    it = iter(refs)
    h_ref = next(it)
    if moe_resid:
        y = _gather_combine(next(it), next(it), next(it), next(it), next(it), refs[-2], refs[-1])
    elif gate_idx is not None:
        y = next(it)[...].astype(F32)
    gmod_ref = next(it) if gate_idx is not None else None
    g_ref = next(it)
    smod_ref = next(it) if shift_idx is not None else None
    wt_ref, rb_ref = (next(it), next(it)) if route else (None, None)
    h = h_ref[...]
    if gate_idx is not None:
        h = h + gmod_ref[0, gate_idx:gate_idx + 1, :] * y
        next(it)[...] = h
    a = h * lax.rsqrt(jnp.mean(h * h, axis=-1, keepdims=True) + RMS_EPS) * g_ref[...]
    if shift_idx is not None:
        a = a * (1.0 + smod_ref[0, shift_idx + 1:shift_idx + 2, :]) + smod_ref[0, shift_idx:shift_idx + 1, :]
    if not route:
        a_ref = next(it)
        a_ref[...] = a.astype(a_ref.dtype)
        return
    packed_ref, ids_ref, rank_ref, w_ref, cnt_ref = (next(it) for _ in range(5))

    @pl.when(pl.program_id(0) == 0)
    def _():
        cnt_ref[...] = jnp.zeros_like(cnt_ref)

    packed_ref[...] = _pack_rows(a)
    ids, ranks, wmat, counts = _route(a, wt_ref[...], rb_ref[...], cnt_ref[:, 0:1])
    ids_ref[...] = ids
    rank_ref[...] = ranks
    w_ref[...] = wmat
    cnt_ref[...] = jnp.broadcast_to(counts, cnt_ref.shape)


def rowwise(h, gain, seg, *, resid=None, shift=None, out_dtype=BF16, rows=None, route=None):
    n, d = h.shape
    rows = n if rows is None else rows
    moe_resid = resid is not None and isinstance(resid[0], MoeOut)
    tm = _pick(seg.tile_unit, 128 if moe_resid else 256)
    row_spec = pl.BlockSpec((tm, d), lambda i: (i, 0))
    mod_spec = pl.BlockSpec((1, N_MOD, d), lambda i: (seg.of_row(i * tm), 0, 0))
    in_specs, args, scratch = [row_spec], [h], []
    if moe_resid:
        last = rows // tm - 1
        in_specs += [pl.BlockSpec((TOP_K, tm), lambda i: (0, i), memory_space=pltpu.SMEM),
                     pl.BlockSpec((TOP_K, tm), lambda i: (0, jnp.minimum(i + 1, last)), memory_space=pltpu.SMEM),
                     pl.BlockSpec(memory_space=pl.ANY),
                     pl.BlockSpec((tm, LANES), lambda i: (i, 0)),
                     row_spec, mod_spec]
        moe_out = resid[0]
        args += [moe_out.dest, moe_out.dest, moe_out.ys, moe_out.wmat, moe_out.shared, resid[1]]
        scratch = [pltpu.VMEM((2, TOP_K, tm, d // 2), U32), pltpu.SemaphoreType.DMA((2,))]
    elif resid is not None:
        in_specs += [row_spec, mod_spec]
        args += [resid[0], resid[1]]
    in_specs.append(pl.BlockSpec((1, d), lambda i: (0, 0)))
    args.append(gain.reshape(1, d))
    if shift is not None:
        in_specs.append(mod_spec)
        args.append(shift[0])
    if route is not None:
        in_specs += [pl.BlockSpec((N_EXPERTS, d), lambda i: (0, 0)),
                     pl.BlockSpec((N_EXPERTS, 1), lambda i: (0, 0))]
        args += [route[0], route[1].reshape(N_EXPERTS, 1)]
    out_specs, out_shape = [], []
    if resid is not None:
        out_specs.append(row_spec)
        out_shape.append(jax.ShapeDtypeStruct((rows, d), F32))
    if route is None:
        out_specs.append(row_spec)
        out_shape.append(jax.ShapeDtypeStruct((rows, d), out_dtype))
    else:
        out_specs += [pl.BlockSpec((tm, d // 2), lambda i: (i, 0)),
                      pl.BlockSpec((TOP_K, tm), lambda i: (0, i)),
                      pl.BlockSpec((TOP_K, tm), lambda i: (0, i)),
                      pl.BlockSpec((tm, LANES), lambda i: (i, 0)),
                      pl.BlockSpec((N_EXPERTS, LANES), lambda i: (0, 0))]
        out_shape += [jax.ShapeDtypeStruct((rows, d // 2), U32),
                      jax.ShapeDtypeStruct((TOP_K, rows), I32),
                      jax.ShapeDtypeStruct((TOP_K, rows), I32),
                      jax.ShapeDtypeStruct((rows, LANES), F32),
                      jax.ShapeDtypeStruct((N_EXPERTS, LANES), F32)]
    return pl.pallas_call(
        functools.partial(_rowwise_kernel, gate_idx=None if resid is None else resid[2],
                          shift_idx=None if shift is None else shift[1], route=route is not None,
                          moe_resid=moe_resid),
        grid=(rows // tm,),
        in_specs=in_specs, out_specs=out_specs, out_shape=out_shape, scratch_shapes=scratch,
        compiler_params=_cparams(1),
        name="rowwise",
    )(*args)


def _mm_plain_kernel(x_ref, w_ref, o_ref, *, scale_tiles, scale):
    acc = jnp.dot(x_ref[...], w_ref[...], preferred_element_type=F32)
    if scale_tiles:
        acc = acc * jnp.where(pl.program_id(1) < scale_tiles, scale, 1.0)
    o_ref[...] = acc.astype(o_ref.dtype)


def _mm_resid_kernel(x_ref, w_ref, h_ref, mod_ref, o_ref, *, gate_idx):
    acc = jnp.dot(x_ref[...], w_ref[...], preferred_element_type=F32)
    o_ref[...] = h_ref[...] + mod_ref[0, gate_idx:gate_idx + 1, :] * acc


def _mm_qkv_kernel(x_ref, w_ref, cos_ref, sin_ref, qn_ref, kn_ref, o_ref, *, nq_tiles, nk_tiles, qk_norm):
    j = pl.program_id(1)
    is_q = j < nq_tiles
    is_qk = j < nq_tiles + nk_tiles
    post = jnp.where(is_q, HEAD_DIM ** -0.5, 1.0)
    gain = jnp.where(is_q, qn_ref[...], kn_ref[...])
    w = w_ref[...]
    tm, tn = o_ref.shape
    for r in range(0, tm, QKV_ROWS):
        acc = jnp.dot(x_ref[r:r + QKV_ROWS, :], w, preferred_element_type=F32)
        cos = cos_ref[r:r + QKV_ROWS, :]
        sin = sin_ref[r:r + QKV_ROWS, :]
        for s in range(0, tn, HEAD_DIM):
            raw = acc[:, s:s + HEAD_DIM]
            xh = raw
            if qk_norm:
                xh = xh * lax.rsqrt(jnp.mean(xh * xh, axis=-1, keepdims=True) + RMS_EPS) * gain
            xh = (xh * cos + pltpu.roll(xh, HEAD_DIM // 2, axis=1) * sin) * post
            o_ref[r:r + QKV_ROWS, s:s + HEAD_DIM] = jnp.where(is_qk, xh, raw).astype(o_ref.dtype)


def matmul(x, w, *, tm=1024, tn=512, out_dtype=BF16, rows=None, scale_tiles=0, scale=1.0,
           resid=None, qkv=None):
    m, k = x.shape
    m = m if rows is None else rows
    n = w.shape[1]
    tm = _pick(m if resid is None else math.gcd(m, resid[3].tile_unit), tm)
    tn = _pick(n, tn)
    grid = (m // tm, n // tn)
    x_spec = pl.BlockSpec((tm, k), lambda i, j: (i, 0))
    w_spec = pl.BlockSpec((k, tn), lambda i, j: (0, j))
    o_spec = pl.BlockSpec((tm, tn), lambda i, j: (i, j))
    if resid is not None:
        h, mods, gate_idx, seg = resid
        kern = functools.partial(_mm_resid_kernel, gate_idx=gate_idx)
        in_specs = [x_spec, w_spec, o_spec,
                    pl.BlockSpec((1, N_MOD, tn), lambda i, j: (seg.of_row(i * tm), 0, j))]
        args = (x, w, h, mods)
        out_dtype = F32
    elif qkv is not None:
        cos, sin, qn, kn, qk_norm = qkv
        kern = functools.partial(_mm_qkv_kernel, nq_tiles=N_HEADS * HEAD_DIM // tn,
                                 nk_tiles=N_KV_HEADS * HEAD_DIM // tn, qk_norm=qk_norm)
        tab_spec = pl.BlockSpec((tm, HEAD_DIM), lambda i, j: (i, 0))
        vec_spec = pl.BlockSpec((1, HEAD_DIM), lambda i, j: (0, 0))
        in_specs = [x_spec, w_spec, tab_spec, tab_spec, vec_spec, vec_spec]
        args = (x, w, cos, sin, qn.reshape(1, HEAD_DIM), kn.reshape(1, HEAD_DIM))
    else:
        kern = functools.partial(_mm_plain_kernel, scale_tiles=scale_tiles, scale=scale)
        in_specs = [x_spec, w_spec]
        args = (x, w)
    return pl.pallas_call(
        kern, grid=grid, in_specs=in_specs, out_specs=o_spec,
        out_shape=jax.ShapeDtypeStruct((m, n), out_dtype),
        compiler_params=_cparams(2),
        name="matmul",
    )(*args)


def _attn_kernel(*refs, tq, lat_keys, window, has_sink):
    it = iter(refs)
    sink_ref = next(it) if has_sink else None
    q_ref = next(it)
    kc_ref, vc_ref = next(it), next(it)
    kl_ref, vl_ref = (next(it), next(it)) if lat_keys else (None, None)
    o_ref = refs[-1]
    kvh = pl.program_id(1)
    qi = pl.program_id(2)
    nt = (((1,), (1,)), ((), ()))
    kc = kc_ref[...]
    vc = vc_ref[...]
    if lat_keys and window:
        seq = kl_ref.shape[0]
        band = min(tq + 2 * WINDOW, seq)
        start = pl.multiple_of(jnp.clip(qi * tq - WINDOW, 0, seq - band), LANES)
        kl = kl_ref[pl.ds(start, band), :]
        vl = vl_ref[pl.ds(start, band), :]
        q_pos = qi * tq + lax.broadcasted_iota(jnp.int32, (tq, band), 0)
        k_pos = start + lax.broadcasted_iota(jnp.int32, (tq, band), 1)
        in_window = jnp.abs(q_pos - k_pos) <= WINDOW
    elif lat_keys:
        kl = kl_ref[...]
        vl = vl_ref[...]
    for g in range(KV_GROUP):
        cols = slice(g * HEAD_DIM, (g + 1) * HEAD_DIM)
        qg = q_ref[:, cols]
        s_c = lax.dot_general(qg, kc, nt, preferred_element_type=F32)
        m = jnp.max(s_c, axis=-1, keepdims=True)
        if lat_keys:
            s_l = lax.dot_general(qg, kl, nt, preferred_element_type=F32)
            if window:
                s_l = jnp.where(in_window, s_l, NEG_INF)
            m = jnp.maximum(m, jnp.max(s_l, axis=-1, keepdims=True))
        if has_sink:
            sink = sink_ref[kvh * KV_GROUP + g]
            m = jnp.maximum(m, sink)
        p_c = jnp.exp(s_c - m)
        den = jnp.sum(p_c, axis=-1, keepdims=True)
        acc = jnp.dot(p_c.astype(BF16), vc, preferred_element_type=F32)
        if lat_keys:
            p_l = jnp.exp(s_l - m)
            den = den + jnp.sum(p_l, axis=-1, keepdims=True)
            acc = acc + jnp.dot(p_l.astype(BF16), vl, preferred_element_type=F32)
        if has_sink:
            den = den + jnp.exp(sink - m)
        o_ref[:, cols] = (acc / den).astype(o_ref.dtype)


def attention(qkv, dims, *, lat_queries, window, sink, out=None, out_rows=None):
    bsz, seq, n_ctx = dims
    n_lat = bsz * seq
    g_cols = KV_GROUP * HEAD_DIM
    k_col0 = N_HEADS * HEAD_DIM // HEAD_DIM
    v_col0 = k_col0 + N_KV_HEADS
    has_sink = sink is not None
    if lat_queries:
        tq = _pick(seq, 512)
        q_tiles = seq // tq
        q_row0 = 0
    else:
        tq = n_ctx
        q_tiles = 1
        q_row0 = n_lat // tq
    aliases = {}
    if out is not None:
        out_rows = out.shape[0]
    ctx_blk0 = n_lat // n_ctx
    in_specs = []
    args = []
    if has_sink:
        in_specs.append(pl.BlockSpec(memory_space=pltpu.SMEM))
        args.append(sink.astype(F32))
    in_specs += [
        pl.BlockSpec((tq, g_cols), lambda b, h, i: (q_row0 + b * q_tiles + i, h)),
        pl.BlockSpec((n_ctx, HEAD_DIM), lambda b, h, i: (ctx_blk0 + b, k_col0 + h)),
        pl.BlockSpec((n_ctx, HEAD_DIM), lambda b, h, i: (ctx_blk0 + b, v_col0 + h)),
    ]
    args += [qkv, qkv, qkv]
    if lat_queries:
        in_specs += [
            pl.BlockSpec((seq, HEAD_DIM), lambda b, h, i: (b, k_col0 + h)),
            pl.BlockSpec((seq, HEAD_DIM), lambda b, h, i: (b, v_col0 + h)),
        ]
        args += [qkv, qkv]
    if out is not None:
        aliases = {len(args): 0}
        in_specs.append(pl.BlockSpec(memory_space=pl.ANY))
        args.append(out)
    return pl.pallas_call(
        functools.partial(_attn_kernel, tq=tq, lat_keys=lat_queries, window=window, has_sink=has_sink),
        grid=(bsz, N_KV_HEADS, q_tiles),
        in_specs=in_specs,
        out_specs=pl.BlockSpec((tq, g_cols), lambda b, h, i: (q_row0 + b * q_tiles + i, h)),
        out_shape=jax.ShapeDtypeStruct((out_rows, N_HEADS * HEAD_DIM), BF16),
        input_output_aliases=aliases,
        compiler_params=_cparams(3),
        name="attention",
    )(*args)


def _mlstm_gate_kernel(x_ref, w_ref, b_ref, o_ref):
    g = jnp.dot(x_ref[...], w_ref[...], preferred_element_type=F32) + b_ref[...]
    g = GATE_CAP * jnp.tanh(g / GATE_CAP)
    log_f = jnp.minimum(g, 0.0) - jnp.log(1.0 + jnp.exp(-jnp.abs(g)))
    col = lax.broadcasted_iota(jnp.int32, g.shape, 1)
    is_f = (col // M_HEADS) % 2 == 1
    o_ref[...] = jnp.transpose(jnp.where(is_f, log_f, g))


def mlstm_gates(a, w_gate_pad, gate_b_pad):
    n, k = a.shape
    tm = _pick(n, 1024)
    return pl.pallas_call(
        _mlstm_gate_kernel,
        grid=(n // tm,),
        in_specs=[pl.BlockSpec((tm, k), lambda i: (i, 0)),
                  pl.BlockSpec((k, LANES), lambda i: (0, 0)),
                  pl.BlockSpec((1, LANES), lambda i: (0, 0))],
        out_specs=pl.BlockSpec((LANES, tm), lambda i: (0, i)),
        out_shape=jax.ShapeDtypeStruct((LANES, n), F32),
        compiler_params=_cparams(1),
        name="mlstm_gates",
    )(a, w_gate_pad, gate_b_pad)


def _mlstm_kernel(*refs):
    ct_ref, n_ref, m_ref = refs[-3:]
    c = pl.program_id(1)
    L = M_CHUNK

    @pl.when(c == 0)
    def _():
        ct_ref[...] = jnp.zeros_like(ct_ref)
        n_ref[...] = jnp.zeros_like(n_ref)
        m_ref[...] = jnp.zeros_like(m_ref)

    nt = (((1,), (1,)), ((), ()))
    tn = (((0,), (0,)), ((), ()))
    for d in range(2):
        q_ref, k_ref, v_ref, gi_ref, gf_ref = refs[5 * d:5 * d + 5]
        o_ref = refs[10 + d]
        diff = lax.broadcasted_iota(jnp.int32, (L, L), 0) - lax.broadcasted_iota(jnp.int32, (L, L), 1)
        if d == 1:
            diff = -diff
        mask = diff >= 0
        mask_t = diff <= 0
        eye = diff == 0
        for h in range(M_HEADS):
            st = d * M_HEADS + h
            qc = q_ref[:, h * M_QK_DIM:(h + 1) * M_QK_DIM]
            kc = k_ref[:, h * M_QK_DIM:(h + 1) * M_QK_DIM]
            vc = v_ref[:, h * M_V_DIM:(h + 1) * M_V_DIM]
            i_row = gi_ref[h:h + 1, :]
            f_row = gf_ref[h:h + 1, :]
            m_prev = m_ref[st, :, 0:1]
            n_prev = n_ref[st]
            ct_prev = ct_ref[st]
            f_col = jnp.sum(jnp.where(eye, f_row, 0.0), axis=1, keepdims=True)
            i_col = jnp.sum(jnp.where(eye, i_row, 0.0), axis=1, keepdims=True)
            b_col = jnp.sum(jnp.where(mask, f_row, 0.0), axis=1, keepdims=True)
            b_row = jnp.sum(jnp.where(mask_t, f_col, 0.0), axis=0, keepdims=True)
            log_inter = b_col + m_prev
            log_intra = jnp.where(mask, b_col - b_row + i_row, NEG_INF)
            m_t = jnp.maximum(log_inter, jnp.max(log_intra, axis=1, keepdims=True))
            w_inter = jnp.exp(log_inter - m_t)
            s_qk = lax.dot_general(qc, kc, nt, preferred_element_type=F32) * jnp.exp(log_intra - m_t)
            num = (jnp.dot(s_qk.astype(BF16), vc, preferred_element_type=F32)
                   + w_inter * jnp.dot(qc, ct_prev.astype(BF16), preferred_element_type=F32))
            den = (jnp.sum(s_qk, axis=1, keepdims=True)
                   + w_inter * jnp.sum(qc.astype(F32) * n_prev, axis=1, keepdims=True))
            h_out = num / jnp.maximum(jnp.abs(den), jnp.exp(-m_t))
            o_ref[:, h * M_V_DIM:(h + 1) * M_V_DIM] = h_out.astype(o_ref.dtype)
            b_tot = jnp.sum(f_row, axis=1, keepdims=True)
            log_w = b_tot - b_col + i_col
            m_new = jnp.maximum(b_tot + m_prev, jnp.max(log_w, axis=0, keepdims=True))
            decay = jnp.exp(b_tot + m_prev - m_new)
            w_k = jnp.exp(log_w - m_new)
            vw = (vc.astype(F32) * w_k).astype(BF16)
            ct_ref[st] = decay * ct_prev + lax.dot_general(kc, vw, tn, preferred_element_type=F32)
            n_ref[st] = decay * n_prev + jnp.sum(kc.astype(F32) * w_k, axis=0, keepdims=True)
            m_ref[st] = jnp.broadcast_to(m_new, (1, LANES))


def mlstm_scan(proj, gates_t, dims):
    bsz, seq, n_ctx = dims
    nt_rows = proj.shape[0]
    L = M_CHUNK
    ncc, nlc = n_ctx // L, seq // L
    lat_blocks = bsz * nlc
    qk_w = M_HEADS * M_QK_DIM

    def row_blk(b, d, c):
        cc = c if d == 0 else ncc - 1 - c
        lc = c - ncc if d == 0 else nlc - 1 - (c - ncc)
        return jnp.where(c < ncc, lat_blocks + b * ncc + cc, b * nlc + lc)

    def dir_specs(d):
        return [
            pl.BlockSpec((L, qk_w), lambda b, c: (row_blk(b, d, c), 0)),
            pl.BlockSpec((L, qk_w), lambda b, c: (row_blk(b, d, c), 1)),
            pl.BlockSpec((L, D_MODEL), lambda b, c: (row_blk(b, d, c), 1)),
            pl.BlockSpec((M_HEADS, L), lambda b, c: (2 * d, row_blk(b, d, c))),
            pl.BlockSpec((M_HEADS, L), lambda b, c: (2 * d + 1, row_blk(b, d, c))),
        ]

    return pl.pallas_call(
        _mlstm_kernel,
        grid=(bsz, ncc + nlc),
        in_specs=dir_specs(0) + dir_specs(1),
        out_specs=[pl.BlockSpec((L, D_MODEL), lambda b, c: (row_blk(b, 0, c), 0)),
                   pl.BlockSpec((L, D_MODEL), lambda b, c: (row_blk(b, 1, c), 0))],
        out_shape=[jax.ShapeDtypeStruct((nt_rows, D_MODEL), BF16)] * 2,
        scratch_shapes=[pltpu.VMEM((2 * M_HEADS, M_QK_DIM, M_V_DIM), F32),
                        pltpu.VMEM((2 * M_HEADS, 1, M_QK_DIM), F32),
                        pltpu.VMEM((2 * M_HEADS, 1, LANES), F32)],
        compiler_params=_cparams(2),
        name="mlstm_scan",
    )(*([proj, proj, proj, gates_t, gates_t] * 2))


def _mlstm_finish_kernel(hf_ref, hb_ref, og_ref, g_ref, o_ref):
    for h in range(M_HEADS):
        sl = slice(h * M_V_DIM, (h + 1) * M_V_DIM)
        x = hf_ref[:, sl].astype(F32) + hb_ref[:, sl].astype(F32)
        y = x * lax.rsqrt(jnp.mean(x * x, axis=-1, keepdims=True) + RMS_EPS) * g_ref[:, sl]
        o_ref[:, sl] = (y * jax.nn.sigmoid(og_ref[:, sl].astype(F32))).astype(o_ref.dtype)


def mlstm_finish(h_fwd, h_bwd, proj, head_norm):
    n, d = h_fwd.shape
    tm = _pick(n, 256)
    row_spec = pl.BlockSpec((tm, d), lambda i: (i, 0))
    return pl.pallas_call(
        _mlstm_finish_kernel,
        grid=(n // tm,),
        in_specs=[row_spec, row_spec,
                  pl.BlockSpec((tm, d), lambda i: (i, 2)),
                  pl.BlockSpec((1, d), lambda i: (0, 0))],
        out_specs=row_spec,
        out_shape=jax.ShapeDtypeStruct((n, d), BF16),
        compiler_params=_cparams(1),
        name="mlstm_finish",
    )(h_fwd, h_bwd, proj, head_norm.reshape(1, d))


def _pack_rows(x):
    half = x.shape[1] // 2
    return _pack_pair(x[:, :half], x[:, half:])


def _pack_pair(lo, hi):
    lo = lax.bitcast_convert_type(lo.astype(BF16).astype(F32), U32)
    hi = lax.bitcast_convert_type(hi.astype(BF16).astype(F32), U32)
    return (hi & jnp.uint32(0xFFFF0000)) | (lo >> 16)


def _unpack_rows(p):
    lo = lax.bitcast_convert_type(p << 16, F32)
    hi = lax.bitcast_convert_type(p & jnp.uint32(0xFFFF0000), F32)
    return lo, hi


def _route(a, w_t, bias, counts):
    tm = a.shape[0]
    per = N_EXPERTS // N_GROUPS
    nt = (((1,), (1,)), ((), ()))
    w_hi = w_t.astype(BF16)
    w_lo = (w_t - w_hi.astype(F32)).astype(BF16)
    a_hi = a.astype(BF16)
    a_lo = (a - a_hi.astype(F32)).astype(BF16)
    logits = (lax.dot_general(w_hi, a_hi, nt, preferred_element_type=F32)
              + lax.dot_general(w_lo, a_hi, nt, preferred_element_type=F32)
              + lax.dot_general(w_hi, a_lo, nt, preferred_element_type=F32))
    scores = jax.nn.sigmoid(logits).reshape(N_GROUPS, per, tm)
    biased = scores + bias.reshape(N_GROUPS, per, 1)
    e_iota = lax.broadcasted_iota(jnp.int32, (N_GROUPS, per, tm), 1).astype(F32)
    g_iota = lax.broadcasted_iota(jnp.int32, (N_GROUPS, 1, tm), 0).astype(F32)
    lin_iota = lax.broadcasted_iota(jnp.int32, (N_GROUPS, per, tm), 0).astype(F32) * per + e_iota
    m1 = jnp.max(biased, axis=1, keepdims=True)
    i1 = jnp.min(jnp.where(biased == m1, e_iota, float(per)), axis=1, keepdims=True)
    m2 = jnp.max(jnp.where(e_iota == i1, NEG_INF, biased), axis=1, keepdims=True)
    gscore = m1 + m2
    gsel = jnp.zeros(gscore.shape, F32)
    for _ in range(TOPK_GROUPS):
        cur = jnp.where(gsel > 0.0, NEG_INF, gscore)
        gm = jnp.max(cur, axis=0, keepdims=True)
        gi = jnp.min(jnp.where(cur == gm, g_iota, float(N_GROUPS)), axis=0, keepdims=True)
        gsel = jnp.where(g_iota == gi, 1.0, gsel)
    cand = jnp.where(gsel > 0.0, biased, NEG_INF)
    sel = jnp.zeros(cand.shape, F32)
    picks = []
    for _ in range(TOP_K):
        cur = jnp.where(sel > 0.0, NEG_INF, cand)
        em = jnp.max(jnp.max(cur, axis=1, keepdims=True), axis=0, keepdims=True)
        hit = jnp.where(cur == em, lin_iota, float(N_EXPERTS))
        ei = jnp.min(jnp.min(hit, axis=1, keepdims=True), axis=0, keepdims=True)
        sel = jnp.where(lin_iota == ei, 1.0, sel)
        picks.append(ei)
    sel2 = sel.reshape(N_EXPERTS, tm)
    before = lax.broadcasted_iota(jnp.int32, (tm, tm), 0) < lax.broadcasted_iota(jnp.int32, (tm, tm), 1)
    rank = jnp.dot(sel2.astype(BF16), jnp.where(before, 1.0, 0.0).astype(BF16),
                   preferred_element_type=F32) + counts
    new_counts = counts + jnp.sum(sel2, axis=1, keepdims=True)
    rank3 = rank.reshape(N_GROUPS, per, tm)
    wsum = jnp.sum(jnp.sum(jnp.where(sel > 0.0, scores, 0.0), axis=1, keepdims=True), axis=0, keepdims=True)

    def pick(ei, table):
        v = jnp.where(lin_iota == ei, table, 0.0)
        return jnp.sum(jnp.sum(v, axis=1, keepdims=True), axis=0, keepdims=True).reshape(1, tm)

    ids = jnp.concatenate([ei.reshape(1, tm) for ei in picks], axis=0).astype(I32)
    ranks = jnp.concatenate([pick(ei, rank3) for ei in picks], axis=0).astype(I32)
    w = jnp.concatenate([pick(ei, scores) for ei in picks], axis=0) / wsum.reshape(1, tm) * ROUTED_SCALE
    wmat = jnp.transpose(jnp.concatenate([w, jnp.zeros((LANES - TOP_K, tm), F32)], axis=0))
    return ids, ranks, wmat, new_counts


def _dest_kernel(starts_ref, ids_ref, rank_ref, dest_ref):
    ids = ids_ref[...]
    acc = rank_ref[...]
    for e in range(N_EXPERTS):
        acc = acc + jnp.where(ids == e, starts_ref[e], 0)
    dest_ref[...] = acc


def dest_rows(starts, ids, ranks):
    return pl.pallas_call(
        _dest_kernel,
        in_specs=[pl.BlockSpec(memory_space=pltpu.SMEM), pl.BlockSpec(memory_space=pltpu.VMEM),
                  pl.BlockSpec(memory_space=pltpu.VMEM)],
        out_specs=pl.BlockSpec(memory_space=pltpu.VMEM),
        out_shape=jax.ShapeDtypeStruct(ids.shape, I32),
        compiler_params=pltpu.CompilerParams(vmem_limit_bytes=VMEM_LIMIT),
        name="dest_rows",
    )(starts, ids, ranks)


def _dispatch_kernel(starts_ref, ends_ref, dest_ref, f_ref, gu_ref, dn_ref, xs_ref, sh_ref,
                     zero_ref, gu_bf, dn_bf, sem):
    tm = f_ref.shape[0]

    def zero_copy(e):
        row0 = pl.multiple_of(ends_ref[e] - MOE_TM, MOE_TM)
        return pltpu.make_async_copy(zero_ref, xs_ref.at[pl.ds(row0, MOE_TM)], sem)

    @pl.when(pl.program_id(0) == 0)
    def _():
        gu_bf[...] = gu_ref[...].astype(BF16)
        dn_bf[...] = dn_ref[...].astype(BF16)
        zero_ref[...] = jnp.zeros_like(zero_ref)
        for e in range(N_EXPERTS):
            @pl.when(ends_ref[e] > starts_ref[e])
            def _():
                zero_copy(e).start()
        for e in range(N_EXPERTS):
            @pl.when(ends_ref[e] > starts_ref[e])
            def _():
                zero_copy(e).wait()

    def row_copy(t, d):
        return pltpu.make_async_copy(f_ref.at[pl.ds(t, 1)], xs_ref.at[pl.ds(d, 1)], sem)

    def issue(t, c):
        for k in range(TOP_K):
            row_copy(t, dest_ref[k, t]).start()
        return c

    def drain(t, c):
        for k in range(TOP_K):
            row_copy(0, 0).wait()
        return c

    lax.fori_loop(0, tm, issue, 0)
    _swiglu_expert(f_ref, gu_bf, dn_bf, sh_ref, pack_out=False)
    lax.fori_loop(0, tm, drain, 0)


def dispatch(starts, ends, dest, packed, n_rows, shared_gu, shared_down):
    n, half = packed.shape
    tm = _pick(n, 256)

    def whole(shape):
        return pl.BlockSpec(shape, lambda i, s, e: (0,) * len(shape))

    return pl.pallas_call(
        _dispatch_kernel,
        grid_spec=pltpu.PrefetchScalarGridSpec(
            num_scalar_prefetch=2, grid=(n // tm,),
            in_specs=[pl.BlockSpec((TOP_K, tm), lambda i, s, e: (0, i), memory_space=pltpu.SMEM),
                      pl.BlockSpec((tm, half), lambda i, s, e: (i, 0)),
                      whole(shared_gu.shape), whole(shared_down.shape)],
            out_specs=[pl.BlockSpec(memory_space=pl.ANY),
                       pl.BlockSpec((tm, 2 * half), lambda i, s, e: (i, 0))],
            scratch_shapes=[pltpu.VMEM((MOE_TM, half), U32), pltpu.VMEM(shared_gu.shape, BF16),
                            pltpu.VMEM(shared_down.shape, BF16), pltpu.SemaphoreType.DMA(())]),
        out_shape=[jax.ShapeDtypeStruct((n_rows, half), U32), jax.ShapeDtypeStruct((n, 2 * half), BF16)],
        compiler_params=_cparams(1),
        name="dispatch",
    )(starts, ends, dest, packed, shared_gu, shared_down)


def _swiglu_expert(x_ref, gu_bf, dn_bf, o_ref, pack_out):
    half = x_ref.shape[1]
    gu = None
    for c in range(0, half, MOE_CHUNK):
        lo, hi = _unpack_rows(x_ref[:, c:c + MOE_CHUNK])
        part = (jnp.dot(lo.astype(BF16), gu_bf[c:c + MOE_CHUNK, :], preferred_element_type=F32)
                + jnp.dot(hi.astype(BF16), gu_bf[half + c:half + c + MOE_CHUNK, :], preferred_element_type=F32))
        gu = part if gu is None else gu + part
    g = gu[:, :D_EXPERT]
    act = (g * jax.nn.sigmoid(g) * gu[:, D_EXPERT:]).astype(BF16)
    for c in range(0, half, MOE_CHUNK):
        y_lo = jnp.dot(act, dn_bf[:, c:c + MOE_CHUNK], preferred_element_type=F32)
        y_hi = jnp.dot(act, dn_bf[:, half + c:half + c + MOE_CHUNK], preferred_element_type=F32)
        if pack_out:
            o_ref[:, c:c + MOE_CHUNK] = _pack_pair(y_lo, y_hi)
        else:
            o_ref[:, c:c + MOE_CHUNK] = y_lo.astype(o_ref.dtype)
            o_ref[:, half + c:half + c + MOE_CHUNK] = y_hi.astype(o_ref.dtype)


def _grouped_kernel(te_ref, nt_ref, x_ref, gu_ref, dn_ref, o_ref, gu_bf, dn_bf):
    j = pl.program_id(0)

    @pl.when(j < nt_ref[0])
    def _():
        @pl.when(jnp.logical_or(j == 0, te_ref[j] != te_ref[jnp.maximum(j - 1, 0)]))
        def _():
            gu_bf[...] = gu_ref[0].astype(BF16)
            dn_bf[...] = dn_ref[0].astype(BF16)

        _swiglu_expert(x_ref, gu_bf, dn_bf, o_ref, pack_out=True)


def grouped_experts(tile_expert, n_tiles, xs, exp_gu, exp_down):
    rows, half = xs.shape
    d = 2 * half

    def tile(j, nt):
        return jnp.minimum(j, nt[0] - 1)

    return pl.pallas_call(
        _grouped_kernel,
        grid_spec=pltpu.PrefetchScalarGridSpec(
            num_scalar_prefetch=2, grid=(rows // MOE_TM,),
            in_specs=[pl.BlockSpec((MOE_TM, half), lambda j, te, nt: (tile(j, nt), 0)),
                      pl.BlockSpec((1, d, 2 * D_EXPERT), lambda j, te, nt: (te[tile(j, nt)], 0, 0)),
                      pl.BlockSpec((1, D_EXPERT, d), lambda j, te, nt: (te[tile(j, nt)], 0, 0))],
            out_specs=pl.BlockSpec((MOE_TM, half), lambda j, te, nt: (tile(j, nt), 0)),
            scratch_shapes=[pltpu.VMEM((d, 2 * D_EXPERT), BF16), pltpu.VMEM((D_EXPERT, d), BF16)]),
        out_shape=jax.ShapeDtypeStruct((rows, half), U32),
        compiler_params=_cparams(1),
        name="grouped_experts",
    )(tile_expert, n_tiles, xs, exp_gu, exp_down)


def _gather_combine(dest_ref, next_dest_ref, ys_ref, w_ref, sh_ref, buf_ref, sems):
    tm, d = sh_ref.shape
    half = d // 2
    step = pl.program_id(0)
    slot = step % 2

    def row_copy(s, k, t, src):
        return pltpu.make_async_copy(ys_ref.at[pl.ds(src, 1)], buf_ref.at[s, k, pl.ds(t, 1)], sems.at[s])

    def request(idx_ref, s):
        def body(t, c):
            for k in range(TOP_K):
                row_copy(s, k, t, idx_ref[k, t]).start()
            return c
        lax.fori_loop(0, tm, body, 0)

    @pl.when(step == 0)
    def _():
        request(dest_ref, 0)

    @pl.when(step + 1 < pl.num_programs(0))
    def _():
        request(next_dest_ref, 1 - slot)

    def drain(t, c):
        for k in range(TOP_K):
            row_copy(slot, 0, 0, 0).wait()
        return c

    lax.fori_loop(0, tm, drain, 0)
    w = w_ref[...]
    y_lo = sh_ref[:, :half].astype(F32)
    y_hi = sh_ref[:, half:].astype(F32)
    for k in range(TOP_K):
        lo, hi = _unpack_rows(buf_ref[slot, k])
        y_lo = y_lo + w[:, k:k + 1] * lo
        y_hi = y_hi + w[:, k:k + 1] * hi
    return jnp.concatenate([y_lo, y_hi], axis=1)


class MoeOut(NamedTuple):
    dest: jax.Array
    ys: jax.Array
    wmat: jax.Array
    shared: jax.Array


def moe_experts(packed, ids, ranks, wmat, counts, moe):
    _, _, exp_gu, exp_down, shared_gu, shared_down = moe
    n = packed.shape[0]
    cnt = counts[:, 0].astype(I32)
    padded = (cnt + MOE_TM - 1) // MOE_TM * MOE_TM
    ends = jnp.cumsum(padded)
    starts = ends - padded
    n_max = n * TOP_K // MOE_TM + N_EXPERTS
    n_tiles = (ends[-1] // MOE_TM).reshape(1)
    tile_row0 = jnp.arange(n_max, dtype=I32) * MOE_TM
    tile_expert = jnp.minimum(jnp.sum(ends[None, :] <= tile_row0[:, None], axis=1), N_EXPERTS - 1).astype(I32)
    dest = dest_rows(starts, ids, ranks)
    xs, shared = dispatch(starts, ends, dest, packed, n_max * MOE_TM, shared_gu, shared_down)
    ys = grouped_experts(tile_expert, n_tiles, xs, exp_gu, exp_down)
    return MoeOut(dest, ys, wmat, shared)


def _rope_tables(seq, n_rows):
    pos = jnp.arange(seq)
    row = (pos // GRID_W).astype(F32)
    col = (pos % GRID_W).astype(F32)
    n_freq = HEAD_DIM // 4
    inv_freq = ROPE_THETA ** (-jnp.arange(n_freq, dtype=F32) / n_freq)
    ang = jnp.concatenate([row[:, None] * inv_freq, col[:, None] * inv_freq], axis=-1)
    cos, sin = jnp.cos(ang), jnp.sin(ang)
    cos_full = jnp.concatenate([cos, cos], axis=-1)
    sin_full = jnp.concatenate([-sin, sin], axis=-1)
    return cos_full, sin_full


def _trunk(x, c, ctx, c_ctx, layers, final_norm):
    bsz, seq, d = x.shape
    n_ctx = ctx.shape[1]
    dims = (bsz, seq, n_ctx)
    n_lat = bsz * seq
    n_all = n_lat + bsz * n_ctx
    depth = len(layers)

    seg = _Segments(seq, bsz, bsz * n_ctx)

    h = jnp.concatenate([x.reshape(n_lat, d), ctx.reshape(bsz * n_ctx, d)], axis=0)
    mod_rows = -(-(bsz + 1) // 8) * 8
    cond = jnp.zeros((mod_rows, d), F32).at[:bsz].set(c).at[bsz].set(c_ctx)

    cos1, sin1 = _rope_tables(seq, n_all)
    cos_t = jnp.concatenate([jnp.tile(cos1, (bsz, 1)), jnp.ones((bsz * n_ctx, HEAD_DIM), F32)], axis=0)
    sin_t = jnp.concatenate([jnp.tile(sin1, (bsz, 1)), jnp.zeros((bsz * n_ctx, HEAD_DIM), F32)], axis=0)
    ones_hd = jnp.ones((HEAD_DIM,), F32)

    tables = [ada_table(cond, *layer[1]) for layer in layers]
    a, = rowwise(h, layers[0][2], seg, shift=(tables[0], 0))
    for li, (kind, ada, norm1, mixer, norm2, moe) in enumerate(layers):
        need_ctx = li < depth - 1
        rows_out = n_all if need_ctx else n_lat
        mods = tables[li]
        resid = (h, mods, 2, seg)
        if kind == "mlstm":
            w_in, gate_b, head_norm, w_o = mixer
            n_main = 2 * M_HEADS * M_QK_DIM + M_HEADS * M_V_DIM + D_MODEL
            proj = matmul(a, w_in[:, :n_main].astype(BF16), scale_tiles=M_HEADS * M_QK_DIM // 512,
                          scale=M_QK_DIM ** -0.5)
            n_gate = 4 * M_HEADS
            w_gate = jnp.zeros((d, LANES), BF16).at[:, :n_gate].set(w_in[:, n_main:].astype(BF16))
            gate_b_pad = jnp.zeros((1, LANES), F32).at[0, :n_gate].set(gate_b.astype(F32))
            gates_t = mlstm_gates(a, w_gate, gate_b_pad)
            h_fwd, h_bwd = mlstm_scan(proj, gates_t, dims)
            mixed = mlstm_finish(h_fwd, h_bwd, proj, head_norm)
            h = matmul(mixed, w_o.astype(BF16), resid=resid, rows=rows_out)
        else:
            if kind == "global":
                w_qkv, q_norm, k_norm, w_o = mixer
                qkv = matmul(a, w_qkv.astype(BF16), qkv=(cos_t, sin_t, q_norm, k_norm, True))
                sink = None
            else:
                w_qkv, sink, w_o = mixer
                qkv = matmul(a, w_qkv.astype(BF16), qkv=(cos_t, sin_t, ones_hd, ones_hd, False))
            mixed = attention(qkv, dims, lat_queries=True, window=kind == "swa", sink=sink, out_rows=rows_out)
            if need_ctx:
                mixed = attention(qkv, dims, lat_queries=False, window=False, sink=sink, out=mixed)
            h = matmul(mixed, w_o.astype(BF16), resid=resid, rows=rows_out)
        router_w, router_b, exp_gu, exp_down, shared_gu, shared_down = moe
        routed = rowwise(h, norm2, seg, shift=(mods, 3), rows=rows_out, route=(router_w.T, router_b))
        y = moe_experts(*routed, moe)
        if li + 1 < depth:
            h, a = rowwise(h, layers[li + 1][2], seg, resid=(y, mods, 5), shift=(tables[li + 1], 0), rows=rows_out)
        else:
            _h, out = rowwise(h, final_norm, seg, resid=(y, mods, 5), rows=rows_out, out_dtype=F32)
    return out.reshape(bsz, seq, d)


def kernel(x, c, ctx, c_ctx, l0_ada_down, l0_ada_up, l0_ada_b, l0_norm1, l0_attn_qkv, l0_q_norm, l0_k_norm, l0_attn_o, l0_norm2, l0_router_w, l0_router_b, l0_exp_gu, l0_exp_down, l0_shared_gu, l0_shared_down, l1_ada_down, l1_ada_up, l1_ada_b, l1_norm1, l1_mlstm_in, l1_mlstm_gate_b, l1_mlstm_head_norm, l1_mlstm_o, l1_norm2, l1_router_w, l1_router_b, l1_exp_gu, l1_exp_down, l1_shared_gu, l1_shared_down, l2_ada_down, l2_ada_up, l2_ada_b, l2_norm1, l2_swa_qkv, l2_swa_sink, l2_swa_o, l2_norm2, l2_router_w, l2_router_b, l2_exp_gu, l2_exp_down, l2_shared_gu, l2_shared_down, l3_ada_down, l3_ada_up, l3_ada_b, l3_norm1, l3_attn_qkv, l3_q_norm, l3_k_norm, l3_attn_o, l3_norm2, l3_router_w, l3_router_b, l3_exp_gu, l3_exp_down, l3_shared_gu, l3_shared_down, final_norm):
    layers = [
        ("global", (l0_ada_down, l0_ada_up, l0_ada_b), l0_norm1, (l0_attn_qkv, l0_q_norm, l0_k_norm, l0_attn_o), l0_norm2,
         (l0_router_w, l0_router_b, l0_exp_gu, l0_exp_down, l0_shared_gu, l0_shared_down)),
        ("mlstm", (l1_ada_down, l1_ada_up, l1_ada_b), l1_norm1, (l1_mlstm_in, l1_mlstm_gate_b, l1_mlstm_head_norm, l1_mlstm_o), l1_norm2,
         (l1_router_w, l1_router_b, l1_exp_gu, l1_exp_down, l1_shared_gu, l1_shared_down)),
        ("swa", (l2_ada_down, l2_ada_up, l2_ada_b), l2_norm1, (l2_swa_qkv, l2_swa_sink, l2_swa_o), l2_norm2,
         (l2_router_w, l2_router_b, l2_exp_gu, l2_exp_down, l2_shared_gu, l2_shared_down)),
        ("global", (l3_ada_down, l3_ada_up, l3_ada_b), l3_norm1, (l3_attn_qkv, l3_q_norm, l3_k_norm, l3_attn_o), l3_norm2,
         (l3_router_w, l3_router_b, l3_exp_gu, l3_exp_down, l3_shared_gu, l3_shared_down)),
    ]
    return _trunk(x, c, ctx, c_ctx, layers, final_norm)
```

```python
import functools
import math
from typing import NamedTuple

import jax
import jax.numpy as jnp
from jax import lax
from jax.experimental import pallas as pl
from jax.experimental.pallas import tpu as pltpu

F32 = jnp.float32
BF16 = jnp.bfloat16
U32 = jnp.uint32
I32 = jnp.int32

D_MODEL = 4096
GRID_W = 64
RMS_EPS = 1e-6
N_MOD = 6
N_HEADS = 32
N_KV_HEADS = 8
HEAD_DIM = D_MODEL // N_HEADS
KV_GROUP = N_HEADS // N_KV_HEADS
ROPE_THETA = 10000.0
WINDOW = 128
M_HEADS = 8
M_V_DIM = D_MODEL // M_HEADS
M_QK_DIM = M_V_DIM // 2
M_CHUNK = 128
GATE_CAP = 15.0
N_EXPERTS = 64
TOP_K = 8
N_GROUPS = 8
TOPK_GROUPS = 4
D_EXPERT = 192
ROUTED_SCALE = 2.5

MOE_TM = 512
MOE_CHUNK = 512
QKV_ROWS = 256
LANES = 128
VMEM_LIMIT = 56 * 1024 * 1024
NEG_INF = float("-inf")


def _cparams(n_axes):
    return pltpu.CompilerParams(dimension_semantics=("arbitrary",) * n_axes,
                                vmem_limit_bytes=VMEM_LIMIT)


def _pick(n, pref):
    t = pref
    while n % t:
        t //= 2
    return t


class _Segments(NamedTuple):
    seq: int
    bsz: int
    n_ctx_rows: int

    @property
    def tile_unit(self):
        return math.gcd(self.seq, self.n_ctx_rows)

    def of_row(self, row):
        return jnp.minimum(row // self.seq, self.bsz)


def _ada_kernel(cond_ref, down_ref, up_ref, b_ref, out_ref):
    c = cond_ref[...]
    t = jnp.dot(c * jax.nn.sigmoid(c), down_ref[...], precision=lax.Precision.HIGHEST,
                preferred_element_type=F32)
    out_ref[...] = jnp.dot(t, up_ref[...], precision=lax.Precision.HIGHEST,
                           preferred_element_type=F32) + b_ref[...]


def ada_table(cond_pad, down, up, bias):
    r, d = cond_pad.shape
    rank = down.shape[1]
    n = up.shape[1]
    tn = 2048
    out = pl.pallas_call(
        _ada_kernel,
        grid=(n // tn,),
        in_specs=[pl.BlockSpec((r, d), lambda j: (0, 0)),
                  pl.BlockSpec((d, rank), lambda j: (0, 0)),
                  pl.BlockSpec((rank, tn), lambda j: (0, j)),
                  pl.BlockSpec((1, tn), lambda j: (0, j))],
        out_specs=pl.BlockSpec((r, tn), lambda j: (0, j)),
        out_shape=jax.ShapeDtypeStruct((r, n), F32),
        compiler_params=_cparams(1),
        name="ada_table",
    )(cond_pad, down, up, bias.reshape(1, n))
    return out.reshape(r, N_MOD, d)


def _rowwise_kernel(*refs, gate_idx, shift_idx, route, moe_resid):
    it = iter(refs)
    h_ref = next(it)
    if moe_resid:
        y = _gather_combine(next(it), next(it), next(it), next(it), next(it), refs[-2], refs[-1])
    elif gate_idx is not None:
        y = next(it)[...].astype(F32)
    gmod_ref = next(it) if gate_idx is not None else None
    g_ref = next(it)
    smod_ref = next(it) if shift_idx is not None else None
    wt_ref, rb_ref = (next(it), next(it)) if route else (None, None)
    h = h_ref[...]
    if gate_idx is not None:
        h = h + gmod_ref[0, gate_idx:gate_idx + 1, :] * y
        next(it)[...] = h
    a = h * lax.rsqrt(jnp.mean(h * h, axis=-1, keepdims=True) + RMS_EPS) * g_ref[...]
    if shift_idx is not None:
        a = a * (1.0 + smod_ref[0, shift_idx + 1:shift_idx + 2, :]) + smod_ref[0, shift_idx:shift_idx + 1, :]
    if not route:
        a_ref = next(it)
        a_ref[...] = a.astype(a_ref.dtype)
        return
    packed_ref, ids_ref, rank_ref, w_ref, cnt_ref = (next(it) for _ in range(5))

    @pl.when(pl.program_id(0) == 0)
    def _():
        cnt_ref[...] = jnp.zeros_like(cnt_ref)

    packed_ref[...] = _pack_rows(a)
    ids, ranks, wmat, counts = _route(a, wt_ref[...], rb_ref[...], cnt_ref[:, 0:1])
    ids_ref[...] = ids
    rank_ref[...] = ranks
    w_ref[...] = wmat
    cnt_ref[...] = jnp.broadcast_to(counts, cnt_ref.shape)


def rowwise(h, gain, seg, *, resid=None, shift=None, out_dtype=BF16, rows=None, route=None):
    n, d = h.shape
    rows = n if rows is None else rows
    moe_resid = resid is not None and isinstance(resid[0], MoeOut)
    tm = _pick(seg.tile_unit, 128 if moe_resid else 256)
    row_spec = pl.BlockSpec((tm, d), lambda i: (i, 0))
    mod_spec = pl.BlockSpec((1, N_MOD, d), lambda i: (seg.of_row(i * tm), 0, 0))
    in_specs, args, scratch = [row_spec], [h], []
    if moe_resid:
        last = rows // tm - 1
        in_specs += [pl.BlockSpec((TOP_K, tm), lambda i: (0, i), memory_space=pltpu.SMEM),
                     pl.BlockSpec((TOP_K, tm), lambda i: (0, jnp.minimum(i + 1, last)), memory_space=pltpu.SMEM),
                     pl.BlockSpec(memory_space=pl.ANY),
                     pl.BlockSpec((tm, LANES), lambda i: (i, 0)),
                     row_spec, mod_spec]
        moe_out = resid[0]
        args += [moe_out.dest, moe_out.dest, moe_out.ys, moe_out.wmat, moe_out.shared, resid[1]]
        scratch = [pltpu.VMEM((2, TOP_K, tm, d // 2), U32), pltpu.SemaphoreType.DMA((2,))]
    elif resid is not None:
        in_specs += [row_spec, mod_spec]
        args += [resid[0], resid[1]]
    in_specs.append(pl.BlockSpec((1, d), lambda i: (0, 0)))
    args.append(gain.reshape(1, d))
    if shift is not None:
        in_specs.append(mod_spec)
        args.append(shift[0])
    if route is not None:
        in_specs += [pl.BlockSpec((N_EXPERTS, d), lambda i: (0, 0)),
                     pl.BlockSpec((N_EXPERTS, 1), lambda i: (0, 0))]
        args += [route[0], route[1].reshape(N_EXPERTS, 1)]
    out_specs, out_shape = [], []
    if resid is not None:
        out_specs.append(row_spec)
        out_shape.append(jax.ShapeDtypeStruct((rows, d), F32))
    if route is None:
        out_specs.append(row_spec)
        out_shape.append(jax.ShapeDtypeStruct((rows, d), out_dtype))
    else:
        out_specs += [pl.BlockSpec((tm, d // 2), lambda i: (i, 0)),
                      pl.BlockSpec((TOP_K, tm), lambda i: (0, i)),
                      pl.BlockSpec((TOP_K, tm), lambda i: (0, i)),
                      pl.BlockSpec((tm, LANES), lambda i: (i, 0)),
                      pl.BlockSpec((N_EXPERTS, LANES), lambda i: (0, 0))]
        out_shape += [jax.ShapeDtypeStruct((rows, d // 2), U32),
                      jax.ShapeDtypeStruct((TOP_K, rows), I32),
                      jax.ShapeDtypeStruct((TOP_K, rows), I32),
                      jax.ShapeDtypeStruct((rows, LANES), F32),
                      jax.ShapeDtypeStruct((N_EXPERTS, LANES), F32)]
    return pl.pallas_call(
        functools.partial(_rowwise_kernel, gate_idx=None if resid is None else resid[2],
                          shift_idx=None if shift is None else shift[1], route=route is not None,
                          moe_resid=moe_resid),
        grid=(rows // tm,),
        in_specs=in_specs, out_specs=out_specs, out_shape=out_shape, scratch_shapes=scratch,
        compiler_params=_cparams(1),
        name="rowwise",
    )(*args)


def _mm_plain_kernel(x_ref, w_ref, o_ref, *, scale_tiles, scale):
    acc = jnp.dot(x_ref[...], w_ref[...], preferred_element_type=F32)
    if scale_tiles:
        acc = acc * jnp.where(pl.program_id(1) < scale_tiles, scale, 1.0)
    o_ref[...] = acc.astype(o_ref.dtype)


def _mm_resid_kernel(x_ref, w_ref, h_ref, mod_ref, o_ref, *, gate_idx):
    acc = jnp.dot(x_ref[...], w_ref[...], preferred_element_type=F32)
    o_ref[...] = h_ref[...] + mod_ref[0, gate_idx:gate_idx + 1, :] * acc


def _mm_qkv_kernel(x_ref, w_ref, cos_ref, sin_ref, qn_ref, kn_ref, o_ref, *, nq_tiles, nk_tiles, qk_norm):
    j = pl.program_id(1)
    is_q = j < nq_tiles
    is_qk = j < nq_tiles + nk_tiles
    post = jnp.where(is_q, HEAD_DIM ** -0.5, 1.0)
    gain = jnp.where(is_q, qn_ref[...], kn_ref[...])
    w = w_ref[...]
    tm, tn = o_ref.shape
    for r in range(0, tm, QKV_ROWS):
        acc = jnp.dot(x_ref[r:r + QKV_ROWS, :], w, preferred_element_type=F32)
        cos = cos_ref[r:r + QKV_ROWS, :]
        sin = sin_ref[r:r + QKV_ROWS, :]
        for s in range(0, tn, HEAD_DIM):
            raw = acc[:, s:s + HEAD_DIM]
            xh = raw
            if qk_norm:
                xh = xh * lax.rsqrt(jnp.mean(xh * xh, axis=-1, keepdims=True) + RMS_EPS) * gain
            xh = (xh * cos + pltpu.roll(xh, HEAD_DIM // 2, axis=1) * sin) * post
            o_ref[r:r + QKV_ROWS, s:s + HEAD_DIM] = jnp.where(is_qk, xh, raw).astype(o_ref.dtype)


def matmul(x, w, *, tm=1024, tn=512, out_dtype=BF16, rows=None, scale_tiles=0, scale=1.0,
           resid=None, qkv=None):
    m, k = x.shape
    m = m if rows is None else rows
    n = w.shape[1]
    tm = _pick(m if resid is None else math.gcd(m, resid[3].tile_unit), tm)
    tn = _pick(n, tn)
    grid = (m // tm, n // tn)
    x_spec = pl.BlockSpec((tm, k), lambda i, j: (i, 0))
    w_spec = pl.BlockSpec((k, tn), lambda i, j: (0, j))
    o_spec = pl.BlockSpec((tm, tn), lambda i, j: (i, j))
    if resid is not None:
        h, mods, gate_idx, seg = resid
        kern = functools.partial(_mm_resid_kernel, gate_idx=gate_idx)
        in_specs = [x_spec, w_spec, o_spec,
                    pl.BlockSpec((1, N_MOD, tn), lambda i, j: (seg.of_row(i * tm), 0, j))]
        args = (x, w, h, mods)
        out_dtype = F32
    elif qkv is not None:
        cos, sin, qn, kn, qk_norm = qkv
        kern = functools.partial(_mm_qkv_kernel, nq_tiles=N_HEADS * HEAD_DIM // tn,
                                 nk_tiles=N_KV_HEADS * HEAD_DIM // tn, qk_norm=qk_norm)
        tab_spec = pl.BlockSpec((tm, HEAD_DIM), lambda i, j: (i, 0))
        vec_spec = pl.BlockSpec((1, HEAD_DIM), lambda i, j: (0, 0))
        in_specs = [x_spec, w_spec, tab_spec, tab_spec, vec_spec, vec_spec]
        args = (x, w, cos, sin, qn.reshape(1, HEAD_DIM), kn.reshape(1, HEAD_DIM))
    else:
        kern = functools.partial(_mm_plain_kernel, scale_tiles=scale_tiles, scale=scale)
        in_specs = [x_spec, w_spec]
        args = (x, w)
    return pl.pallas_call(
        kern, grid=grid, in_specs=in_specs, out_specs=o_spec,
        out_shape=jax.ShapeDtypeStruct((m, n), out_dtype),
        compiler_params=_cparams(2),
        name="matmul",
    )(*args)


def _attn_kernel(*refs, tq, lat_keys, window, has_sink):
    it = iter(refs)
    sink_ref = next(it) if has_sink else None
    q_ref = next(it)
    kc_ref, vc_ref = next(it), next(it)
    kl_ref, vl_ref = (next(it), next(it)) if lat_keys else (None, None)
    o_ref = refs[-1]
    kvh = pl.program_id(1)
    qi = pl.program_id(2)
    nt = (((1,), (1,)), ((), ()))
    kc = kc_ref[...]
    vc = _with_ones(vc_ref[...])
    if lat_keys and window:
        seq = kl_ref.shape[0]
        band = min(tq + 2 * WINDOW, seq)
        start = pl.multiple_of(jnp.clip(qi * tq - WINDOW, 0, seq - band), LANES)
        kl = kl_ref[pl.ds(start, band), :]
        vl = _with_ones(vl_ref[pl.ds(start, band), :])
        q_pos = qi * tq + lax.broadcasted_iota(jnp.int32, (tq, band), 0)
        k_pos = start + lax.broadcasted_iota(jnp.int32, (tq, band), 1)
        in_window = jnp.abs(q_pos - k_pos) <= WINDOW
    elif lat_keys:
        kl = kl_ref[...]
        vl = _with_ones(vl_ref[...])
    for g in range(KV_GROUP):
        cols = slice(g * HEAD_DIM, (g + 1) * HEAD_DIM)
        qg = q_ref[:, cols]
        s_c = lax.dot_general(qg, kc, nt, preferred_element_type=F32)
        m = jnp.max(s_c, axis=-1, keepdims=True)
        if lat_keys:
            s_l = lax.dot_general(qg, kl, nt, preferred_element_type=F32)
            if window:
                s_l = jnp.where(in_window, s_l, NEG_INF)
            m = jnp.maximum(m, jnp.max(s_l, axis=-1, keepdims=True))
        if has_sink:
            sink = sink_ref[kvh * KV_GROUP + g]
            m = jnp.maximum(m, sink)
        acc = jnp.dot(jnp.exp((s_c - m).astype(BF16)), vc, preferred_element_type=F32)
        if lat_keys:
            acc = acc + jnp.dot(jnp.exp((s_l - m).astype(BF16)), vl, preferred_element_type=F32)
        den = acc[:, HEAD_DIM:HEAD_DIM + 1]
        if has_sink:
            den = den + jnp.exp(sink - m)
        o_ref[:, cols] = (acc[:, :HEAD_DIM] / den).astype(o_ref.dtype)


def _with_ones(v):
    return jnp.concatenate([v, jnp.ones_like(v)], axis=1)


def attention(qkv, dims, *, lat_queries, window, sink, out=None, out_rows=None):
    bsz, seq, n_ctx = dims
    n_lat = bsz * seq
    g_cols = KV_GROUP * HEAD_DIM
    k_col0 = N_HEADS * HEAD_DIM // HEAD_DIM
    v_col0 = k_col0 + N_KV_HEADS
    has_sink = sink is not None
    if lat_queries:
        tq = _pick(seq, 512)
        q_tiles = seq // tq
        q_row0 = 0
    else:
        tq = n_ctx
        q_tiles = 1
        q_row0 = n_lat // tq
    aliases = {}
    if out is not None:
        out_rows = out.shape[0]
    ctx_blk0 = n_lat // n_ctx
    in_specs = []
    args = []
    if has_sink:
        in_specs.append(pl.BlockSpec(memory_space=pltpu.SMEM))
        args.append(sink.astype(F32))
    in_specs += [
        pl.BlockSpec((tq, g_cols), lambda b, h, i: (q_row0 + b * q_tiles + i, h)),
        pl.BlockSpec((n_ctx, HEAD_DIM), lambda b, h, i: (ctx_blk0 + b, k_col0 + h)),
        pl.BlockSpec((n_ctx, HEAD_DIM), lambda b, h, i: (ctx_blk0 + b, v_col0 + h)),
    ]
    args += [qkv, qkv, qkv]
    if lat_queries:
        in_specs += [
            pl.BlockSpec((seq, HEAD_DIM), lambda b, h, i: (b, k_col0 + h)),
            pl.BlockSpec((seq, HEAD_DIM), lambda b, h, i: (b, v_col0 + h)),
        ]
        args += [qkv, qkv]
    if out is not None:
        aliases = {len(args): 0}
        in_specs.append(pl.BlockSpec(memory_space=pl.ANY))
        args.append(out)
    return pl.pallas_call(
        functools.partial(_attn_kernel, tq=tq, lat_keys=lat_queries, window=window, has_sink=has_sink),
        grid=(bsz, N_KV_HEADS, q_tiles),
        in_specs=in_specs,
        out_specs=pl.BlockSpec((tq, g_cols), lambda b, h, i: (q_row0 + b * q_tiles + i, h)),
        out_shape=jax.ShapeDtypeStruct((out_rows, N_HEADS * HEAD_DIM), BF16),
        input_output_aliases=aliases,
        compiler_params=_cparams(3),
        name="attention",
    )(*args)


def _mlstm_gate_kernel(x_ref, w_ref, b_ref, o_ref):
    g = jnp.dot(x_ref[...], w_ref[...], preferred_element_type=F32) + b_ref[...]
    g = GATE_CAP * jnp.tanh(g / GATE_CAP)
    log_f = jnp.minimum(g, 0.0) - jnp.log(1.0 + jnp.exp(-jnp.abs(g)))
    col = lax.broadcasted_iota(jnp.int32, g.shape, 1)
    is_f = (col // M_HEADS) % 2 == 1
    o_ref[...] = jnp.transpose(jnp.where(is_f, log_f, g))


def mlstm_gates(a, w_gate_pad, gate_b_pad):
    n, k = a.shape
    tm = _pick(n, 1024)
    return pl.pallas_call(
        _mlstm_gate_kernel,
        grid=(n // tm,),
        in_specs=[pl.BlockSpec((tm, k), lambda i: (i, 0)),
                  pl.BlockSpec((k, LANES), lambda i: (0, 0)),
                  pl.BlockSpec((1, LANES), lambda i: (0, 0))],
        out_specs=pl.BlockSpec((LANES, tm), lambda i: (0, i)),
        out_shape=jax.ShapeDtypeStruct((LANES, n), F32),
        compiler_params=_cparams(1),
        name="mlstm_gates",
    )(a, w_gate_pad, gate_b_pad)


def _mlstm_kernel(*refs):
    ct_ref, n_ref, m_ref = refs[-3:]
    c = pl.program_id(1)
    L = M_CHUNK

    @pl.when(c == 0)
    def _():
        ct_ref[...] = jnp.zeros_like(ct_ref)
        n_ref[...] = jnp.zeros_like(n_ref)
        m_ref[...] = jnp.zeros_like(m_ref)

    nt = (((1,), (1,)), ((), ()))
    tn = (((0,), (0,)), ((), ()))
    for d in range(2):
        q_ref, k_ref, v_ref, gi_ref, gf_ref = refs[5 * d:5 * d + 5]
        o_ref = refs[10 + d]
        diff = lax.broadcasted_iota(jnp.int32, (L, L), 0) - lax.broadcasted_iota(jnp.int32, (L, L), 1)
        if d == 1:
            diff = -diff
        mask = diff >= 0
        mask_t = diff <= 0
        eye = diff == 0
        for h in range(M_HEADS):
            st = d * M_HEADS + h
            qc = q_ref[:, h * M_QK_DIM:(h + 1) * M_QK_DIM]
            kc = k_ref[:, h * M_QK_DIM:(h + 1) * M_QK_DIM]
            vc = v_ref[:, h * M_V_DIM:(h + 1) * M_V_DIM]
            i_row = gi_ref[h:h + 1, :]
            f_row = gf_ref[h:h + 1, :]
            m_prev = m_ref[st, :, 0:1]
            n_prev = n_ref[st]
            ct_prev = ct_ref[st]
            f_col = jnp.sum(jnp.where(eye, f_row, 0.0), axis=1, keepdims=True)
            i_col = jnp.sum(jnp.where(eye, i_row, 0.0), axis=1, keepdims=True)
            b_col = jnp.sum(jnp.where(mask, f_row, 0.0), axis=1, keepdims=True)
            b_row = jnp.sum(jnp.where(mask_t, f_col, 0.0), axis=0, keepdims=True)
            log_inter = b_col + m_prev
            log_intra = jnp.where(mask, b_col - b_row + i_row, NEG_INF)
            m_t = jnp.maximum(log_inter, jnp.max(log_intra, axis=1, keepdims=True))
            w_inter = jnp.exp(log_inter - m_t)
            s_qk = lax.dot_general(qc, kc, nt, preferred_element_type=F32) * jnp.exp(log_intra - m_t)
            num = (jnp.dot(s_qk.astype(BF16), vc, preferred_element_type=F32)
                   + w_inter * jnp.dot(qc, ct_prev.astype(BF16), preferred_element_type=F32))
            den = (jnp.sum(s_qk, axis=1, keepdims=True)
                   + w_inter * jnp.sum(qc.astype(F32) * n_prev, axis=1, keepdims=True))
            h_out = num / jnp.maximum(jnp.abs(den), jnp.exp(-m_t))
            o_ref[:, h * M_V_DIM:(h + 1) * M_V_DIM] = h_out.astype(o_ref.dtype)
            b_tot = jnp.sum(f_row, axis=1, keepdims=True)
            log_w = b_tot - b_col + i_col
            m_new = jnp.maximum(b_tot + m_prev, jnp.max(log_w, axis=0, keepdims=True))
            decay = jnp.exp(b_tot + m_prev - m_new)
            w_k = jnp.exp(log_w - m_new)
            vw = (vc.astype(F32) * w_k).astype(BF16)
            ct_ref[st] = decay * ct_prev + lax.dot_general(kc, vw, tn, preferred_element_type=F32)
            n_ref[st] = decay * n_prev + jnp.sum(kc.astype(F32) * w_k, axis=0, keepdims=True)
            m_ref[st] = jnp.broadcast_to(m_new, (1, LANES))


def mlstm_scan(proj, gates_t, dims):
    bsz, seq, n_ctx = dims
    nt_rows = proj.shape[0]
    L = M_CHUNK
    ncc, nlc = n_ctx // L, seq // L
    lat_blocks = bsz * nlc
    qk_w = M_HEADS * M_QK_DIM

    def row_blk(b, d, c):
        cc = c if d == 0 else ncc - 1 - c
        lc = c - ncc if d == 0 else nlc - 1 - (c - ncc)
        return jnp.where(c < ncc, lat_blocks + b * ncc + cc, b * nlc + lc)

    def dir_specs(d):
        return [
            pl.BlockSpec((L, qk_w), lambda b, c: (row_blk(b, d, c), 0)),
            pl.BlockSpec((L, qk_w), lambda b, c: (row_blk(b, d, c), 1)),
            pl.BlockSpec((L, D_MODEL), lambda b, c: (row_blk(b, d, c), 1)),
            pl.BlockSpec((M_HEADS, L), lambda b, c: (2 * d, row_blk(b, d, c))),
            pl.BlockSpec((M_HEADS, L), lambda b, c: (2 * d + 1, row_blk(b, d, c))),
        ]

    return pl.pallas_call(
        _mlstm_kernel,
        grid=(bsz, ncc + nlc),
        in_specs=dir_specs(0) + dir_specs(1),
        out_specs=[pl.BlockSpec((L, D_MODEL), lambda b, c: (row_blk(b, 0, c), 0)),
                   pl.BlockSpec((L, D_MODEL), lambda b, c: (row_blk(b, 1, c), 0))],
        out_shape=[jax.ShapeDtypeStruct((nt_rows, D_MODEL), BF16)] * 2,
        scratch_shapes=[pltpu.VMEM((2 * M_HEADS, M_QK_DIM, M_V_DIM), F32),
                        pltpu.VMEM((2 * M_HEADS, 1, M_QK_DIM), F32),
                        pltpu.VMEM((2 * M_HEADS, 1, LANES), F32)],
        compiler_params=_cparams(2),
        name="mlstm_scan",
    )(*([proj, proj, proj, gates_t, gates_t] * 2))


def _mlstm_finish_kernel(hf_ref, hb_ref, og_ref, g_ref, o_ref):
    for h in range(M_HEADS):
        sl = slice(h * M_V_DIM, (h + 1) * M_V_DIM)
        x = hf_ref[:, sl].astype(F32) + hb_ref[:, sl].astype(F32)
        y = x * lax.rsqrt(jnp.mean(x * x, axis=-1, keepdims=True) + RMS_EPS) * g_ref[:, sl]
        o_ref[:, sl] = (y * jax.nn.sigmoid(og_ref[:, sl].astype(F32))).astype(o_ref.dtype)


def mlstm_finish(h_fwd, h_bwd, proj, head_norm):
    n, d = h_fwd.shape
    tm = _pick(n, 256)
    row_spec = pl.BlockSpec((tm, d), lambda i: (i, 0))
    return pl.pallas_call(
        _mlstm_finish_kernel,
        grid=(n // tm,),
        in_specs=[row_spec, row_spec,
                  pl.BlockSpec((tm, d), lambda i: (i, 2)),
                  pl.BlockSpec((1, d), lambda i: (0, 0))],
        out_specs=row_spec,
        out_shape=jax.ShapeDtypeStruct((n, d), BF16),
        compiler_params=_cparams(1),
        name="mlstm_finish",
    )(h_fwd, h_bwd, proj, head_norm.reshape(1, d))


def _pack_rows(x):
    half = x.shape[1] // 2
    return _pack_pair(x[:, :half], x[:, half:])


def _pack_pair(lo, hi):
    lo = lax.bitcast_convert_type(lo.astype(BF16).astype(F32), U32)
    hi = lax.bitcast_convert_type(hi.astype(BF16).astype(F32), U32)
    return (hi & jnp.uint32(0xFFFF0000)) | (lo >> 16)


def _unpack_rows(p):
    lo = lax.bitcast_convert_type(p << 16, F32)
    hi = lax.bitcast_convert_type(p & jnp.uint32(0xFFFF0000), F32)
    return lo, hi


def _route(a, w_t, bias, counts):
    tm = a.shape[0]
    per = N_EXPERTS // N_GROUPS
    nt = (((1,), (1,)), ((), ()))
    w_hi = w_t.astype(BF16)
    w_lo = (w_t - w_hi.astype(F32)).astype(BF16)
    a_hi = a.astype(BF16)
    a_lo = (a - a_hi.astype(F32)).astype(BF16)
    logits = (lax.dot_general(w_hi, a_hi, nt, preferred_element_type=F32)
              + lax.dot_general(w_lo, a_hi, nt, preferred_element_type=F32)
              + lax.dot_general(w_hi, a_lo, nt, preferred_element_type=F32))
    scores = jax.nn.sigmoid(logits).reshape(N_GROUPS, per, tm)
    biased = scores + bias.reshape(N_GROUPS, per, 1)
    e_iota = lax.broadcasted_iota(jnp.int32, (N_GROUPS, per, tm), 1).astype(F32)
    g_iota = lax.broadcasted_iota(jnp.int32, (N_GROUPS, 1, tm), 0).astype(F32)
    lin_iota = lax.broadcasted_iota(jnp.int32, (N_GROUPS, per, tm), 0).astype(F32) * per + e_iota
    m1 = jnp.max(biased, axis=1, keepdims=True)
    i1 = jnp.min(jnp.where(biased == m1, e_iota, float(per)), axis=1, keepdims=True)
    m2 = jnp.max(jnp.where(e_iota == i1, NEG_INF, biased), axis=1, keepdims=True)
    gscore = m1 + m2
    gsel = jnp.zeros(gscore.shape, F32)
    for _ in range(TOPK_GROUPS):
        cur = jnp.where(gsel > 0.0, NEG_INF, gscore)
        gm = jnp.max(cur, axis=0, keepdims=True)
        gi = jnp.min(jnp.where(cur == gm, g_iota, float(N_GROUPS)), axis=0, keepdims=True)
        gsel = jnp.where(g_iota == gi, 1.0, gsel)
    cand = jnp.where(gsel > 0.0, biased, NEG_INF)
    sel = jnp.zeros(cand.shape, F32)
    picks = []
    for _ in range(TOP_K):
        cur = jnp.where(sel > 0.0, NEG_INF, cand)
        em = jnp.max(jnp.max(cur, axis=1, keepdims=True), axis=0, keepdims=True)
        hit = jnp.where(cur == em, lin_iota, float(N_EXPERTS))
        ei = jnp.min(jnp.min(hit, axis=1, keepdims=True), axis=0, keepdims=True)
        sel = jnp.where(lin_iota == ei, 1.0, sel)
        picks.append(ei)
    sel2 = sel.reshape(N_EXPERTS, tm)
    before = lax.broadcasted_iota(jnp.int32, (tm, tm), 0) < lax.broadcasted_iota(jnp.int32, (tm, tm), 1)
    rank = jnp.dot(sel2.astype(BF16), jnp.where(before, 1.0, 0.0).astype(BF16),
                   preferred_element_type=F32) + counts
    new_counts = counts + jnp.sum(sel2, axis=1, keepdims=True)
    rank3 = rank.reshape(N_GROUPS, per, tm)
    wsum = jnp.sum(jnp.sum(jnp.where(sel > 0.0, scores, 0.0), axis=1, keepdims=True), axis=0, keepdims=True)

    def pick(ei, table):
        v = jnp.where(lin_iota == ei, table, 0.0)
        return jnp.sum(jnp.sum(v, axis=1, keepdims=True), axis=0, keepdims=True).reshape(1, tm)

    ids = jnp.concatenate([ei.reshape(1, tm) for ei in picks], axis=0).astype(I32)
    ranks = jnp.concatenate([pick(ei, rank3) for ei in picks], axis=0).astype(I32)
    w = jnp.concatenate([pick(ei, scores) for ei in picks], axis=0) / wsum.reshape(1, tm) * ROUTED_SCALE
    wmat = jnp.transpose(jnp.concatenate([w, jnp.zeros((LANES - TOP_K, tm), F32)], axis=0))
    return ids, ranks, wmat, new_counts


def _dest_kernel(starts_ref, ids_ref, rank_ref, dest_ref):
    ids = ids_ref[...]
    acc = rank_ref[...]
    for e in range(N_EXPERTS):
        acc = acc + jnp.where(ids == e, starts_ref[e], 0)
    dest_ref[...] = acc


def dest_rows(starts, ids, ranks):
    return pl.pallas_call(
        _dest_kernel,
        in_specs=[pl.BlockSpec(memory_space=pltpu.SMEM), pl.BlockSpec(memory_space=pltpu.VMEM),
                  pl.BlockSpec(memory_space=pltpu.VMEM)],
        out_specs=pl.BlockSpec(memory_space=pltpu.VMEM),
        out_shape=jax.ShapeDtypeStruct(ids.shape, I32),
        compiler_params=pltpu.CompilerParams(vmem_limit_bytes=VMEM_LIMIT),
        name="dest_rows",
    )(starts, ids, ranks)


def _dispatch_kernel(starts_ref, ends_ref, dest_ref, f_ref, gu_ref, dn_ref, xs_ref, sh_ref,
                     zero_ref, gu_bf, dn_bf, sem):
    tm = f_ref.shape[0]

    def zero_copy(e):
        row0 = pl.multiple_of(ends_ref[e] - MOE_TM, MOE_TM)
        return pltpu.make_async_copy(zero_ref, xs_ref.at[pl.ds(row0, MOE_TM)], sem)

    @pl.when(pl.program_id(0) == 0)
    def _():
        gu_bf[...] = gu_ref[...].astype(BF16)
        dn_bf[...] = dn_ref[...].astype(BF16)
        zero_ref[...] = jnp.zeros_like(zero_ref)
        for e in range(N_EXPERTS):
            @pl.when(ends_ref[e] > starts_ref[e])
            def _():
                zero_copy(e).start()
        for e in range(N_EXPERTS):
            @pl.when(ends_ref[e] > starts_ref[e])
            def _():
                zero_copy(e).wait()

    def row_copy(t, d):
        return pltpu.make_async_copy(f_ref.at[pl.ds(t, 1)], xs_ref.at[pl.ds(d, 1)], sem)

    def issue(t, c):
        for k in range(TOP_K):
            row_copy(t, dest_ref[k, t]).start()
        return c

    def drain(t, c):
        for k in range(TOP_K):
            row_copy(0, 0).wait()
        return c

    lax.fori_loop(0, tm, issue, 0)
    _swiglu_expert(f_ref, gu_bf, dn_bf, sh_ref, pack_out=False)
    lax.fori_loop(0, tm, drain, 0)


def dispatch(starts, ends, dest, packed, n_rows, shared_gu, shared_down):
    n, half = packed.shape
    tm = _pick(n, 256)

    def whole(shape):
        return pl.BlockSpec(shape, lambda i, s, e: (0,) * len(shape))

    return pl.pallas_call(
        _dispatch_kernel,
        grid_spec=pltpu.PrefetchScalarGridSpec(
            num_scalar_prefetch=2, grid=(n // tm,),
            in_specs=[pl.BlockSpec((TOP_K, tm), lambda i, s, e: (0, i), memory_space=pltpu.SMEM),
                      pl.BlockSpec((tm, half), lambda i, s, e: (i, 0)),
                      whole(shared_gu.shape), whole(shared_down.shape)],
            out_specs=[pl.BlockSpec(memory_space=pl.ANY),
                       pl.BlockSpec((tm, 2 * half), lambda i, s, e: (i, 0))],
            scratch_shapes=[pltpu.VMEM((MOE_TM, half), U32), pltpu.VMEM(shared_gu.shape, BF16),
                            pltpu.VMEM(shared_down.shape, BF16), pltpu.SemaphoreType.DMA(())]),
        out_shape=[jax.ShapeDtypeStruct((n_rows, half), U32), jax.ShapeDtypeStruct((n, 2 * half), BF16)],
        compiler_params=_cparams(1),
        name="dispatch",
    )(starts, ends, dest, packed, shared_gu, shared_down)


def _swiglu_expert(x_ref, gu_bf, dn_bf, o_ref, pack_out):
    half = x_ref.shape[1]
    gu = None
    for c in range(0, half, MOE_CHUNK):
        lo, hi = _unpack_rows(x_ref[:, c:c + MOE_CHUNK])
        part = (jnp.dot(lo.astype(BF16), gu_bf[c:c + MOE_CHUNK, :], preferred_element_type=F32)
                + jnp.dot(hi.astype(BF16), gu_bf[half + c:half + c + MOE_CHUNK, :], preferred_element_type=F32))
        gu = part if gu is None else gu + part
    g = gu[:, :D_EXPERT]
    act = (g * jax.nn.sigmoid(g) * gu[:, D_EXPERT:]).astype(BF16)
    for c in range(0, half, MOE_CHUNK):
        y_lo = jnp.dot(act, dn_bf[:, c:c + MOE_CHUNK], preferred_element_type=F32)
        y_hi = jnp.dot(act, dn_bf[:, half + c:half + c + MOE_CHUNK], preferred_element_type=F32)
        if pack_out:
            o_ref[:, c:c + MOE_CHUNK] = _pack_pair(y_lo, y_hi)
        else:
            o_ref[:, c:c + MOE_CHUNK] = y_lo.astype(o_ref.dtype)
            o_ref[:, half + c:half + c + MOE_CHUNK] = y_hi.astype(o_ref.dtype)


def _grouped_kernel(te_ref, nt_ref, x_ref, gu_ref, dn_ref, o_ref, gu_bf, dn_bf):
    j = pl.program_id(0)

    @pl.when(j < nt_ref[0])
    def _():
        @pl.when(jnp.logical_or(j == 0, te_ref[j] != te_ref[jnp.maximum(j - 1, 0)]))
        def _():
            gu_bf[...] = gu_ref[0].astype(BF16)
            dn_bf[...] = dn_ref[0].astype(BF16)

        _swiglu_expert(x_ref, gu_bf, dn_bf, o_ref, pack_out=True)


def grouped_experts(tile_expert, n_tiles, xs, exp_gu, exp_down):
    rows, half = xs.shape
    d = 2 * half

    def tile(j, nt):
        return jnp.minimum(j, nt[0] - 1)

    return pl.pallas_call(
        _grouped_kernel,
        grid_spec=pltpu.PrefetchScalarGridSpec(
            num_scalar_prefetch=2, grid=(rows // MOE_TM,),
            in_specs=[pl.BlockSpec((MOE_TM, half), lambda j, te, nt: (tile(j, nt), 0)),
                      pl.BlockSpec((1, d, 2 * D_EXPERT), lambda j, te, nt: (te[tile(j, nt)], 0, 0)),
                      pl.BlockSpec((1, D_EXPERT, d), lambda j, te, nt: (te[tile(j, nt)], 0, 0))],
            out_specs=pl.BlockSpec((MOE_TM, half), lambda j, te, nt: (tile(j, nt), 0)),
            scratch_shapes=[pltpu.VMEM((d, 2 * D_EXPERT), BF16), pltpu.VMEM((D_EXPERT, d), BF16)]),
        out_shape=jax.ShapeDtypeStruct((rows, half), U32),
        compiler_params=_cparams(1),
        name="grouped_experts",
    )(tile_expert, n_tiles, xs, exp_gu, exp_down)


def _gather_combine(dest_ref, next_dest_ref, ys_ref, w_ref, sh_ref, buf_ref, sems):
    tm, d = sh_ref.shape
    half = d // 2
    step = pl.program_id(0)
    slot = step % 2

    def row_copy(s, k, t, src):
        return pltpu.make_async_copy(ys_ref.at[pl.ds(src, 1)], buf_ref.at[s, k, pl.ds(t, 1)], sems.at[s])

    def request(idx_ref, s):
        def body(t, c):
            for k in range(TOP_K):
                row_copy(s, k, t, idx_ref[k, t]).start()
            return c
        lax.fori_loop(0, tm, body, 0)

    @pl.when(step == 0)
    def _():
        request(dest_ref, 0)

    @pl.when(step + 1 < pl.num_programs(0))
    def _():
        request(next_dest_ref, 1 - slot)

    def drain(t, c):
        for k in range(TOP_K):
            row_copy(slot, 0, 0, 0).wait()
        return c

    lax.fori_loop(0, tm, drain, 0)
    w = w_ref[...]
    y_lo = sh_ref[:, :half].astype(F32)
    y_hi = sh_ref[:, half:].astype(F32)
    for k in range(TOP_K):
        lo, hi = _unpack_rows(buf_ref[slot, k])
        y_lo = y_lo + w[:, k:k + 1] * lo
        y_hi = y_hi + w[:, k:k + 1] * hi
    return jnp.concatenate([y_lo, y_hi], axis=1)


class MoeOut(NamedTuple):
    dest: jax.Array
    ys: jax.Array
    wmat: jax.Array
    shared: jax.Array


def moe_experts(packed, ids, ranks, wmat, counts, moe):
    _, _, exp_gu, exp_down, shared_gu, shared_down = moe
    n = packed.shape[0]
    cnt = counts[:, 0].astype(I32)
    padded = (cnt + MOE_TM - 1) // MOE_TM * MOE_TM
    ends = jnp.cumsum(padded)
    starts = ends - padded
    n_max = n * TOP_K // MOE_TM + N_EXPERTS
    n_tiles = (ends[-1] // MOE_TM).reshape(1)
    tile_row0 = jnp.arange(n_max, dtype=I32) * MOE_TM
    tile_expert = jnp.minimum(jnp.sum(ends[None, :] <= tile_row0[:, None], axis=1), N_EXPERTS - 1).astype(I32)
    dest = dest_rows(starts, ids, ranks)
    xs, shared = dispatch(starts, ends, dest, packed, n_max * MOE_TM, shared_gu, shared_down)
    ys = grouped_experts(tile_expert, n_tiles, xs, exp_gu, exp_down)
    return MoeOut(dest, ys, wmat, shared)


def _rope_tables(seq, n_rows):
    pos = jnp.arange(seq)
    row = (pos // GRID_W).astype(F32)
    col = (pos % GRID_W).astype(F32)
    n_freq = HEAD_DIM // 4
    inv_freq = ROPE_THETA ** (-jnp.arange(n_freq, dtype=F32) / n_freq)
    ang = jnp.concatenate([row[:, None] * inv_freq, col[:, None] * inv_freq], axis=-1)
    cos, sin = jnp.cos(ang), jnp.sin(ang)
    cos_full = jnp.concatenate([cos, cos], axis=-1)
    sin_full = jnp.concatenate([-sin, sin], axis=-1)
    return cos_full, sin_full


def _trunk(x, c, ctx, c_ctx, layers, final_norm):
    bsz, seq, d = x.shape
    n_ctx = ctx.shape[1]
    dims = (bsz, seq, n_ctx)
    n_lat = bsz * seq
    n_all = n_lat + bsz * n_ctx
    depth = len(layers)

    seg = _Segments(seq, bsz, bsz * n_ctx)

    h = jnp.concatenate([x.reshape(n_lat, d), ctx.reshape(bsz * n_ctx, d)], axis=0)
    mod_rows = -(-(bsz + 1) // 8) * 8
    cond = jnp.zeros((mod_rows, d), F32).at[:bsz].set(c).at[bsz].set(c_ctx)

    cos1, sin1 = _rope_tables(seq, n_all)
    cos_t = jnp.concatenate([jnp.tile(cos1, (bsz, 1)), jnp.ones((bsz * n_ctx, HEAD_DIM), F32)], axis=0)
    sin_t = jnp.concatenate([jnp.tile(sin1, (bsz, 1)), jnp.zeros((bsz * n_ctx, HEAD_DIM), F32)], axis=0)
    ones_hd = jnp.ones((HEAD_DIM,), F32)

    tables = [ada_table(cond, *layer[1]) for layer in layers]
    a, = rowwise(h, layers[0][2], seg, shift=(tables[0], 0))
    for li, (kind, ada, norm1, mixer, norm2, moe) in enumerate(layers):
        need_ctx = li < depth - 1
        rows_out = n_all if need_ctx else n_lat
        mods = tables[li]
        resid = (h, mods, 2, seg)
        if kind == "mlstm":
            w_in, gate_b, head_norm, w_o = mixer
            n_main = 2 * M_HEADS * M_QK_DIM + M_HEADS * M_V_DIM + D_MODEL
            proj = matmul(a, w_in[:, :n_main].astype(BF16), scale_tiles=M_HEADS * M_QK_DIM // 512,
                          scale=M_QK_DIM ** -0.5)
            n_gate = 4 * M_HEADS
            w_gate = jnp.zeros((d, LANES), BF16).at[:, :n_gate].set(w_in[:, n_main:].astype(BF16))
            gate_b_pad = jnp.zeros((1, LANES), F32).at[0, :n_gate].set(gate_b.astype(F32))
            gates_t = mlstm_gates(a, w_gate, gate_b_pad)
            h_fwd, h_bwd = mlstm_scan(proj, gates_t, dims)
            mixed = mlstm_finish(h_fwd, h_bwd, proj, head_norm)
            h = matmul(mixed, w_o.astype(BF16), resid=resid, rows=rows_out)
        else:
            if kind == "global":
                w_qkv, q_norm, k_norm, w_o = mixer
                qkv = matmul(a, w_qkv.astype(BF16), qkv=(cos_t, sin_t, q_norm, k_norm, True))
                sink = None
            else:
                w_qkv, sink, w_o = mixer
                qkv = matmul(a, w_qkv.astype(BF16), qkv=(cos_t, sin_t, ones_hd, ones_hd, False))
            mixed = attention(qkv, dims, lat_queries=True, window=kind == "swa", sink=sink, out_rows=rows_out)
            if need_ctx:
                mixed = attention(qkv, dims, lat_queries=False, window=False, sink=sink, out=mixed)
            h = matmul(mixed, w_o.astype(BF16), resid=resid, rows=rows_out)
        router_w, router_b, exp_gu, exp_down, shared_gu, shared_down = moe
        routed = rowwise(h, norm2, seg, shift=(mods, 3), rows=rows_out, route=(router_w.T, router_b))
        y = moe_experts(*routed, moe)
        if li + 1 < depth:
            h, a = rowwise(h, layers[li + 1][2], seg, resid=(y, mods, 5), shift=(tables[li + 1], 0), rows=rows_out)
        else:
            _h, out = rowwise(h, final_norm, seg, resid=(y, mods, 5), rows=rows_out, out_dtype=F32)
    return out.reshape(bsz, seq, d)


def kernel(x, c, ctx, c_ctx, l0_ada_down, l0_ada_up, l0_ada_b, l0_norm1, l0_attn_qkv, l0_q_norm, l0_k_norm, l0_attn_o, l0_norm2, l0_router_w, l0_router_b, l0_exp_gu, l0_exp_down, l0_shared_gu, l0_shared_down, l1_ada_down, l1_ada_up, l1_ada_b, l1_norm1, l1_mlstm_in, l1_mlstm_gate_b, l1_mlstm_head_norm, l1_mlstm_o, l1_norm2, l1_router_w, l1_router_b, l1_exp_gu, l1_exp_down, l1_shared_gu, l1_shared_down, l2_ada_down, l2_ada_up, l2_ada_b, l2_norm1, l2_swa_qkv, l2_swa_sink, l2_swa_o, l2_norm2, l2_router_w, l2_router_b, l2_exp_gu, l2_exp_down, l2_shared_gu, l2_shared_down, l3_ada_down, l3_ada_up, l3_ada_b, l3_norm1, l3_attn_qkv, l3_q_norm, l3_k_norm, l3_attn_o, l3_norm2, l3_router_w, l3_router_b, l3_exp_gu, l3_exp_down, l3_shared_gu, l3_shared_down, final_norm):
    layers = [
        ("global", (l0_ada_down, l0_ada_up, l0_ada_b), l0_norm1, (l0_attn_qkv, l0_q_norm, l0_k_norm, l0_attn_o), l0_norm2,
         (l0_router_w, l0_router_b, l0_exp_gu, l0_exp_down, l0_shared_gu, l0_shared_down)),
        ("mlstm", (l1_ada_down, l1_ada_up, l1_ada_b), l1_norm1, (l1_mlstm_in, l1_mlstm_gate_b, l1_mlstm_head_norm, l1_mlstm_o), l1_norm2,
         (l1_router_w, l1_router_b, l1_exp_gu, l1_exp_down, l1_shared_gu, l1_shared_down)),
        ("swa", (l2_ada_down, l2_ada_up, l2_ada_b), l2_norm1, (l2_swa_qkv, l2_swa_sink, l2_swa_o), l2_norm2,
         (l2_router_w, l2_router_b, l2_exp_gu, l2_exp_down, l2_shared_gu, l2_shared_down)),
        ("global", (l3_ada_down, l3_ada_up, l3_ada_b), l3_norm1, (l3_attn_qkv, l3_q_norm, l3_k_norm, l3_attn_o), l3_norm2,
         (l3_router_w, l3_router_b, l3_exp_gu, l3_exp_down, l3_shared_gu, l3_shared_down)),
    ]
    return _trunk(x, c, ctx, c_ctx, layers, final_norm)
```

```python
import functools
import math
from typing import NamedTuple

import jax
import jax.numpy as jnp
from jax import lax
from jax.experimental import pallas as pl
from jax.experimental.pallas import tpu as pltpu

F32 = jnp.float32
BF16 = jnp.bfloat16
U32 = jnp.uint32
I32 = jnp.int32

D_MODEL = 4096
GRID_W = 64
RMS_EPS = 1e-6
N_MOD = 6
N_HEADS = 32
N_KV_HEADS = 8
HEAD_DIM = D_MODEL // N_HEADS
KV_GROUP = N_HEADS // N_KV_HEADS
ROPE_THETA = 10000.0
WINDOW = 128
M_HEADS = 8
M_V_DIM = D_MODEL // M_HEADS
M_QK_DIM = M_V_DIM // 2
M_CHUNK = 128
GATE_CAP = 15.0
N_EXPERTS = 64
TOP_K = 8
N_GROUPS = 8
TOPK_GROUPS = 4
D_EXPERT = 192
ROUTED_SCALE = 2.5

MOE_TM = 512
MOE_CHUNK = 512
QKV_ROWS = 256
LANES = 128
VMEM_LIMIT = 56 * 1024 * 1024
NEG_INF = float("-inf")


def _cparams(n_axes):
    return pltpu.CompilerParams(dimension_semantics=("arbitrary",) * n_axes,
                                vmem_limit_bytes=VMEM_LIMIT)


def _pick(n, pref):
    t = pref
    while n % t:
        t //= 2
    return t


class _Segments(NamedTuple):
    seq: int
    bsz: int
    n_ctx_rows: int

    @property
    def tile_unit(self):
        return math.gcd(self.seq, self.n_ctx_rows)

    def of_row(self, row):
        return jnp.minimum(row // self.seq, self.bsz)


def _ada_kernel(cond_ref, down_ref, up_ref, b_ref, out_ref):
    c = cond_ref[...]
    t = jnp.dot(c * jax.nn.sigmoid(c), down_ref[...], precision=lax.Precision.HIGHEST,
                preferred_element_type=F32)
    out_ref[...] = jnp.dot(t, up_ref[...], precision=lax.Precision.HIGHEST,
                           preferred_element_type=F32) + b_ref[...]


def ada_table(cond_pad, down, up, bias):
    r, d = cond_pad.shape
    rank = down.shape[1]
    n = up.shape[1]
    tn = 2048
    out = pl.pallas_call(
        _ada_kernel,
        grid=(n // tn,),
        in_specs=[pl.BlockSpec((r, d), lambda j: (0, 0)),
                  pl.BlockSpec((d, rank), lambda j: (0, 0)),
                  pl.BlockSpec((rank, tn), lambda j: (0, j)),
                  pl.BlockSpec((1, tn), lambda j: (0, j))],
        out_specs=pl.BlockSpec((r, tn), lambda j: (0, j)),
        out_shape=jax.ShapeDtypeStruct((r, n), F32),
        compiler_params=_cparams(1),
        name="ada_table",
    )(cond_pad, down, up, bias.reshape(1, n))
    return out.reshape(r, N_MOD, d)


def _rowwise_kernel(*refs, gate_idx, shift_idx, route, moe_resid):
    it = iter(refs)
    h_ref = next(it)
    if moe_resid:
        y = _gather_combine(next(it), next(it), next(it), next(it), next(it), refs[-2], refs[-1])
    elif gate_idx is not None:
        y = next(it)[...].astype(F32)
    gmod_ref = next(it) if gate_idx is not None else None
    g_ref = next(it)
    smod_ref = next(it) if shift_idx is not None else None
    wt_ref, rb_ref = (next(it), next(it)) if route else (None, None)
    h = h_ref[...]
    if gate_idx is not None:
        h = h + gmod_ref[0, gate_idx:gate_idx + 1, :] * y
        next(it)[...] = h
    a = h * lax.rsqrt(jnp.mean(h * h, axis=-1, keepdims=True) + RMS_EPS) * g_ref[...]
    if shift_idx is not None:
        a = a * (1.0 + smod_ref[0, shift_idx + 1:shift_idx + 2, :]) + smod_ref[0, shift_idx:shift_idx + 1, :]
    if not route:
        a_ref = next(it)
        a_ref[...] = a.astype(a_ref.dtype)
        return
    packed_ref, ids_ref, rank_ref, w_ref, cnt_ref = (next(it) for _ in range(5))

    @pl.when(pl.program_id(0) == 0)
    def _():
        cnt_ref[...] = jnp.zeros_like(cnt_ref)

    packed_ref[...] = _pack_rows(a)
    ids, ranks, wmat, counts = _route(a, wt_ref[...], rb_ref[...], cnt_ref[:, 0:1])
    ids_ref[...] = ids
    rank_ref[...] = ranks
    w_ref[...] = wmat
    cnt_ref[...] = jnp.broadcast_to(counts, cnt_ref.shape)


def rowwise(h, gain, seg, *, resid=None, shift=None, out_dtype=BF16, rows=None, route=None):
    n, d = h.shape
    rows = n if rows is None else rows
    moe_resid = resid is not None and isinstance(resid[0], MoeOut)
    tm = _pick(seg.tile_unit, 128 if moe_resid else 256)
    row_spec = pl.BlockSpec((tm, d), lambda i: (i, 0))
    mod_spec = pl.BlockSpec((1, N_MOD, d), lambda i: (seg.of_row(i * tm), 0, 0))
    in_specs, args, scratch = [row_spec], [h], []
    if moe_resid:
        last = rows // tm - 1
        in_specs += [pl.BlockSpec((TOP_K, tm), lambda i: (0, i), memory_space=pltpu.SMEM),
                     pl.BlockSpec((TOP_K, tm), lambda i: (0, jnp.minimum(i + 1, last)), memory_space=pltpu.SMEM),
                     pl.BlockSpec(memory_space=pl.ANY),
                     pl.BlockSpec((tm, LANES), lambda i: (i, 0)),
                     row_spec, mod_spec]
        moe_out = resid[0]
        args += [moe_out.dest, moe_out.dest, moe_out.ys, moe_out.wmat, moe_out.shared, resid[1]]
        scratch = [pltpu.VMEM((2, TOP_K, tm, d // 2), U32), pltpu.SemaphoreType.DMA((2,))]
    elif resid is not None:
        in_specs += [row_spec, mod_spec]
        args += [resid[0], resid[1]]
    in_specs.append(pl.BlockSpec((1, d), lambda i: (0, 0)))
    args.append(gain.reshape(1, d))
    if shift is not None:
        in_specs.append(mod_spec)
        args.append(shift[0])
    if route is not None:
        in_specs += [pl.BlockSpec((N_EXPERTS, d), lambda i: (0, 0)),
                     pl.BlockSpec((N_EXPERTS, 1), lambda i: (0, 0))]
        args += [route[0], route[1].reshape(N_EXPERTS, 1)]
    out_specs, out_shape = [], []
    if resid is not None:
        out_specs.append(row_spec)
        out_shape.append(jax.ShapeDtypeStruct((rows, d), F32))
    if route is None:
        out_specs.append(row_spec)
        out_shape.append(jax.ShapeDtypeStruct((rows, d), out_dtype))
    else:
        out_specs += [pl.BlockSpec((tm, d // 2), lambda i: (i, 0)),
                      pl.BlockSpec((TOP_K, tm), lambda i: (0, i)),
                      pl.BlockSpec((TOP_K, tm), lambda i: (0, i)),
                      pl.BlockSpec((tm, LANES), lambda i: (i, 0)),
                      pl.BlockSpec((N_EXPERTS, LANES), lambda i: (0, 0))]
        out_shape += [jax.ShapeDtypeStruct((rows, d // 2), U32),
                      jax.ShapeDtypeStruct((TOP_K, rows), I32),
                      jax.ShapeDtypeStruct((TOP_K, rows), I32),
                      jax.ShapeDtypeStruct((rows, LANES), F32),
                      jax.ShapeDtypeStruct((N_EXPERTS, LANES), F32)]
    return pl.pallas_call(
        functools.partial(_rowwise_kernel, gate_idx=None if resid is None else resid[2],
                          shift_idx=None if shift is None else shift[1], route=route is not None,
                          moe_resid=moe_resid),
        grid=(rows // tm,),
        in_specs=in_specs, out_specs=out_specs, out_shape=out_shape, scratch_shapes=scratch,
        compiler_params=_cparams(1),
        name="rowwise",
    )(*args)


def _mm_plain_kernel(x_ref, w_ref, o_ref, *, scale_tiles, scale):
    acc = jnp.dot(x_ref[...], w_ref[...], preferred_element_type=F32)
    if scale_tiles:
        acc = acc * jnp.where(pl.program_id(1) < scale_tiles, scale, 1.0)
    o_ref[...] = acc.astype(o_ref.dtype)


def _mm_resid_kernel(x_ref, w_ref, h_ref, mod_ref, o_ref, *, gate_idx):
    acc = jnp.dot(x_ref[...], w_ref[...], preferred_element_type=F32)
    o_ref[...] = h_ref[...] + mod_ref[0, gate_idx:gate_idx + 1, :] * acc


def _mm_qkv_kernel(x_ref, w_ref, cos_ref, sin_ref, qn_ref, kn_ref, o_ref, *, nq_tiles, nk_tiles, qk_norm):
    j = pl.program_id(1)
    is_q = j < nq_tiles
    is_qk = j < nq_tiles + nk_tiles
    post = jnp.where(is_q, HEAD_DIM ** -0.5, 1.0)
    gain = jnp.where(is_q, qn_ref[...], kn_ref[...])
    w = w_ref[...]
    tm, tn = o_ref.shape
    def product(r):
        return jnp.dot(x_ref[r:r + QKV_ROWS, :], w, preferred_element_type=F32)

    nxt = product(0)
    for r in range(0, tm, QKV_ROWS):
        acc = nxt
        if r + QKV_ROWS < tm:
            nxt = product(r + QKV_ROWS)
        cos = cos_ref[r:r + QKV_ROWS, :]
        sin = sin_ref[r:r + QKV_ROWS, :]
        for s in range(0, tn, HEAD_DIM):
            raw = acc[:, s:s + HEAD_DIM]
            xh = raw
            if qk_norm:
                xh = xh * lax.rsqrt(jnp.mean(xh * xh, axis=-1, keepdims=True) + RMS_EPS) * gain
            xh = (xh * cos + pltpu.roll(xh, HEAD_DIM // 2, axis=1) * sin) * post
            o_ref[r:r + QKV_ROWS, s:s + HEAD_DIM] = jnp.where(is_qk, xh, raw).astype(o_ref.dtype)


def matmul(x, w, *, tm=1024, tn=512, out_dtype=BF16, rows=None, scale_tiles=0, scale=1.0,
           resid=None, qkv=None):
    m, k = x.shape
    m = m if rows is None else rows
    n = w.shape[1]
    tm = _pick(m if resid is None else math.gcd(m, resid[3].tile_unit), tm)
    tn = _pick(n, tn)
    grid = (m // tm, n // tn)
    x_spec = pl.BlockSpec((tm, k), lambda i, j: (i, 0))
    w_spec = pl.BlockSpec((k, tn), lambda i, j: (0, j))
    o_spec = pl.BlockSpec((tm, tn), lambda i, j: (i, j))
    if resid is not None:
        h, mods, gate_idx, seg = resid
        kern = functools.partial(_mm_resid_kernel, gate_idx=gate_idx)
        in_specs = [x_spec, w_spec, o_spec,
                    pl.BlockSpec((1, N_MOD, tn), lambda i, j: (seg.of_row(i * tm), 0, j))]
        args = (x, w, h, mods)
        out_dtype = F32
    elif qkv is not None:
        cos, sin, qn, kn, qk_norm = qkv
        kern = functools.partial(_mm_qkv_kernel, nq_tiles=N_HEADS * HEAD_DIM // tn,
                                 nk_tiles=N_KV_HEADS * HEAD_DIM // tn, qk_norm=qk_norm)
        tab_spec = pl.BlockSpec((tm, HEAD_DIM), lambda i, j: (i, 0))
        vec_spec = pl.BlockSpec((1, HEAD_DIM), lambda i, j: (0, 0))
        in_specs = [x_spec, w_spec, tab_spec, tab_spec, vec_spec, vec_spec]
        args = (x, w, cos, sin, qn.reshape(1, HEAD_DIM), kn.reshape(1, HEAD_DIM))
    else:
        kern = functools.partial(_mm_plain_kernel, scale_tiles=scale_tiles, scale=scale)
        in_specs = [x_spec, w_spec]
        args = (x, w)
    return pl.pallas_call(
        kern, grid=grid, in_specs=in_specs, out_specs=o_spec,
        out_shape=jax.ShapeDtypeStruct((m, n), out_dtype),
        compiler_params=_cparams(2),
        name="matmul",
    )(*args)


def _attn_kernel(*refs, tq, lat_keys, window, has_sink):
    it = iter(refs)
    sink_ref = next(it) if has_sink else None
    q_ref = next(it)
    kc_ref, vc_ref = next(it), next(it)
    kl_ref, vl_ref = (next(it), next(it)) if lat_keys else (None, None)
    o_ref = refs[-1]
    kvh = pl.program_id(1)
    qi = pl.program_id(2)
    nt = (((1,), (1,)), ((), ()))
    kc = kc_ref[...]
    vc = _with_ones(vc_ref[...])
    if lat_keys and window:
        seq = kl_ref.shape[0]
        band = min(tq + 2 * WINDOW, seq)
        start = pl.multiple_of(jnp.clip(qi * tq - WINDOW, 0, seq - band), LANES)
        kl = kl_ref[pl.ds(start, band), :]
        vl = _with_ones(vl_ref[pl.ds(start, band), :])
        q_pos = qi * tq + lax.broadcasted_iota(jnp.int32, (tq, band), 0)
        k_pos = start + lax.broadcasted_iota(jnp.int32, (tq, band), 1)
        in_window = jnp.abs(q_pos - k_pos) <= WINDOW
    elif lat_keys:
        kl = kl_ref[...]
        vl = _with_ones(vl_ref[...])
    def scores(g):
        qg = q_ref[:, g * HEAD_DIM:(g + 1) * HEAD_DIM]
        s_c = lax.dot_general(qg, kc, nt, preferred_element_type=F32)
        if not lat_keys:
            return s_c, None
        s_l = lax.dot_general(qg, kl, nt, preferred_element_type=F32)
        return s_c, jnp.where(in_window, s_l, NEG_INF) if window else s_l

    nxt = scores(0)
    for g in range(KV_GROUP):
        s_c, s_l = nxt
        if g + 1 < KV_GROUP:
            nxt = scores(g + 1)
        m = jnp.max(s_c, axis=-1, keepdims=True)
        if lat_keys:
            m = jnp.maximum(m, jnp.max(s_l, axis=-1, keepdims=True))
        if has_sink:
            sink = sink_ref[kvh * KV_GROUP + g]
            m = jnp.maximum(m, sink)
        acc = jnp.dot(jnp.exp((s_c - m).astype(BF16)), vc, preferred_element_type=F32)
        if lat_keys:
            acc = acc + jnp.dot(jnp.exp((s_l - m).astype(BF16)), vl, preferred_element_type=F32)
        den = acc[:, HEAD_DIM:HEAD_DIM + 1]
        if has_sink:
            den = den + jnp.exp(sink - m)
        o_ref[:, g * HEAD_DIM:(g + 1) * HEAD_DIM] = (acc[:, :HEAD_DIM] / den).astype(o_ref.dtype)


def _with_ones(v):
    return jnp.concatenate([v, jnp.ones_like(v)], axis=1)


def attention(qkv, dims, *, lat_queries, window, sink, out=None, out_rows=None):
    bsz, seq, n_ctx = dims
    n_lat = bsz * seq
    g_cols = KV_GROUP * HEAD_DIM
    k_col0 = N_HEADS * HEAD_DIM // HEAD_DIM
    v_col0 = k_col0 + N_KV_HEADS
    has_sink = sink is not None
    if lat_queries:
        tq = _pick(seq, 512)
        q_tiles = seq // tq
        q_row0 = 0
    else:
        tq = n_ctx
        q_tiles = 1
        q_row0 = n_lat // tq
    aliases = {}
    if out is not None:
        out_rows = out.shape[0]
    ctx_blk0 = n_lat // n_ctx
    in_specs = []
    args = []
    if has_sink:
        in_specs.append(pl.BlockSpec(memory_space=pltpu.SMEM))
        args.append(sink.astype(F32))
    in_specs += [
        pl.BlockSpec((tq, g_cols), lambda b, h, i: (q_row0 + b * q_tiles + i, h)),
        pl.BlockSpec((n_ctx, HEAD_DIM), lambda b, h, i: (ctx_blk0 + b, k_col0 + h)),
        pl.BlockSpec((n_ctx, HEAD_DIM), lambda b, h, i: (ctx_blk0 + b, v_col0 + h)),
    ]
    args += [qkv, qkv, qkv]
    if lat_queries:
        in_specs += [
            pl.BlockSpec((seq, HEAD_DIM), lambda b, h, i: (b, k_col0 + h)),
            pl.BlockSpec((seq, HEAD_DIM), lambda b, h, i: (b, v_col0 + h)),
        ]
        args += [qkv, qkv]
    if out is not None:
        aliases = {len(args): 0}
        in_specs.append(pl.BlockSpec(memory_space=pl.ANY))
        args.append(out)
    return pl.pallas_call(
        functools.partial(_attn_kernel, tq=tq, lat_keys=lat_queries, window=window, has_sink=has_sink),
        grid=(bsz, N_KV_HEADS, q_tiles),
        in_specs=in_specs,
        out_specs=pl.BlockSpec((tq, g_cols), lambda b, h, i: (q_row0 + b * q_tiles + i, h)),
        out_shape=jax.ShapeDtypeStruct((out_rows, N_HEADS * HEAD_DIM), BF16),
        input_output_aliases=aliases,
        compiler_params=_cparams(3),
        name="attention",
    )(*args)


def _mlstm_gate_kernel(x_ref, w_ref, b_ref, o_ref):
    g = jnp.dot(x_ref[...], w_ref[...], preferred_element_type=F32) + b_ref[...]
    g = GATE_CAP * jnp.tanh(g / GATE_CAP)
    log_f = jnp.minimum(g, 0.0) - jnp.log(1.0 + jnp.exp(-jnp.abs(g)))
    col = lax.broadcasted_iota(jnp.int32, g.shape, 1)
    is_f = (col // M_HEADS) % 2 == 1
    o_ref[...] = jnp.transpose(jnp.where(is_f, log_f, g))


def mlstm_gates(a, w_gate_pad, gate_b_pad):
    n, k = a.shape
    tm = _pick(n, 1024)
    return pl.pallas_call(
        _mlstm_gate_kernel,
        grid=(n // tm,),
        in_specs=[pl.BlockSpec((tm, k), lambda i: (i, 0)),
                  pl.BlockSpec((k, LANES), lambda i: (0, 0)),
                  pl.BlockSpec((1, LANES), lambda i: (0, 0))],
        out_specs=pl.BlockSpec((LANES, tm), lambda i: (0, i)),
        out_shape=jax.ShapeDtypeStruct((LANES, n), F32),
        compiler_params=_cparams(1),
        name="mlstm_gates",
    )(a, w_gate_pad, gate_b_pad)


def _mlstm_kernel(*refs):
    ct_ref, n_ref, m_ref = refs[-3:]
    c = pl.program_id(1)
    L = M_CHUNK

    @pl.when(c == 0)
    def _():
        ct_ref[...] = jnp.zeros_like(ct_ref)
        n_ref[...] = jnp.zeros_like(n_ref)
        m_ref[...] = jnp.zeros_like(m_ref)

    nt = (((1,), (1,)), ((), ()))
    tn = (((0,), (0,)), ((), ()))
    for d in range(2):
        q_ref, k_ref, v_ref, gi_ref, gf_ref = refs[5 * d:5 * d + 5]
        o_ref = refs[10 + d]
        diff = lax.broadcasted_iota(jnp.int32, (L, L), 0) - lax.broadcasted_iota(jnp.int32, (L, L), 1)
        if d == 1:
            diff = -diff
        mask = diff >= 0
        mask_t = diff <= 0
        eye = diff == 0
        def early(h):
            st = d * M_HEADS + h
            qc = q_ref[:, h * M_QK_DIM:(h + 1) * M_QK_DIM]
            kc = k_ref[:, h * M_QK_DIM:(h + 1) * M_QK_DIM]
            i_row = gi_ref[h:h + 1, :]
            f_row = gf_ref[h:h + 1, :]
            f_col = jnp.sum(jnp.where(eye, f_row, 0.0), axis=1, keepdims=True)
            i_col = jnp.sum(jnp.where(eye, i_row, 0.0), axis=1, keepdims=True)
            b_col = jnp.sum(jnp.where(mask, f_row, 0.0), axis=1, keepdims=True)
            b_row = jnp.sum(jnp.where(mask_t, f_col, 0.0), axis=0, keepdims=True)
            log_intra = jnp.where(mask, b_col - b_row + i_row, NEG_INF)
            qk = lax.dot_general(qc, kc, nt, preferred_element_type=F32)
            inter = jnp.dot(qc, ct_ref[st].astype(BF16), preferred_element_type=F32)
            return qc, kc, f_row, i_col, b_col, log_intra, qk, inter

        nxt = early(0)
        for h in range(M_HEADS):
            st = d * M_HEADS + h
            qc, kc, f_row, i_col, b_col, log_intra, qk, inter = nxt
            if h + 1 < M_HEADS:
                nxt = early(h + 1)
            vc = v_ref[:, h * M_V_DIM:(h + 1) * M_V_DIM]
            m_prev = m_ref[st, :, 0:1]
            n_prev = n_ref[st]
            log_inter = b_col + m_prev
            m_t = jnp.maximum(log_inter, jnp.max(log_intra, axis=1, keepdims=True))
            w_inter = jnp.exp(log_inter - m_t)
            s_qk = qk * jnp.exp(log_intra - m_t)
            num = jnp.dot(s_qk.astype(BF16), vc, preferred_element_type=F32) + w_inter * inter
            den = (jnp.sum(s_qk, axis=1, keepdims=True)
                   + w_inter * jnp.sum(qc.astype(F32) * n_prev, axis=1, keepdims=True))
            h_out = num / jnp.maximum(jnp.abs(den), jnp.exp(-m_t))
            o_ref[:, h * M_V_DIM:(h + 1) * M_V_DIM] = h_out.astype(o_ref.dtype)
            b_tot = jnp.sum(f_row, axis=1, keepdims=True)
            log_w = b_tot - b_col + i_col
            m_new = jnp.maximum(b_tot + m_prev, jnp.max(log_w, axis=0, keepdims=True))
            decay = jnp.exp(b_tot + m_prev - m_new)
            w_k = jnp.exp(log_w - m_new)
            vw = (vc.astype(F32) * w_k).astype(BF16)
            ct_ref[st] = decay * ct_ref[st] + lax.dot_general(kc, vw, tn, preferred_element_type=F32)
            n_ref[st] = decay * n_prev + jnp.sum(kc.astype(F32) * w_k, axis=0, keepdims=True)
            m_ref[st] = jnp.broadcast_to(m_new, (1, LANES))


def mlstm_scan(proj, gates_t, dims):
    bsz, seq, n_ctx = dims
    nt_rows = proj.shape[0]
    L = M_CHUNK
    ncc, nlc = n_ctx // L, seq // L
    lat_blocks = bsz * nlc
    qk_w = M_HEADS * M_QK_DIM

    def row_blk(b, d, c):
        cc = c if d == 0 else ncc - 1 - c
        lc = c - ncc if d == 0 else nlc - 1 - (c - ncc)
        return jnp.where(c < ncc, lat_blocks + b * ncc + cc, b * nlc + lc)

    def dir_specs(d):
        return [
            pl.BlockSpec((L, qk_w), lambda b, c: (row_blk(b, d, c), 0)),
            pl.BlockSpec((L, qk_w), lambda b, c: (row_blk(b, d, c), 1)),
            pl.BlockSpec((L, D_MODEL), lambda b, c: (row_blk(b, d, c), 1)),
            pl.BlockSpec((M_HEADS, L), lambda b, c: (2 * d, row_blk(b, d, c))),
            pl.BlockSpec((M_HEADS, L), lambda b, c: (2 * d + 1, row_blk(b, d, c))),
        ]

    return pl.pallas_call(
        _mlstm_kernel,
        grid=(bsz, ncc + nlc),
        in_specs=dir_specs(0) + dir_specs(1),
        out_specs=[pl.BlockSpec((L, D_MODEL), lambda b, c: (row_blk(b, 0, c), 0)),
                   pl.BlockSpec((L, D_MODEL), lambda b, c: (row_blk(b, 1, c), 0))],
        out_shape=[jax.ShapeDtypeStruct((nt_rows, D_MODEL), BF16)] * 2,
        scratch_shapes=[pltpu.VMEM((2 * M_HEADS, M_QK_DIM, M_V_DIM), F32),
                        pltpu.VMEM((2 * M_HEADS, 1, M_QK_DIM), F32),
                        pltpu.VMEM((2 * M_HEADS, 1, LANES), F32)],
        compiler_params=_cparams(2),
        name="mlstm_scan",
    )(*([proj, proj, proj, gates_t, gates_t] * 2))


def _mlstm_finish_kernel(hf_ref, hb_ref, og_ref, g_ref, o_ref):
    for h in range(M_HEADS):
        sl = slice(h * M_V_DIM, (h + 1) * M_V_DIM)
        x = hf_ref[:, sl].astype(F32) + hb_ref[:, sl].astype(F32)
        y = x * lax.rsqrt(jnp.mean(x * x, axis=-1, keepdims=True) + RMS_EPS) * g_ref[:, sl]
        o_ref[:, sl] = (y * jax.nn.sigmoid(og_ref[:, sl].astype(F32))).astype(o_ref.dtype)


def mlstm_finish(h_fwd, h_bwd, proj, head_norm):
    n, d = h_fwd.shape
    tm = _pick(n, 256)
    row_spec = pl.BlockSpec((tm, d), lambda i: (i, 0))
    return pl.pallas_call(
        _mlstm_finish_kernel,
        grid=(n // tm,),
        in_specs=[row_spec, row_spec,
                  pl.BlockSpec((tm, d), lambda i: (i, 2)),
                  pl.BlockSpec((1, d), lambda i: (0, 0))],
        out_specs=row_spec,
        out_shape=jax.ShapeDtypeStruct((n, d), BF16),
        compiler_params=_cparams(1),
        name="mlstm_finish",
    )(h_fwd, h_bwd, proj, head_norm.reshape(1, d))


def _pack_rows(x):
    half = x.shape[1] // 2
    return _pack_pair(x[:, :half], x[:, half:])


def _pack_pair(lo, hi):
    lo = lax.bitcast_convert_type(lo.astype(BF16).astype(F32), U32)
    hi = lax.bitcast_convert_type(hi.astype(BF16).astype(F32), U32)
    return (hi & jnp.uint32(0xFFFF0000)) | (lo >> 16)


def _unpack_rows(p):
    lo = lax.bitcast_convert_type(p << 16, F32)
    hi = lax.bitcast_convert_type(p & jnp.uint32(0xFFFF0000), F32)
    return lo, hi


def _route(a, w_t, bias, counts):
    tm = a.shape[0]
    per = N_EXPERTS // N_GROUPS
    nt = (((1,), (1,)), ((), ()))
    w_hi = w_t.astype(BF16)
    w_lo = (w_t - w_hi.astype(F32)).astype(BF16)
    a_hi = a.astype(BF16)
    a_lo = (a - a_hi.astype(F32)).astype(BF16)
    logits = (lax.dot_general(w_hi, a_hi, nt, preferred_element_type=F32)
              + lax.dot_general(w_lo, a_hi, nt, preferred_element_type=F32)
              + lax.dot_general(w_hi, a_lo, nt, preferred_element_type=F32))
    scores = jax.nn.sigmoid(logits).reshape(N_GROUPS, per, tm)
    biased = scores + bias.reshape(N_GROUPS, per, 1)
    e_iota = lax.broadcasted_iota(jnp.int32, (N_GROUPS, per, tm), 1).astype(F32)
    g_iota = lax.broadcasted_iota(jnp.int32, (N_GROUPS, 1, tm), 0).astype(F32)
    lin_iota = lax.broadcasted_iota(jnp.int32, (N_GROUPS, per, tm), 0).astype(F32) * per + e_iota
    m1 = jnp.max(biased, axis=1, keepdims=True)
    i1 = jnp.min(jnp.where(biased == m1, e_iota, float(per)), axis=1, keepdims=True)
    m2 = jnp.max(jnp.where(e_iota == i1, NEG_INF, biased), axis=1, keepdims=True)
    gscore = m1 + m2
    gsel = jnp.zeros(gscore.shape, F32)
    for _ in range(TOPK_GROUPS):
        cur = jnp.where(gsel > 0.0, NEG_INF, gscore)
        gm = jnp.max(cur, axis=0, keepdims=True)
        gi = jnp.min(jnp.where(cur == gm, g_iota, float(N_GROUPS)), axis=0, keepdims=True)
        gsel = jnp.where(g_iota == gi, 1.0, gsel)
    cand = jnp.where(gsel > 0.0, biased, NEG_INF)
    sel = jnp.zeros(cand.shape, F32)
    picks = []
    for _ in range(TOP_K):
        cur = jnp.where(sel > 0.0, NEG_INF, cand)
        em = jnp.max(jnp.max(cur, axis=1, keepdims=True), axis=0, keepdims=True)
        hit = jnp.where(cur == em, lin_iota, float(N_EXPERTS))
        ei = jnp.min(jnp.min(hit, axis=1, keepdims=True), axis=0, keepdims=True)
        sel = jnp.where(lin_iota == ei, 1.0, sel)
        picks.append(ei)
    sel2 = sel.reshape(N_EXPERTS, tm)
    before = lax.broadcasted_iota(jnp.int32, (tm, tm), 0) < lax.broadcasted_iota(jnp.int32, (tm, tm), 1)
    rank = jnp.dot(sel2.astype(BF16), jnp.where(before, 1.0, 0.0).astype(BF16),
                   preferred_element_type=F32) + counts
    new_counts = counts + jnp.sum(sel2, axis=1, keepdims=True)
    rank3 = rank.reshape(N_GROUPS, per, tm)
    wsum = jnp.sum(jnp.sum(jnp.where(sel > 0.0, scores, 0.0), axis=1, keepdims=True), axis=0, keepdims=True)

    def pick(ei, table):
        v = jnp.where(lin_iota == ei, table, 0.0)
        return jnp.sum(jnp.sum(v, axis=1, keepdims=True), axis=0, keepdims=True).reshape(1, tm)

    ids = jnp.concatenate([ei.reshape(1, tm) for ei in picks], axis=0).astype(I32)
    ranks = jnp.concatenate([pick(ei, rank3) for ei in picks], axis=0).astype(I32)
    w = jnp.concatenate([pick(ei, scores) for ei in picks], axis=0) / wsum.reshape(1, tm) * ROUTED_SCALE
    wmat = jnp.transpose(jnp.concatenate([w, jnp.zeros((LANES - TOP_K, tm), F32)], axis=0))
    return ids, ranks, wmat, new_counts


def _dest_kernel(starts_ref, ids_ref, rank_ref, dest_ref):
    ids = ids_ref[...]
    acc = rank_ref[...]
    for e in range(N_EXPERTS):
        acc = acc + jnp.where(ids == e, starts_ref[e], 0)
    dest_ref[...] = acc


def dest_rows(starts, ids, ranks):
    return pl.pallas_call(
        _dest_kernel,
        in_specs=[pl.BlockSpec(memory_space=pltpu.SMEM), pl.BlockSpec(memory_space=pltpu.VMEM),
                  pl.BlockSpec(memory_space=pltpu.VMEM)],
        out_specs=pl.BlockSpec(memory_space=pltpu.VMEM),
        out_shape=jax.ShapeDtypeStruct(ids.shape, I32),
        compiler_params=pltpu.CompilerParams(vmem_limit_bytes=VMEM_LIMIT),
        name="dest_rows",
    )(starts, ids, ranks)


def _dispatch_kernel(starts_ref, ends_ref, dest_ref, f_ref, gu_ref, dn_ref, xs_ref, sh_ref,
                     zero_ref, gu_bf, dn_bf, sem):
    tm = f_ref.shape[0]

    def zero_copy(e):
        row0 = pl.multiple_of(ends_ref[e] - MOE_TM, MOE_TM)
        return pltpu.make_async_copy(zero_ref, xs_ref.at[pl.ds(row0, MOE_TM)], sem)

    @pl.when(pl.program_id(0) == 0)
    def _():
        gu_bf[...] = gu_ref[...].astype(BF16)
        dn_bf[...] = dn_ref[...].astype(BF16)
        zero_ref[...] = jnp.zeros_like(zero_ref)
        for e in range(N_EXPERTS):
            @pl.when(ends_ref[e] > starts_ref[e])
            def _():
                zero_copy(e).start()
        for e in range(N_EXPERTS):
            @pl.when(ends_ref[e] > starts_ref[e])
            def _():
                zero_copy(e).wait()

    def row_copy(t, d):
        return pltpu.make_async_copy(f_ref.at[pl.ds(t, 1)], xs_ref.at[pl.ds(d, 1)], sem)

    def issue(t, c):
        for k in range(TOP_K):
            row_copy(t, dest_ref[k, t]).start()
        return c

    def drain(t, c):
        for k in range(TOP_K):
            row_copy(0, 0).wait()
        return c

    lax.fori_loop(0, tm, issue, 0)
    _swiglu_expert(f_ref, gu_bf, dn_bf, sh_ref, pack_out=False)
    lax.fori_loop(0, tm, drain, 0)


def dispatch(starts, ends, dest, packed, n_rows, shared_gu, shared_down):
    n, half = packed.shape
    tm = _pick(n, 256)

    def whole(shape):
        return pl.BlockSpec(shape, lambda i, s, e: (0,) * len(shape))

    return pl.pallas_call(
        _dispatch_kernel,
        grid_spec=pltpu.PrefetchScalarGridSpec(
            num_scalar_prefetch=2, grid=(n // tm,),
            in_specs=[pl.BlockSpec((TOP_K, tm), lambda i, s, e: (0, i), memory_space=pltpu.SMEM),
                      pl.BlockSpec((tm, half), lambda i, s, e: (i, 0)),
                      whole(shared_gu.shape), whole(shared_down.shape)],
            out_specs=[pl.BlockSpec(memory_space=pl.ANY),
                       pl.BlockSpec((tm, 2 * half), lambda i, s, e: (i, 0))],
            scratch_shapes=[pltpu.VMEM((MOE_TM, half), U32), pltpu.VMEM(shared_gu.shape, BF16),
                            pltpu.VMEM(shared_down.shape, BF16), pltpu.SemaphoreType.DMA(())]),
        out_shape=[jax.ShapeDtypeStruct((n_rows, half), U32), jax.ShapeDtypeStruct((n, 2 * half), BF16)],
        compiler_params=_cparams(1),
        name="dispatch",
    )(starts, ends, dest, packed, shared_gu, shared_down)


def _swiglu_expert(x_ref, gu_bf, dn_bf, o_ref, pack_out):
    half = x_ref.shape[1]
    chunks = range(0, half, MOE_CHUNK)

    def unpack(c):
        lo, hi = _unpack_rows(x_ref[:, c:c + MOE_CHUNK])
        return lo.astype(BF16), hi.astype(BF16)

    def down(c):
        return (jnp.dot(act, dn_bf[:, c:c + MOE_CHUNK], preferred_element_type=F32),
                jnp.dot(act, dn_bf[:, half + c:half + c + MOE_CHUNK], preferred_element_type=F32))

    gu = None
    nxt = unpack(0)
    for c in chunks:
        lo, hi = nxt
        if c + MOE_CHUNK < half:
            nxt = unpack(c + MOE_CHUNK)
        part = (jnp.dot(lo, gu_bf[c:c + MOE_CHUNK, :], preferred_element_type=F32)
                + jnp.dot(hi, gu_bf[half + c:half + c + MOE_CHUNK, :], preferred_element_type=F32))
        gu = part if gu is None else gu + part
    g = gu[:, :D_EXPERT]
    act = (g * jax.nn.sigmoid(g) * gu[:, D_EXPERT:]).astype(BF16)
    nxt = down(0)
    for c in chunks:
        y_lo, y_hi = nxt
        if c + MOE_CHUNK < half:
            nxt = down(c + MOE_CHUNK)
        if pack_out:
            o_ref[:, c:c + MOE_CHUNK] = _pack_pair(y_lo, y_hi)
        else:
            o_ref[:, c:c + MOE_CHUNK] = y_lo.astype(o_ref.dtype)
            o_ref[:, half + c:half + c + MOE_CHUNK] = y_hi.astype(o_ref.dtype)


def _grouped_kernel(te_ref, nt_ref, x_ref, gu_ref, dn_ref, o_ref, gu_bf, dn_bf):
    j = pl.program_id(0)

    @pl.when(j < nt_ref[0])
    def _():
        @pl.when(jnp.logical_or(j == 0, te_ref[j] != te_ref[jnp.maximum(j - 1, 0)]))
        def _():
            gu_bf[...] = gu_ref[0].astype(BF16)
            dn_bf[...] = dn_ref[0].astype(BF16)

        _swiglu_expert(x_ref, gu_bf, dn_bf, o_ref, pack_out=True)


def grouped_experts(tile_expert, n_tiles, xs, exp_gu, exp_down):
    rows, half = xs.shape
    d = 2 * half

    def tile(j, nt):
        return jnp.minimum(j, nt[0] - 1)

    return pl.pallas_call(
        _grouped_kernel,
        grid_spec=pltpu.PrefetchScalarGridSpec(
            num_scalar_prefetch=2, grid=(rows // MOE_TM,),
            in_specs=[pl.BlockSpec((MOE_TM, half), lambda j, te, nt: (tile(j, nt), 0)),
                      pl.BlockSpec((1, d, 2 * D_EXPERT), lambda j, te, nt: (te[tile(j, nt)], 0, 0)),
                      pl.BlockSpec((1, D_EXPERT, d), lambda j, te, nt: (te[tile(j, nt)], 0, 0))],
            out_specs=pl.BlockSpec((MOE_TM, half), lambda j, te, nt: (tile(j, nt), 0)),
            scratch_shapes=[pltpu.VMEM((d, 2 * D_EXPERT), BF16), pltpu.VMEM((D_EXPERT, d), BF16)]),
        out_shape=jax.ShapeDtypeStruct((rows, half), U32),
        compiler_params=_cparams(1),
        name="grouped_experts",
    )(tile_expert, n_tiles, xs, exp_gu, exp_down)


def _gather_combine(dest_ref, next_dest_ref, ys_ref, w_ref, sh_ref, buf_ref, sems):
    tm, d = sh_ref.shape
    half = d // 2
    step = pl.program_id(0)
    slot = step % 2

    def row_copy(s, k, t, src):
        return pltpu.make_async_copy(ys_ref.at[pl.ds(src, 1)], buf_ref.at[s, k, pl.ds(t, 1)], sems.at[s])

    def request(idx_ref, s):
        def body(t, c):
            for k in range(TOP_K):
                row_copy(s, k, t, idx_ref[k, t]).start()
            return c
        lax.fori_loop(0, tm, body, 0)

    @pl.when(step == 0)
    def _():
        request(dest_ref, 0)

    @pl.when(step + 1 < pl.num_programs(0))
    def _():
        request(next_dest_ref, 1 - slot)

    def drain(t, c):
        for k in range(TOP_K):
            row_copy(slot, 0, 0, 0).wait()
        return c

    lax.fori_loop(0, tm, drain, 0)
    w = w_ref[...]
    y_lo = sh_ref[:, :half].astype(F32)
    y_hi = sh_ref[:, half:].astype(F32)
    for k in range(TOP_K):
        lo, hi = _unpack_rows(buf_ref[slot, k])
        y_lo = y_lo + w[:, k:k + 1] * lo
        y_hi = y_hi + w[:, k:k + 1] * hi
    return jnp.concatenate([y_lo, y_hi], axis=1)


class MoeOut(NamedTuple):
    dest: jax.Array
    ys: jax.Array
    wmat: jax.Array
    shared: jax.Array


def moe_experts(packed, ids, ranks, wmat, counts, moe):
    _, _, exp_gu, exp_down, shared_gu, shared_down = moe
    n = packed.shape[0]
    cnt = counts[:, 0].astype(I32)
    padded = (cnt + MOE_TM - 1) // MOE_TM * MOE_TM
    ends = jnp.cumsum(padded)
    starts = ends - padded
    n_max = n * TOP_K // MOE_TM + N_EXPERTS
    n_tiles = (ends[-1] // MOE_TM).reshape(1)
    tile_row0 = jnp.arange(n_max, dtype=I32) * MOE_TM
    tile_expert = jnp.minimum(jnp.sum(ends[None, :] <= tile_row0[:, None], axis=1), N_EXPERTS - 1).astype(I32)
    dest = dest_rows(starts, ids, ranks)
    xs, shared = dispatch(starts, ends, dest, packed, n_max * MOE_TM, shared_gu, shared_down)
    ys = grouped_experts(tile_expert, n_tiles, xs, exp_gu, exp_down)
    return MoeOut(dest, ys, wmat, shared)


def _rope_tables(seq, n_rows):
    pos = jnp.arange(seq)
    row = (pos // GRID_W).astype(F32)
    col = (pos % GRID_W).astype(F32)
    n_freq = HEAD_DIM // 4
    inv_freq = ROPE_THETA ** (-jnp.arange(n_freq, dtype=F32) / n_freq)
    ang = jnp.concatenate([row[:, None] * inv_freq, col[:, None] * inv_freq], axis=-1)
    cos, sin = jnp.cos(ang), jnp.sin(ang)
    cos_full = jnp.concatenate([cos, cos], axis=-1)
    sin_full = jnp.concatenate([-sin, sin], axis=-1)
    return cos_full, sin_full


def _trunk(x, c, ctx, c_ctx, layers, final_norm):
    bsz, seq, d = x.shape
    n_ctx = ctx.shape[1]
    dims = (bsz, seq, n_ctx)
    n_lat = bsz * seq
    n_all = n_lat + bsz * n_ctx
    depth = len(layers)

    seg = _Segments(seq, bsz, bsz * n_ctx)

    h = jnp.concatenate([x.reshape(n_lat, d), ctx.reshape(bsz * n_ctx, d)], axis=0)
    mod_rows = -(-(bsz + 1) // 8) * 8
    cond = jnp.zeros((mod_rows, d), F32).at[:bsz].set(c).at[bsz].set(c_ctx)

    cos1, sin1 = _rope_tables(seq, n_all)
    cos_t = jnp.concatenate([jnp.tile(cos1, (bsz, 1)), jnp.ones((bsz * n_ctx, HEAD_DIM), F32)], axis=0)
    sin_t = jnp.concatenate([jnp.tile(sin1, (bsz, 1)), jnp.zeros((bsz * n_ctx, HEAD_DIM), F32)], axis=0)
    ones_hd = jnp.ones((HEAD_DIM,), F32)

    tables = [ada_table(cond, *layer[1]) for layer in layers]
    a, = rowwise(h, layers[0][2], seg, shift=(tables[0], 0))
    for li, (kind, ada, norm1, mixer, norm2, moe) in enumerate(layers):
        need_ctx = li < depth - 1
        rows_out = n_all if need_ctx else n_lat
        mods = tables[li]
        resid = (h, mods, 2, seg)
        if kind == "mlstm":
            w_in, gate_b, head_norm, w_o = mixer
            n_main = 2 * M_HEADS * M_QK_DIM + M_HEADS * M_V_DIM + D_MODEL
            proj = matmul(a, w_in[:, :n_main].astype(BF16), scale_tiles=M_HEADS * M_QK_DIM // 512,
                          scale=M_QK_DIM ** -0.5)
            n_gate = 4 * M_HEADS
            w_gate = jnp.zeros((d, LANES), BF16).at[:, :n_gate].set(w_in[:, n_main:].astype(BF16))
            gate_b_pad = jnp.zeros((1, LANES), F32).at[0, :n_gate].set(gate_b.astype(F32))
            gates_t = mlstm_gates(a, w_gate, gate_b_pad)
            h_fwd, h_bwd = mlstm_scan(proj, gates_t, dims)
            mixed = mlstm_finish(h_fwd, h_bwd, proj, head_norm)
            h = matmul(mixed, w_o.astype(BF16), resid=resid, rows=rows_out)
        else:
            if kind == "global":
                w_qkv, q_norm, k_norm, w_o = mixer
                qkv = matmul(a, w_qkv.astype(BF16), qkv=(cos_t, sin_t, q_norm, k_norm, True))
                sink = None
            else:
                w_qkv, sink, w_o = mixer
                qkv = matmul(a, w_qkv.astype(BF16), qkv=(cos_t, sin_t, ones_hd, ones_hd, False))
            mixed = attention(qkv, dims, lat_queries=True, window=kind == "swa", sink=sink, out_rows=rows_out)
            if need_ctx:
                mixed = attention(qkv, dims, lat_queries=False, window=False, sink=sink, out=mixed)
            h = matmul(mixed, w_o.astype(BF16), resid=resid, rows=rows_out)
        router_w, router_b, exp_gu, exp_down, shared_gu, shared_down = moe
        routed = rowwise(h, norm2, seg, shift=(mods, 3), rows=rows_out, route=(router_w.T, router_b))
        y = moe_experts(*routed, moe)
        if li + 1 < depth:
            h, a = rowwise(h, layers[li + 1][2], seg, resid=(y, mods, 5), shift=(tables[li + 1], 0), rows=rows_out)
        else:
            _h, out = rowwise(h, final_norm, seg, resid=(y, mods, 5), rows=rows_out, out_dtype=F32)
    return out.reshape(bsz, seq, d)


def kernel(x, c, ctx, c_ctx, l0_ada_down, l0_ada_up, l0_ada_b, l0_norm1, l0_attn_qkv, l0_q_norm, l0_k_norm, l0_attn_o, l0_norm2, l0_router_w, l0_router_b, l0_exp_gu, l0_exp_down, l0_shared_gu, l0_shared_down, l1_ada_down, l1_ada_up, l1_ada_b, l1_norm1, l1_mlstm_in, l1_mlstm_gate_b, l1_mlstm_head_norm, l1_mlstm_o, l1_norm2, l1_router_w, l1_router_b, l1_exp_gu, l1_exp_down, l1_shared_gu, l1_shared_down, l2_ada_down, l2_ada_up, l2_ada_b, l2_norm1, l2_swa_qkv, l2_swa_sink, l2_swa_o, l2_norm2, l2_router_w, l2_router_b, l2_exp_gu, l2_exp_down, l2_shared_gu, l2_shared_down, l3_ada_down, l3_ada_up, l3_ada_b, l3_norm1, l3_attn_qkv, l3_q_norm, l3_k_norm, l3_attn_o, l3_norm2, l3_router_w, l3_router_b, l3_exp_gu, l3_exp_down, l3_shared_gu, l3_shared_down, final_norm):
    layers = [
        ("global", (l0_ada_down, l0_ada_up, l0_ada_b), l0_norm1, (l0_attn_qkv, l0_q_norm, l0_k_norm, l0_attn_o), l0_norm2,
         (l0_router_w, l0_router_b, l0_exp_gu, l0_exp_down, l0_shared_gu, l0_shared_down)),
        ("mlstm", (l1_ada_down, l1_ada_up, l1_ada_b), l1_norm1, (l1_mlstm_in, l1_mlstm_gate_b, l1_mlstm_head_norm, l1_mlstm_o), l1_norm2,
         (l1_router_w, l1_router_b, l1_exp_gu, l1_exp_down, l1_shared_gu, l1_shared_down)),
        ("swa", (l2_ada_down, l2_ada_up, l2_ada_b), l2_norm1, (l2_swa_qkv, l2_swa_sink, l2_swa_o), l2_norm2,
         (l2_router_w, l2_router_b, l2_exp_gu, l2_exp_down, l2_shared_gu, l2_shared_down)),
        ("global", (l3_ada_down, l3_ada_up, l3_ada_b), l3_norm1, (l3_attn_qkv, l3_q_norm, l3_k_norm, l3_attn_o), l3_norm2,
         (l3_router_w, l3_router_b, l3_exp_gu, l3_exp_down, l3_shared_gu, l3_shared_down)),
    ]
    return _trunk(x, c, ctx, c_ctx, layers, final_norm)
```

```python
import functools
import math
from typing import NamedTuple

import jax
import jax.numpy as jnp
from jax import lax
from jax.experimental import pallas as pl
from jax.experimental.pallas import tpu as pltpu

F32 = jnp.float32
BF16 = jnp.bfloat16
U32 = jnp.uint32
I32 = jnp.int32

D_MODEL = 4096
GRID_W = 64
RMS_EPS = 1e-6
N_MOD = 6
N_HEADS = 32
N_KV_HEADS = 8
HEAD_DIM = D_MODEL // N_HEADS
KV_GROUP = N_HEADS // N_KV_HEADS
ROPE_THETA = 10000.0
WINDOW = 128
M_HEADS = 8
M_V_DIM = D_MODEL // M_HEADS
M_QK_DIM = M_V_DIM // 2
M_CHUNK = 128
GATE_CAP = 15.0
N_EXPERTS = 64
TOP_K = 8
N_GROUPS = 8
TOPK_GROUPS = 4
D_EXPERT = 192
ROUTED_SCALE = 2.5

MOE_TM = 512
MOE_CHUNK = 512
QKV_ROWS = 256
LANES = 128
VMEM_LIMIT = 56 * 1024 * 1024
NEG_INF = float("-inf")


def _cparams(n_axes):
    return pltpu.CompilerParams(dimension_semantics=("arbitrary",) * n_axes,
                                vmem_limit_bytes=VMEM_LIMIT)


def _pick(n, pref):
    t = pref
    while n % t:
        t //= 2
    return t


class _Segments(NamedTuple):
    seq: int
    bsz: int
    n_ctx_rows: int

    @property
    def tile_unit(self):
        return math.gcd(self.seq, self.n_ctx_rows)

    def of_row(self, row):
        return jnp.minimum(row // self.seq, self.bsz)


def _ada_kernel(cond_ref, down_ref, up_ref, b_ref, out_ref, t_ref):
    @pl.when(pl.program_id(0) == 0)
    def _():
        c = cond_ref[...]
        t_ref[...] = jnp.dot(c * jax.nn.sigmoid(c), down_ref[...], precision=lax.Precision.HIGHEST,
                             preferred_element_type=F32)

    out_ref[...] = jnp.dot(t_ref[...], up_ref[...], precision=lax.Precision.HIGHEST,
                           preferred_element_type=F32) + b_ref[...]


def ada_table(cond_pad, down, up, bias):
    r, d = cond_pad.shape
    rank = down.shape[1]
    n = up.shape[1]
    tn = 2048
    out = pl.pallas_call(
        _ada_kernel,
        grid=(n // tn,),
        in_specs=[pl.BlockSpec((r, d), lambda j: (0, 0)),
                  pl.BlockSpec((d, rank), lambda j: (0, 0)),
                  pl.BlockSpec((rank, tn), lambda j: (0, j)),
                  pl.BlockSpec((1, tn), lambda j: (0, j))],
        out_specs=pl.BlockSpec((r, tn), lambda j: (0, j)),
        out_shape=jax.ShapeDtypeStruct((r, n), F32),
        scratch_shapes=[pltpu.VMEM((r, rank), F32)],
        compiler_params=_cparams(1),
        name="ada_table",
    )(cond_pad, down, up, bias.reshape(1, n))
    return out.reshape(r, N_MOD, d)


def _rowwise_kernel(*refs, gate_idx, shift_idx, route, moe_resid, emit_h):
    it = iter(refs)
    h_ref = next(it)
    if moe_resid:
        y = _gather_combine(next(it), next(it), next(it), next(it), next(it), refs[-2], refs[-1])
    elif gate_idx is not None:
        y = next(it)[...].astype(F32)
    gmod_ref = next(it) if gate_idx is not None else None
    g_ref = next(it)
    smod_ref = next(it) if shift_idx is not None else None
    wt_ref, rb_ref = (next(it), next(it)) if route else (None, None)
    h = h_ref[...]
    if gate_idx is not None:
        h = h + gmod_ref[0, gate_idx:gate_idx + 1, :] * y
        if emit_h:
            next(it)[...] = h
    a = h * lax.rsqrt(jnp.mean(h * h, axis=-1, keepdims=True) + RMS_EPS) * g_ref[...]
    if shift_idx is not None:
        a = a * (1.0 + smod_ref[0, shift_idx + 1:shift_idx + 2, :]) + smod_ref[0, shift_idx:shift_idx + 1, :]
    if not route:
        a_ref = next(it)
        a_ref[...] = a.astype(a_ref.dtype)
        return
    packed_ref, ids_ref, rank_ref, w_ref, cnt_ref = (next(it) for _ in range(5))

    @pl.when(pl.program_id(0) == 0)
    def _():
        cnt_ref[...] = jnp.zeros_like(cnt_ref)

    packed_ref[...] = _pack_rows(a)
    ids, ranks, wmat, counts = _route(a, wt_ref[...], rb_ref[...], cnt_ref[:, 0:1])
    ids_ref[...] = ids
    rank_ref[...] = ranks
    w_ref[...] = wmat
    cnt_ref[...] = jnp.broadcast_to(counts, cnt_ref.shape)


def rowwise(h, gain, seg, *, resid=None, shift=None, out_dtype=BF16, rows=None, route=None, emit_h=True):
    n, d = h.shape
    rows = n if rows is None else rows
    moe_resid = resid is not None and isinstance(resid[0], MoeOut)
    tm = _pick(seg.tile_unit, 128 if moe_resid else 256)
    row_spec = pl.BlockSpec((tm, d), lambda i: (i, 0))
    mod_spec = pl.BlockSpec((1, N_MOD, d), lambda i: (seg.of_row(i * tm), 0, 0))
    in_specs, args, scratch = [row_spec], [h], []
    if moe_resid:
        last = rows // tm - 1
        in_specs += [pl.BlockSpec((TOP_K, tm), lambda i: (0, i), memory_space=pltpu.SMEM),
                     pl.BlockSpec((TOP_K, tm), lambda i: (0, jnp.minimum(i + 1, last)), memory_space=pltpu.SMEM),
                     pl.BlockSpec(memory_space=pl.ANY),
                     pl.BlockSpec((tm, LANES), lambda i: (i, 0)),
                     row_spec, mod_spec]
        moe_out = resid[0]
        args += [moe_out.dest, moe_out.dest, moe_out.ys, moe_out.wmat, moe_out.shared, resid[1]]
        scratch = [pltpu.VMEM((2, TOP_K, tm, d // 2), U32), pltpu.SemaphoreType.DMA((2,))]
    elif resid is not None:
        in_specs += [row_spec, mod_spec]
        args += [resid[0], resid[1]]
    in_specs.append(pl.BlockSpec((1, d), lambda i: (0, 0)))
    args.append(gain.reshape(1, d))
    if shift is not None:
        in_specs.append(mod_spec)
        args.append(shift[0])
    if route is not None:
        in_specs += [pl.BlockSpec((N_EXPERTS, d), lambda i: (0, 0)),
                     pl.BlockSpec((N_EXPERTS, 1), lambda i: (0, 0))]
        args += [route[0], route[1].reshape(N_EXPERTS, 1)]
    out_specs, out_shape = [], []
    if resid is not None and emit_h:
        out_specs.append(row_spec)
        out_shape.append(jax.ShapeDtypeStruct((rows, d), F32))
    if route is None:
        out_specs.append(row_spec)
        out_shape.append(jax.ShapeDtypeStruct((rows, d), out_dtype))
    else:
        out_specs += [pl.BlockSpec((tm, d // 2), lambda i: (i, 0)),
                      pl.BlockSpec((TOP_K, tm), lambda i: (0, i)),
                      pl.BlockSpec((TOP_K, tm), lambda i: (0, i)),
                      pl.BlockSpec((tm, LANES), lambda i: (i, 0)),
                      pl.BlockSpec((N_EXPERTS, LANES), lambda i: (0, 0))]
        out_shape += [jax.ShapeDtypeStruct((rows, d // 2), U32),
                      jax.ShapeDtypeStruct((TOP_K, rows), I32),
                      jax.ShapeDtypeStruct((TOP_K, rows), I32),
                      jax.ShapeDtypeStruct((rows, LANES), F32),
                      jax.ShapeDtypeStruct((N_EXPERTS, LANES), F32)]
    return pl.pallas_call(
        functools.partial(_rowwise_kernel, gate_idx=None if resid is None else resid[2],
                          shift_idx=None if shift is None else shift[1], route=route is not None,
                          moe_resid=moe_resid, emit_h=emit_h),
        grid=(rows // tm,),
        in_specs=in_specs, out_specs=out_specs, out_shape=out_shape, scratch_shapes=scratch,
        compiler_params=_cparams(1),
        name="rowwise",
    )(*args)


def _mm_plain_kernel(x_ref, w_ref, o_ref, *, scale_tiles, scale):
    acc = jnp.dot(x_ref[...], w_ref[...], preferred_element_type=F32)
    if scale_tiles:
        acc = acc * jnp.where(pl.program_id(1) < scale_tiles, scale, 1.0)
    o_ref[...] = acc.astype(o_ref.dtype)


def _mm_resid_kernel(x_ref, w_ref, h_ref, mod_ref, o_ref, *, gate_idx):
    acc = jnp.dot(x_ref[...], w_ref[...], preferred_element_type=F32)
    o_ref[...] = h_ref[...] + mod_ref[0, gate_idx:gate_idx + 1, :] * acc


def _mm_qkv_kernel(x_ref, w_ref, cos_ref, sin_ref, qn_ref, kn_ref, o_ref, *, nq_tiles, nk_tiles, qk_norm):
    j = pl.program_id(1)
    is_q = j < nq_tiles
    is_qk = j < nq_tiles + nk_tiles
    post = jnp.where(is_q, HEAD_DIM ** -0.5, 1.0)
    gain = jnp.where(is_q, qn_ref[...], kn_ref[...])
    w = w_ref[...]
    tm, tn = o_ref.shape
    def product(r):
        return jnp.dot(x_ref[r:r + QKV_ROWS, :], w, preferred_element_type=F32)

    nxt = product(0)
    for r in range(0, tm, QKV_ROWS):
        acc = nxt
        if r + QKV_ROWS < tm:
            nxt = product(r + QKV_ROWS)
        cos = cos_ref[r:r + QKV_ROWS, :]
        sin = sin_ref[r:r + QKV_ROWS, :]
        for s in range(0, tn, HEAD_DIM):
            raw = acc[:, s:s + HEAD_DIM]
            xh = raw
            if qk_norm:
                xh = xh * lax.rsqrt(jnp.mean(xh * xh, axis=-1, keepdims=True) + RMS_EPS) * gain
            xh = (xh * cos + pltpu.roll(xh, HEAD_DIM // 2, axis=1) * sin) * post
            o_ref[r:r + QKV_ROWS, s:s + HEAD_DIM] = jnp.where(is_qk, xh, raw).astype(o_ref.dtype)


def matmul(x, w, *, tm=1024, tn=512, out_dtype=BF16, rows=None, cols=None, scale_tiles=0, scale=1.0,
           resid=None, qkv=None):
    m, k = x.shape
    m = m if rows is None else rows
    n = w.shape[1] if cols is None else cols
    tm = _pick(m if resid is None else math.gcd(m, resid[3].tile_unit), tm)
    tn = _pick(n, tn)
    grid = (m // tm, n // tn)
    x_spec = pl.BlockSpec((tm, k), lambda i, j: (i, 0))
    w_spec = pl.BlockSpec((k, tn), lambda i, j: (0, j))
    o_spec = pl.BlockSpec((tm, tn), lambda i, j: (i, j))
    if resid is not None:
        h, mods, gate_idx, seg = resid
        kern = functools.partial(_mm_resid_kernel, gate_idx=gate_idx)
        in_specs = [x_spec, w_spec, o_spec,
                    pl.BlockSpec((1, N_MOD, tn), lambda i, j: (seg.of_row(i * tm), 0, j))]
        args = (x, w, h, mods)
        out_dtype = F32
    elif qkv is not None:
        cos, sin, qn, kn, qk_norm = qkv
        kern = functools.partial(_mm_qkv_kernel, nq_tiles=N_HEADS * HEAD_DIM // tn,
                                 nk_tiles=N_KV_HEADS * HEAD_DIM // tn, qk_norm=qk_norm)
        tab_spec = pl.BlockSpec((tm, HEAD_DIM), lambda i, j: (i, 0))
        vec_spec = pl.BlockSpec((1, HEAD_DIM), lambda i, j: (0, 0))
        in_specs = [x_spec, w_spec, tab_spec, tab_spec, vec_spec, vec_spec]
        args = (x, w, cos, sin, qn.reshape(1, HEAD_DIM), kn.reshape(1, HEAD_DIM))
    else:
        kern = functools.partial(_mm_plain_kernel, scale_tiles=scale_tiles, scale=scale)
        in_specs = [x_spec, w_spec]
        args = (x, w)
    return pl.pallas_call(
        kern, grid=grid, in_specs=in_specs, out_specs=o_spec,
        out_shape=jax.ShapeDtypeStruct((m, n), out_dtype),
        compiler_params=_cparams(2),
        name="matmul",
    )(*args)


def _attn_kernel(*refs, tq, lat_keys, window, has_sink):
    it = iter(refs)
    sink_ref = next(it) if has_sink else None
    q_ref = next(it)
    kc_ref, vc_ref = next(it), next(it)
    kl_ref, vl_ref = (next(it), next(it)) if lat_keys else (None, None)
    o_ref = refs[-1]
    kvh = pl.program_id(1)
    qi = pl.program_id(2)
    nt = (((1,), (1,)), ((), ()))
    kc = kc_ref[...]
    vc = _with_ones(vc_ref[...])
    if lat_keys and window:
        seq = kl_ref.shape[0]
        band = min(tq + 2 * WINDOW, seq)
        start = pl.multiple_of(jnp.clip(qi * tq - WINDOW, 0, seq - band), LANES)
        kl = kl_ref[pl.ds(start, band), :]
        vl = _with_ones(vl_ref[pl.ds(start, band), :])
        q_pos = qi * tq + lax.broadcasted_iota(jnp.int32, (tq, band), 0)
        k_pos = start + lax.broadcasted_iota(jnp.int32, (tq, band), 1)
        in_window = jnp.abs(q_pos - k_pos) <= WINDOW
    elif lat_keys:
        kl = kl_ref[...]
        vl = _with_ones(vl_ref[...])
    def scores(g):
        qg = q_ref[:, g * HEAD_DIM:(g + 1) * HEAD_DIM]
        s_c = lax.dot_general(qg, kc, nt, preferred_element_type=F32)
        if not lat_keys:
            return s_c, None
        s_l = lax.dot_general(qg, kl, nt, preferred_element_type=F32)
        return s_c, jnp.where(in_window, s_l, NEG_INF) if window else s_l

    nxt = scores(0)
    for g in range(KV_GROUP):
        s_c, s_l = nxt
        if g + 1 < KV_GROUP:
            nxt = scores(g + 1)
        m = jnp.max(s_c, axis=-1, keepdims=True)
        if lat_keys:
            m = jnp.maximum(m, jnp.max(s_l, axis=-1, keepdims=True))
        if has_sink:
            sink = sink_ref[kvh * KV_GROUP + g]
            m = jnp.maximum(m, sink)
        acc = jnp.dot(jnp.exp((s_c - m).astype(BF16)), vc, preferred_element_type=F32)
        if lat_keys:
            acc = acc + jnp.dot(jnp.exp((s_l - m).astype(BF16)), vl, preferred_element_type=F32)
        den = acc[:, HEAD_DIM:HEAD_DIM + 1]
        if has_sink:
            den = den + jnp.exp(sink - m)
        o_ref[:, g * HEAD_DIM:(g + 1) * HEAD_DIM] = (acc[:, :HEAD_DIM] / den).astype(o_ref.dtype)


def _with_ones(v):
    return jnp.concatenate([v, jnp.ones_like(v)], axis=1)


def attention(qkv, dims, *, lat_queries, window, sink, out=None, out_rows=None):
    bsz, seq, n_ctx = dims
    n_lat = bsz * seq
    g_cols = KV_GROUP * HEAD_DIM
    k_col0 = N_HEADS * HEAD_DIM // HEAD_DIM
    v_col0 = k_col0 + N_KV_HEADS
    has_sink = sink is not None
    if lat_queries:
        tq = _pick(seq, 512)
        q_tiles = seq // tq
        q_row0 = 0
    else:
        tq = n_ctx
        q_tiles = 1
        q_row0 = n_lat // tq
    aliases = {}
    if out is not None:
        out_rows = out.shape[0]
    ctx_blk0 = n_lat // n_ctx
    in_specs = []
    args = []
    if has_sink:
        in_specs.append(pl.BlockSpec(memory_space=pltpu.SMEM))
        args.append(sink.astype(F32))
    in_specs += [
        pl.BlockSpec((tq, g_cols), lambda b, h, i: (q_row0 + b * q_tiles + i, h)),
        pl.BlockSpec((n_ctx, HEAD_DIM), lambda b, h, i: (ctx_blk0 + b, k_col0 + h)),
        pl.BlockSpec((n_ctx, HEAD_DIM), lambda b, h, i: (ctx_blk0 + b, v_col0 + h)),
    ]
    args += [qkv, qkv, qkv]
    if lat_queries:
        in_specs += [
            pl.BlockSpec((seq, HEAD_DIM), lambda b, h, i: (b, k_col0 + h)),
            pl.BlockSpec((seq, HEAD_DIM), lambda b, h, i: (b, v_col0 + h)),
        ]
        args += [qkv, qkv]
    if out is not None:
        aliases = {len(args): 0}
        in_specs.append(pl.BlockSpec(memory_space=pl.ANY))
        args.append(out)
    return pl.pallas_call(
        functools.partial(_attn_kernel, tq=tq, lat_keys=lat_queries, window=window, has_sink=has_sink),
        grid=(bsz, N_KV_HEADS, q_tiles),
        in_specs=in_specs,
        out_specs=pl.BlockSpec((tq, g_cols), lambda b, h, i: (q_row0 + b * q_tiles + i, h)),
        out_shape=jax.ShapeDtypeStruct((out_rows, N_HEADS * HEAD_DIM), BF16),
        input_output_aliases=aliases,
        compiler_params=_cparams(3),
        name="attention",
    )(*args)


def _mlstm_gate_kernel(x_ref, w_ref, b_ref, o_ref):
    g = jnp.dot(x_ref[...], w_ref[...], preferred_element_type=F32) + b_ref[...]
    g = GATE_CAP * jnp.tanh(g / GATE_CAP)
    log_f = jnp.minimum(g, 0.0) - jnp.log(1.0 + jnp.exp(-jnp.abs(g)))
    col = lax.broadcasted_iota(jnp.int32, g.shape, 1)
    is_f = (col // M_HEADS) % 2 == 1
    o_ref[...] = jnp.transpose(jnp.where(is_f, log_f, g))


def mlstm_gates(a, w_gate_pad, gate_b_pad):
    n, k = a.shape
    tm = _pick(n, 1024)
    return pl.pallas_call(
        _mlstm_gate_kernel,
        grid=(n // tm,),
        in_specs=[pl.BlockSpec((tm, k), lambda i: (i, 0)),
                  pl.BlockSpec((k, LANES), lambda i: (0, 0)),
                  pl.BlockSpec((1, LANES), lambda i: (0, 0))],
        out_specs=pl.BlockSpec((LANES, tm), lambda i: (0, i)),
        out_shape=jax.ShapeDtypeStruct((LANES, n), F32),
        compiler_params=_cparams(1),
        name="mlstm_gates",
    )(a, w_gate_pad, gate_b_pad)


def _mlstm_kernel(*refs):
    ct_ref, n_ref, m_ref = refs[-3:]
    c = pl.program_id(1)
    L = M_CHUNK

    @pl.when(c == 0)
    def _():
        ct_ref[...] = jnp.zeros_like(ct_ref)
        n_ref[...] = jnp.zeros_like(n_ref)
        m_ref[...] = jnp.zeros_like(m_ref)

    nt = (((1,), (1,)), ((), ()))
    tn = (((0,), (0,)), ((), ()))
    for d in range(2):
        q_ref, k_ref, v_ref, gi_ref, gf_ref = refs[5 * d:5 * d + 5]
        o_ref = refs[10 + d]
        diff = lax.broadcasted_iota(jnp.int32, (L, L), 0) - lax.broadcasted_iota(jnp.int32, (L, L), 1)
        if d == 1:
            diff = -diff
        mask = diff >= 0
        mask_t = diff <= 0
        eye = diff == 0
        def early(h):
            st = d * M_HEADS + h
            qc = q_ref[:, h * M_QK_DIM:(h + 1) * M_QK_DIM]
            kc = k_ref[:, h * M_QK_DIM:(h + 1) * M_QK_DIM]
            i_row = gi_ref[h:h + 1, :]
            f_row = gf_ref[h:h + 1, :]
            f_col = jnp.sum(jnp.where(eye, f_row, 0.0), axis=1, keepdims=True)
            i_col = jnp.sum(jnp.where(eye, i_row, 0.0), axis=1, keepdims=True)
            b_col = jnp.sum(jnp.where(mask, f_row, 0.0), axis=1, keepdims=True)
            b_row = jnp.sum(jnp.where(mask_t, f_col, 0.0), axis=0, keepdims=True)
            log_intra = jnp.where(mask, b_col - b_row + i_row, NEG_INF)
            qk = lax.dot_general(qc, kc, nt, preferred_element_type=F32)
            inter = jnp.dot(qc, ct_ref[st].astype(BF16), preferred_element_type=F32)
            return qc, kc, f_row, i_col, b_col, log_intra, qk, inter

        nxt = early(0)
        for h in range(M_HEADS):
            st = d * M_HEADS + h
            qc, kc, f_row, i_col, b_col, log_intra, qk, inter = nxt
            if h + 1 < M_HEADS:
                nxt = early(h + 1)
            vc = v_ref[:, h * M_V_DIM:(h + 1) * M_V_DIM]
            m_prev = m_ref[st, :, 0:1]
            n_prev = n_ref[st]
            log_inter = b_col + m_prev
            m_t = jnp.maximum(log_inter, jnp.max(log_intra, axis=1, keepdims=True))
            w_inter = jnp.exp(log_inter - m_t)
            s_qk = qk * jnp.exp(log_intra - m_t)
            num = jnp.dot(s_qk.astype(BF16), vc, preferred_element_type=F32) + w_inter * inter
            den = (jnp.sum(s_qk, axis=1, keepdims=True)
                   + w_inter * jnp.sum(qc.astype(F32) * n_prev, axis=1, keepdims=True))
            h_out = num / jnp.maximum(jnp.abs(den), jnp.exp(-m_t))
            o_ref[:, h * M_V_DIM:(h + 1) * M_V_DIM] = h_out.astype(o_ref.dtype)
            b_tot = jnp.sum(f_row, axis=1, keepdims=True)
            log_w = b_tot - b_col + i_col
            m_new = jnp.maximum(b_tot + m_prev, jnp.max(log_w, axis=0, keepdims=True))
            decay = jnp.exp(b_tot + m_prev - m_new)
            w_k = jnp.exp(log_w - m_new)
            vw = (vc.astype(F32) * w_k).astype(BF16)
            ct_ref[st] = decay * ct_ref[st] + lax.dot_general(kc, vw, tn, preferred_element_type=F32)
            n_ref[st] = decay * n_prev + jnp.sum(kc.astype(F32) * w_k, axis=0, keepdims=True)
            m_ref[st] = jnp.broadcast_to(m_new, (1, LANES))


def mlstm_scan(proj, gates_t, dims):
    bsz, seq, n_ctx = dims
    nt_rows = proj.shape[0]
    L = M_CHUNK
    ncc, nlc = n_ctx // L, seq // L
    lat_blocks = bsz * nlc
    qk_w = M_HEADS * M_QK_DIM

    def row_blk(b, d, c):
        cc = c if d == 0 else ncc - 1 - c
        lc = c - ncc if d == 0 else nlc - 1 - (c - ncc)
        return jnp.where(c < ncc, lat_blocks + b * ncc + cc, b * nlc + lc)

    def dir_specs(d):
        return [
            pl.BlockSpec((L, qk_w), lambda b, c: (row_blk(b, d, c), 0)),
            pl.BlockSpec((L, qk_w), lambda b, c: (row_blk(b, d, c), 1)),
            pl.BlockSpec((L, D_MODEL), lambda b, c: (row_blk(b, d, c), 1)),
            pl.BlockSpec((M_HEADS, L), lambda b, c: (2 * d, row_blk(b, d, c))),
            pl.BlockSpec((M_HEADS, L), lambda b, c: (2 * d + 1, row_blk(b, d, c))),
        ]

    return pl.pallas_call(
        _mlstm_kernel,
        grid=(bsz, ncc + nlc),
        in_specs=dir_specs(0) + dir_specs(1),
        out_specs=[pl.BlockSpec((L, D_MODEL), lambda b, c: (row_blk(b, 0, c), 0)),
                   pl.BlockSpec((L, D_MODEL), lambda b, c: (row_blk(b, 1, c), 0))],
        out_shape=[jax.ShapeDtypeStruct((nt_rows, D_MODEL), BF16)] * 2,
        scratch_shapes=[pltpu.VMEM((2 * M_HEADS, M_QK_DIM, M_V_DIM), F32),
                        pltpu.VMEM((2 * M_HEADS, 1, M_QK_DIM), F32),
                        pltpu.VMEM((2 * M_HEADS, 1, LANES), F32)],
        compiler_params=_cparams(2),
        name="mlstm_scan",
    )(*([proj, proj, proj, gates_t, gates_t] * 2))


def _mlstm_finish_kernel(hf_ref, hb_ref, og_ref, g_ref, o_ref):
    for h in range(M_HEADS):
        sl = slice(h * M_V_DIM, (h + 1) * M_V_DIM)
        x = hf_ref[:, sl].astype(F32) + hb_ref[:, sl].astype(F32)
        y = x * lax.rsqrt(jnp.mean(x * x, axis=-1, keepdims=True) + RMS_EPS) * g_ref[:, sl]
        o_ref[:, sl] = (y * jax.nn.sigmoid(og_ref[:, sl].astype(F32))).astype(o_ref.dtype)


def mlstm_finish(h_fwd, h_bwd, proj, head_norm):
    n, d = h_fwd.shape
    tm = _pick(n, 256)
    row_spec = pl.BlockSpec((tm, d), lambda i: (i, 0))
    return pl.pallas_call(
        _mlstm_finish_kernel,
        grid=(n // tm,),
        in_specs=[row_spec, row_spec,
                  pl.BlockSpec((tm, d), lambda i: (i, 2)),
                  pl.BlockSpec((1, d), lambda i: (0, 0))],
        out_specs=row_spec,
        out_shape=jax.ShapeDtypeStruct((n, d), BF16),
        compiler_params=_cparams(1),
        name="mlstm_finish",
    )(h_fwd, h_bwd, proj, head_norm.reshape(1, d))


def _pack_rows(x):
    half = x.shape[1] // 2
    return _pack_pair(x[:, :half], x[:, half:])


def _pack_pair(lo, hi):
    lo = lax.bitcast_convert_type(lo.astype(BF16).astype(F32), U32)
    hi = lax.bitcast_convert_type(hi.astype(BF16).astype(F32), U32)
    return (hi & jnp.uint32(0xFFFF0000)) | (lo >> 16)


def _unpack_rows(p):
    lo = lax.bitcast_convert_type(p << 16, F32)
    hi = lax.bitcast_convert_type(p & jnp.uint32(0xFFFF0000), F32)
    return lo, hi


def _route(a, w_t, bias, counts):
    tm = a.shape[0]
    per = N_EXPERTS // N_GROUPS
    nt = (((1,), (1,)), ((), ()))
    w_hi = w_t.astype(BF16)
    w_lo = (w_t - w_hi.astype(F32)).astype(BF16)
    a_hi = a.astype(BF16)
    a_lo = (a - a_hi.astype(F32)).astype(BF16)
    logits = (lax.dot_general(w_hi, a_hi, nt, preferred_element_type=F32)
              + lax.dot_general(w_lo, a_hi, nt, preferred_element_type=F32)
              + lax.dot_general(w_hi, a_lo, nt, preferred_element_type=F32))
    scores = jax.nn.sigmoid(logits).reshape(N_GROUPS, per, tm)
    biased = scores + bias.reshape(N_GROUPS, per, 1)
    e_iota = lax.broadcasted_iota(jnp.int32, (N_GROUPS, per, tm), 1).astype(F32)
    g_iota = lax.broadcasted_iota(jnp.int32, (N_GROUPS, 1, tm), 0).astype(F32)
    lin_iota = lax.broadcasted_iota(jnp.int32, (N_GROUPS, per, tm), 0).astype(F32) * per + e_iota
    m1 = jnp.max(biased, axis=1, keepdims=True)
    i1 = jnp.min(jnp.where(biased == m1, e_iota, float(per)), axis=1, keepdims=True)
    m2 = jnp.max(jnp.where(e_iota == i1, NEG_INF, biased), axis=1, keepdims=True)
    gscore = m1 + m2
    gsel = jnp.zeros(gscore.shape, F32)
    for _ in range(TOPK_GROUPS):
        cur = jnp.where(gsel > 0.0, NEG_INF, gscore)
        gm = jnp.max(cur, axis=0, keepdims=True)
        gi = jnp.min(jnp.where(cur == gm, g_iota, float(N_GROUPS)), axis=0, keepdims=True)
        gsel = jnp.where(g_iota == gi, 1.0, gsel)
    cand = jnp.where(gsel > 0.0, biased, NEG_INF)
    sel = jnp.zeros(cand.shape, F32)
    picks = []
    for _ in range(TOP_K):
        cur = jnp.where(sel > 0.0, NEG_INF, cand)
        em = jnp.max(jnp.max(cur, axis=1, keepdims=True), axis=0, keepdims=True)
        hit = jnp.where(cur == em, lin_iota, float(N_EXPERTS))
        ei = jnp.min(jnp.min(hit, axis=1, keepdims=True), axis=0, keepdims=True)
        sel = jnp.where(lin_iota == ei, 1.0, sel)
        picks.append(ei)
    sel2 = sel.reshape(N_EXPERTS, tm)
    before = lax.broadcasted_iota(jnp.int32, (tm, tm), 0) < lax.broadcasted_iota(jnp.int32, (tm, tm), 1)
    rank = jnp.dot(sel2.astype(BF16), jnp.where(before, 1.0, 0.0).astype(BF16),
                   preferred_element_type=F32) + counts
    new_counts = counts + jnp.sum(sel2, axis=1, keepdims=True)
    rank3 = rank.reshape(N_GROUPS, per, tm)
    wsum = jnp.sum(jnp.sum(jnp.where(sel > 0.0, scores, 0.0), axis=1, keepdims=True), axis=0, keepdims=True)

    def pick(ei, table):
        v = jnp.where(lin_iota == ei, table, 0.0)
        return jnp.sum(jnp.sum(v, axis=1, keepdims=True), axis=0, keepdims=True).reshape(1, tm)

    ids = jnp.concatenate([ei.reshape(1, tm) for ei in picks], axis=0).astype(I32)
    ranks = jnp.concatenate([pick(ei, rank3) for ei in picks], axis=0).astype(I32)
    w = jnp.concatenate([pick(ei, scores) for ei in picks], axis=0) / wsum.reshape(1, tm) * ROUTED_SCALE
    wmat = jnp.transpose(jnp.concatenate([w, jnp.zeros((LANES - TOP_K, tm), F32)], axis=0))
    return ids, ranks, wmat, new_counts


def _dest_kernel(starts_ref, ids_ref, rank_ref, dest_ref):
    ids = ids_ref[...]
    acc = rank_ref[...]
    for e in range(N_EXPERTS):
        acc = acc + jnp.where(ids == e, starts_ref[e], 0)
    dest_ref[...] = acc


def dest_rows(starts, ids, ranks):
    return pl.pallas_call(
        _dest_kernel,
        in_specs=[pl.BlockSpec(memory_space=pltpu.SMEM), pl.BlockSpec(memory_space=pltpu.VMEM),
                  pl.BlockSpec(memory_space=pltpu.VMEM)],
        out_specs=pl.BlockSpec(memory_space=pltpu.VMEM),
        out_shape=jax.ShapeDtypeStruct(ids.shape, I32),
        compiler_params=pltpu.CompilerParams(vmem_limit_bytes=VMEM_LIMIT),
        name="dest_rows",
    )(starts, ids, ranks)


def _dispatch_kernel(starts_ref, ends_ref, dest_ref, f_ref, gu_ref, dn_ref, xs_ref, sh_ref,
                     zero_ref, gu_bf, dn_bf, sem):
    tm = f_ref.shape[0]

    def zero_copy(e):
        row0 = pl.multiple_of(ends_ref[e] - MOE_TM, MOE_TM)
        return pltpu.make_async_copy(zero_ref, xs_ref.at[pl.ds(row0, MOE_TM)], sem)

    @pl.when(pl.program_id(0) == 0)
    def _():
        gu_bf[...] = gu_ref[...].astype(BF16)
        dn_bf[...] = dn_ref[...].astype(BF16)
        zero_ref[...] = jnp.zeros_like(zero_ref)
        for e in range(N_EXPERTS):
            @pl.when(ends_ref[e] > starts_ref[e])
            def _():
                zero_copy(e).start()
        for e in range(N_EXPERTS):
            @pl.when(ends_ref[e] > starts_ref[e])
            def _():
                zero_copy(e).wait()

    def row_copy(t, d):
        return pltpu.make_async_copy(f_ref.at[pl.ds(t, 1)], xs_ref.at[pl.ds(d, 1)], sem)

    def issue(t, c):
        for k in range(TOP_K):
            row_copy(t, dest_ref[k, t]).start()
        return c

    def drain(t, c):
        for k in range(TOP_K):
            row_copy(0, 0).wait()
        return c

    lax.fori_loop(0, tm, issue, 0)
    _swiglu_expert(f_ref, gu_bf, dn_bf, sh_ref, pack_out=False)
    lax.fori_loop(0, tm, drain, 0)


def dispatch(starts, ends, dest, packed, n_rows, shared_gu, shared_down):
    n, half = packed.shape
    tm = _pick(n, 256)

    def whole(shape):
        return pl.BlockSpec(shape, lambda i, s, e: (0,) * len(shape))

    return pl.pallas_call(
        _dispatch_kernel,
        grid_spec=pltpu.PrefetchScalarGridSpec(
            num_scalar_prefetch=2, grid=(n // tm,),
            in_specs=[pl.BlockSpec((TOP_K, tm), lambda i, s, e: (0, i), memory_space=pltpu.SMEM),
                      pl.BlockSpec((tm, half), lambda i, s, e: (i, 0)),
                      whole(shared_gu.shape), whole(shared_down.shape)],
            out_specs=[pl.BlockSpec(memory_space=pl.ANY),
                       pl.BlockSpec((tm, 2 * half), lambda i, s, e: (i, 0))],
            scratch_shapes=[pltpu.VMEM((MOE_TM, half), U32), pltpu.VMEM(shared_gu.shape, BF16),
                            pltpu.VMEM(shared_down.shape, BF16), pltpu.SemaphoreType.DMA(())]),
        out_shape=[jax.ShapeDtypeStruct((n_rows, half), U32), jax.ShapeDtypeStruct((n, 2 * half), BF16)],
        compiler_params=_cparams(1),
        name="dispatch",
    )(starts, ends, dest, packed, shared_gu, shared_down)


def _swiglu_expert(x_ref, gu_bf, dn_bf, o_ref, pack_out):
    half = x_ref.shape[1]
    chunks = range(0, half, MOE_CHUNK)

    def unpack(c):
        lo, hi = _unpack_rows(x_ref[:, c:c + MOE_CHUNK])
        return lo.astype(BF16), hi.astype(BF16)

    def down(c):
        return (jnp.dot(act, dn_bf[:, c:c + MOE_CHUNK], preferred_element_type=F32),
                jnp.dot(act, dn_bf[:, half + c:half + c + MOE_CHUNK], preferred_element_type=F32))

    gu = None
    nxt = unpack(0)
    for c in chunks:
        lo, hi = nxt
        if c + MOE_CHUNK < half:
            nxt = unpack(c + MOE_CHUNK)
        part = (jnp.dot(lo, gu_bf[c:c + MOE_CHUNK, :], preferred_element_type=F32)
                + jnp.dot(hi, gu_bf[half + c:half + c + MOE_CHUNK, :], preferred_element_type=F32))
        gu = part if gu is None else gu + part
    g = gu[:, :D_EXPERT]
    act = (g * jax.nn.sigmoid(g) * gu[:, D_EXPERT:]).astype(BF16)
    nxt = down(0)
    for c in chunks:
        y_lo, y_hi = nxt
        if c + MOE_CHUNK < half:
            nxt = down(c + MOE_CHUNK)
        if pack_out:
            o_ref[:, c:c + MOE_CHUNK] = _pack_pair(y_lo, y_hi)
        else:
            o_ref[:, c:c + MOE_CHUNK] = y_lo.astype(o_ref.dtype)
            o_ref[:, half + c:half + c + MOE_CHUNK] = y_hi.astype(o_ref.dtype)


def _grouped_kernel(te_ref, nt_ref, x_ref, gu_ref, dn_ref, o_ref, gu_bf, dn_bf):
    j = pl.program_id(0)

    @pl.when(j < nt_ref[0])
    def _():
        @pl.when(jnp.logical_or(j == 0, te_ref[j] != te_ref[jnp.maximum(j - 1, 0)]))
        def _():
            gu_bf[...] = gu_ref[0].astype(BF16)
            dn_bf[...] = dn_ref[0].astype(BF16)

        _swiglu_expert(x_ref, gu_bf, dn_bf, o_ref, pack_out=True)


def grouped_experts(tile_expert, n_tiles, xs, exp_gu, exp_down):
    rows, half = xs.shape
    d = 2 * half

    def tile(j, nt):
        return jnp.minimum(j, nt[0] - 1)

    return pl.pallas_call(
        _grouped_kernel,
        grid_spec=pltpu.PrefetchScalarGridSpec(
            num_scalar_prefetch=2, grid=(rows // MOE_TM,),
            in_specs=[pl.BlockSpec((MOE_TM, half), lambda j, te, nt: (tile(j, nt), 0)),
                      pl.BlockSpec((1, d, 2 * D_EXPERT), lambda j, te, nt: (te[tile(j, nt)], 0, 0)),
                      pl.BlockSpec((1, D_EXPERT, d), lambda j, te, nt: (te[tile(j, nt)], 0, 0))],
            out_specs=pl.BlockSpec((MOE_TM, half), lambda j, te, nt: (tile(j, nt), 0)),
            scratch_shapes=[pltpu.VMEM((d, 2 * D_EXPERT), BF16), pltpu.VMEM((D_EXPERT, d), BF16)]),
        out_shape=jax.ShapeDtypeStruct((rows, half), U32),
        compiler_params=_cparams(1),
        name="grouped_experts",
    )(tile_expert, n_tiles, xs, exp_gu, exp_down)


def _gather_combine(dest_ref, next_dest_ref, ys_ref, w_ref, sh_ref, buf_ref, sems):
    tm, d = sh_ref.shape
    half = d // 2
    step = pl.program_id(0)
    slot = step % 2

    def row_copy(s, k, t, src):
        return pltpu.make_async_copy(ys_ref.at[pl.ds(src, 1)], buf_ref.at[s, k, pl.ds(t, 1)], sems.at[s])

    def request(idx_ref, s):
        def body(t, c):
            for k in range(TOP_K):
                row_copy(s, k, t, idx_ref[k, t]).start()
            return c
        lax.fori_loop(0, tm, body, 0)

    @pl.when(step == 0)
    def _():
        request(dest_ref, 0)

    @pl.when(step + 1 < pl.num_programs(0))
    def _():
        request(next_dest_ref, 1 - slot)

    def drain(t, c):
        for k in range(TOP_K):
            row_copy(slot, 0, 0, 0).wait()
        return c

    lax.fori_loop(0, tm, drain, 0)
    w = w_ref[...]
    y_lo = sh_ref[:, :half].astype(F32)
    y_hi = sh_ref[:, half:].astype(F32)
    for k in range(TOP_K):
        lo, hi = _unpack_rows(buf_ref[slot, k])
        y_lo = y_lo + w[:, k:k + 1] * lo
        y_hi = y_hi + w[:, k:k + 1] * hi
    return jnp.concatenate([y_lo, y_hi], axis=1)


class MoeOut(NamedTuple):
    dest: jax.Array
    ys: jax.Array
    wmat: jax.Array
    shared: jax.Array


def moe_experts(packed, ids, ranks, wmat, counts, moe):
    _, _, exp_gu, exp_down, shared_gu, shared_down = moe
    n = packed.shape[0]
    cnt = counts[:, 0].astype(I32)
    padded = (cnt + MOE_TM - 1) // MOE_TM * MOE_TM
    ends = jnp.cumsum(padded)
    starts = ends - padded
    n_max = n * TOP_K // MOE_TM + N_EXPERTS
    n_tiles = (ends[-1] // MOE_TM).reshape(1)
    tile_row0 = jnp.arange(n_max, dtype=I32) * MOE_TM
    tile_expert = jnp.minimum(jnp.sum(ends[None, :] <= tile_row0[:, None], axis=1), N_EXPERTS - 1).astype(I32)
    dest = dest_rows(starts, ids, ranks)
    xs, shared = dispatch(starts, ends, dest, packed, n_max * MOE_TM, shared_gu, shared_down)
    ys = grouped_experts(tile_expert, n_tiles, xs, exp_gu, exp_down)
    return MoeOut(dest, ys, wmat, shared)


def _rope_tables(seq, n_rows):
    pos = jnp.arange(seq)
    row = (pos // GRID_W).astype(F32)
    col = (pos % GRID_W).astype(F32)
    n_freq = HEAD_DIM // 4
    inv_freq = ROPE_THETA ** (-jnp.arange(n_freq, dtype=F32) / n_freq)
    ang = jnp.concatenate([row[:, None] * inv_freq, col[:, None] * inv_freq], axis=-1)
    cos, sin = jnp.cos(ang), jnp.sin(ang)
    cos_full = jnp.concatenate([cos, cos], axis=-1)
    sin_full = jnp.concatenate([-sin, sin], axis=-1)
    return cos_full, sin_full


def _trunk(x, c, ctx, c_ctx, layers, final_norm):
    bsz, seq, d = x.shape
    n_ctx = ctx.shape[1]
    dims = (bsz, seq, n_ctx)
    n_lat = bsz * seq
    n_all = n_lat + bsz * n_ctx
    depth = len(layers)

    seg = _Segments(seq, bsz, bsz * n_ctx)

    h = jnp.concatenate([x.reshape(n_lat, d), ctx.reshape(bsz * n_ctx, d)], axis=0)
    mod_rows = -(-(bsz + 1) // 8) * 8
    cond = jnp.zeros((mod_rows, d), F32).at[:bsz].set(c).at[bsz].set(c_ctx)

    cos1, sin1 = _rope_tables(seq, n_all)
    cos_t = jnp.concatenate([jnp.tile(cos1, (bsz, 1)), jnp.ones((bsz * n_ctx, HEAD_DIM), F32)], axis=0)
    sin_t = jnp.concatenate([jnp.tile(sin1, (bsz, 1)), jnp.zeros((bsz * n_ctx, HEAD_DIM), F32)], axis=0)
    ones_hd = jnp.ones((HEAD_DIM,), F32)

    tables = [ada_table(cond, *layer[1]) for layer in layers]
    a, = rowwise(h, layers[0][2], seg, shift=(tables[0], 0))
    for li, (kind, ada, norm1, mixer, norm2, moe) in enumerate(layers):
        need_ctx = li < depth - 1
        rows_out = n_all if need_ctx else n_lat
        mods = tables[li]
        resid = (h, mods, 2, seg)
        if kind == "mlstm":
            w_in, gate_b, head_norm, w_o = mixer
            n_main = 2 * M_HEADS * M_QK_DIM + M_HEADS * M_V_DIM + D_MODEL
            proj = matmul(a, w_in.astype(BF16), cols=n_main, scale_tiles=M_HEADS * M_QK_DIM // 512,
                          scale=M_QK_DIM ** -0.5)
            n_gate = 4 * M_HEADS
            w_gate = jnp.zeros((d, LANES), BF16).at[:, :n_gate].set(w_in[:, n_main:].astype(BF16))
            gate_b_pad = jnp.zeros((1, LANES), F32).at[0, :n_gate].set(gate_b.astype(F32))
            gates_t = mlstm_gates(a, w_gate, gate_b_pad)
            h_fwd, h_bwd = mlstm_scan(proj, gates_t, dims)
            mixed = mlstm_finish(h_fwd, h_bwd, proj, head_norm)
            h = matmul(mixed, w_o.astype(BF16), resid=resid, rows=rows_out)
        else:
            if kind == "global":
                w_qkv, q_norm, k_norm, w_o = mixer
                qkv = matmul(a, w_qkv.astype(BF16), qkv=(cos_t, sin_t, q_norm, k_norm, True))
                sink = None
            else:
                w_qkv, sink, w_o = mixer
                qkv = matmul(a, w_qkv.astype(BF16), qkv=(cos_t, sin_t, ones_hd, ones_hd, False))
            mixed = attention(qkv, dims, lat_queries=True, window=kind == "swa", sink=sink, out_rows=rows_out)
            if need_ctx:
                mixed = attention(qkv, dims, lat_queries=False, window=False, sink=sink, out=mixed)
            h = matmul(mixed, w_o.astype(BF16), resid=resid, rows=rows_out)
        router_w, router_b, exp_gu, exp_down, shared_gu, shared_down = moe
        routed = rowwise(h, norm2, seg, shift=(mods, 3), rows=rows_out, route=(router_w.T, router_b))
        y = moe_experts(*routed, moe)
        if li + 1 < depth:
            h, a = rowwise(h, layers[li + 1][2], seg, resid=(y, mods, 5), shift=(tables[li + 1], 0), rows=rows_out)
        else:
            out, = rowwise(h, final_norm, seg, resid=(y, mods, 5), rows=rows_out, out_dtype=F32, emit_h=False)
    return out.reshape(bsz, seq, d)


def kernel(x, c, ctx, c_ctx, l0_ada_down, l0_ada_up, l0_ada_b, l0_norm1, l0_attn_qkv, l0_q_norm, l0_k_norm, l0_attn_o, l0_norm2, l0_router_w, l0_router_b, l0_exp_gu, l0_exp_down, l0_shared_gu, l0_shared_down, l1_ada_down, l1_ada_up, l1_ada_b, l1_norm1, l1_mlstm_in, l1_mlstm_gate_b, l1_mlstm_head_norm, l1_mlstm_o, l1_norm2, l1_router_w, l1_router_b, l1_exp_gu, l1_exp_down, l1_shared_gu, l1_shared_down, l2_ada_down, l2_ada_up, l2_ada_b, l2_norm1, l2_swa_qkv, l2_swa_sink, l2_swa_o, l2_norm2, l2_router_w, l2_router_b, l2_exp_gu, l2_exp_down, l2_shared_gu, l2_shared_down, l3_ada_down, l3_ada_up, l3_ada_b, l3_norm1, l3_attn_qkv, l3_q_norm, l3_k_norm, l3_attn_o, l3_norm2, l3_router_w, l3_router_b, l3_exp_gu, l3_exp_down, l3_shared_gu, l3_shared_down, final_norm):
    layers = [
        ("global", (l0_ada_down, l0_ada_up, l0_ada_b), l0_norm1, (l0_attn_qkv, l0_q_norm, l0_k_norm, l0_attn_o), l0_norm2,
         (l0_router_w, l0_router_b, l0_exp_gu, l0_exp_down, l0_shared_gu, l0_shared_down)),
        ("mlstm", (l1_ada_down, l1_ada_up, l1_ada_b), l1_norm1, (l1_mlstm_in, l1_mlstm_gate_b, l1_mlstm_head_norm, l1_mlstm_o), l1_norm2,
         (l1_router_w, l1_router_b, l1_exp_gu, l1_exp_down, l1_shared_gu, l1_shared_down)),
        ("swa", (l2_ada_down, l2_ada_up, l2_ada_b), l2_norm1, (l2_swa_qkv, l2_swa_sink, l2_swa_o), l2_norm2,
         (l2_router_w, l2_router_b, l2_exp_gu, l2_exp_down, l2_shared_gu, l2_shared_down)),
        ("global", (l3_ada_down, l3_ada_up, l3_ada_b), l3_norm1, (l3_attn_qkv, l3_q_norm, l3_k_norm, l3_attn_o), l3_norm2,
         (l3_router_w, l3_router_b, l3_exp_gu, l3_exp_down, l3_shared_gu, l3_shared_down)),
    ]
    return _trunk(x, c, ctx, c_ctx, layers, final_norm)
```

```python
import functools
import math
from typing import NamedTuple

import jax
import jax.numpy as jnp
from jax import lax
from jax.experimental import pallas as pl
from jax.experimental.pallas import tpu as pltpu

F32 = jnp.float32
BF16 = jnp.bfloat16
U32 = jnp.uint32
I32 = jnp.int32

D_MODEL = 4096
GRID_W = 64
RMS_EPS = 1e-6
N_MOD = 6
N_HEADS = 32
N_KV_HEADS = 8
HEAD_DIM = D_MODEL // N_HEADS
KV_GROUP = N_HEADS // N_KV_HEADS
ROPE_THETA = 10000.0
WINDOW = 128
M_HEADS = 8
M_V_DIM = D_MODEL // M_HEADS
M_QK_DIM = M_V_DIM // 2
M_CHUNK = 128
GATE_CAP = 15.0
N_EXPERTS = 64
TOP_K = 8
N_GROUPS = 8
TOPK_GROUPS = 4
D_EXPERT = 192
ROUTED_SCALE = 2.5

MOE_TM = 512
MOE_CHUNK = 512
QKV_ROWS = 256
LANES = 128
VMEM_LIMIT = 56 * 1024 * 1024
NEG_INF = float("-inf")


def _cparams(n_axes):
    return pltpu.CompilerParams(dimension_semantics=("arbitrary",) * n_axes,
                                vmem_limit_bytes=VMEM_LIMIT)


def _pick(n, pref):
    t = pref
    while n % t:
        t //= 2
    return t


class _Segments(NamedTuple):
    seq: int
    bsz: int
    n_ctx_rows: int

    @property
    def tile_unit(self):
        return math.gcd(self.seq, self.n_ctx_rows)

    def of_row(self, row):
        return jnp.minimum(row // self.seq, self.bsz)


def _ada_kernel(cond_ref, down_ref, up_ref, b_ref, out_ref, t_ref):
    @pl.when(pl.program_id(0) == 0)
    def _():
        c = cond_ref[...]
        t_ref[...] = jnp.dot(c * jax.nn.sigmoid(c), down_ref[...], precision=lax.Precision.HIGHEST,
                             preferred_element_type=F32)

    out_ref[...] = jnp.dot(t_ref[...], up_ref[...], precision=lax.Precision.HIGHEST,
                           preferred_element_type=F32) + b_ref[...]


def ada_table(cond_pad, down, up, bias):
    r, d = cond_pad.shape
    rank = down.shape[1]
    n = up.shape[1]
    tn = 2048
    out = pl.pallas_call(
        _ada_kernel,
        grid=(n // tn,),
        in_specs=[pl.BlockSpec((r, d), lambda j: (0, 0)),
                  pl.BlockSpec((d, rank), lambda j: (0, 0)),
                  pl.BlockSpec((rank, tn), lambda j: (0, j)),
                  pl.BlockSpec((1, tn), lambda j: (0, j))],
        out_specs=pl.BlockSpec((r, tn), lambda j: (0, j)),
        out_shape=jax.ShapeDtypeStruct((r, n), F32),
        scratch_shapes=[pltpu.VMEM((r, rank), F32)],
        compiler_params=_cparams(1),
        name="ada_table",
    )(cond_pad, down, up, bias.reshape(1, n))
    return out.reshape(r, N_MOD, d)


def _rowwise_kernel(*refs, gate_idx, shift_idx, route, moe_resid, emit_h):
    it = iter(refs)
    h_ref = next(it)
    if moe_resid:
        y = _gather_combine(next(it), next(it), next(it), next(it), next(it), refs[-2], refs[-1])
    elif gate_idx is not None:
        y = next(it)[...].astype(F32)
    gmod_ref = next(it) if gate_idx is not None else None
    g_ref = next(it)
    smod_ref = next(it) if shift_idx is not None else None
    wt_ref, rb_ref = (next(it), next(it)) if route else (None, None)
    h = h_ref[...]
    if gate_idx is not None:
        h = h + gmod_ref[0, gate_idx:gate_idx + 1, :] * y
        if emit_h:
            next(it)[...] = h
    a = h * lax.rsqrt(jnp.mean(h * h, axis=-1, keepdims=True) + RMS_EPS) * g_ref[...]
    if shift_idx is not None:
        a = a * (1.0 + smod_ref[0, shift_idx + 1:shift_idx + 2, :]) + smod_ref[0, shift_idx:shift_idx + 1, :]
    if not route:
        a_ref = next(it)
        a_ref[...] = a.astype(a_ref.dtype)
        return
    packed_ref, ids_ref, rank_ref, w_ref, cnt_ref = (next(it) for _ in range(5))

    @pl.when(pl.program_id(0) == 0)
    def _():
        cnt_ref[...] = jnp.zeros_like(cnt_ref)

    packed_ref[...] = _pack_rows(a)
    ids, ranks, wmat, counts = _route(a, wt_ref[...], rb_ref[...], cnt_ref[:, 0:1])
    ids_ref[...] = ids
    rank_ref[...] = ranks
    w_ref[...] = wmat
    cnt_ref[...] = jnp.broadcast_to(counts, cnt_ref.shape)


def rowwise(h, gain, seg, *, resid=None, shift=None, out_dtype=BF16, rows=None, route=None, emit_h=True):
    n, d = h.shape
    rows = n if rows is None else rows
    moe_resid = resid is not None and isinstance(resid[0], MoeOut)
    tm = _pick(seg.tile_unit, 128 if moe_resid else 256)
    row_spec = pl.BlockSpec((tm, d), lambda i: (i, 0))
    mod_spec = pl.BlockSpec((1, N_MOD, d), lambda i: (seg.of_row(i * tm), 0, 0))
    in_specs, args, scratch = [row_spec], [h], []
    if moe_resid:
        last = rows // tm - 1
        in_specs += [pl.BlockSpec((TOP_K, tm), lambda i: (0, i), memory_space=pltpu.SMEM),
                     pl.BlockSpec((TOP_K, tm), lambda i: (0, jnp.minimum(i + 1, last)), memory_space=pltpu.SMEM),
                     pl.BlockSpec(memory_space=pl.ANY),
                     pl.BlockSpec((tm, LANES), lambda i: (i, 0)),
                     row_spec, mod_spec]
        moe_out = resid[0]
        args += [moe_out.dest, moe_out.dest, moe_out.ys, moe_out.wmat, moe_out.shared, resid[1]]
        scratch = [pltpu.VMEM((2, TOP_K, tm, d // 2), U32), pltpu.SemaphoreType.DMA((2,))]
    elif resid is not None:
        in_specs += [row_spec, mod_spec]
        args += [resid[0], resid[1]]
    in_specs.append(pl.BlockSpec((1, d), lambda i: (0, 0)))
    args.append(gain.reshape(1, d))
    if shift is not None:
        in_specs.append(mod_spec)
        args.append(shift[0])
    if route is not None:
        in_specs += [pl.BlockSpec((N_EXPERTS, d), lambda i: (0, 0)),
                     pl.BlockSpec((N_EXPERTS, 1), lambda i: (0, 0))]
        args += [route[0], route[1].reshape(N_EXPERTS, 1)]
    out_specs, out_shape = [], []
    if resid is not None and emit_h:
        out_specs.append(row_spec)
        out_shape.append(jax.ShapeDtypeStruct((rows, d), F32))
    if route is None:
        out_specs.append(row_spec)
        out_shape.append(jax.ShapeDtypeStruct((rows, d), out_dtype))
    else:
        out_specs += [pl.BlockSpec((tm, d // 2), lambda i: (i, 0)),
                      pl.BlockSpec((TOP_K, tm), lambda i: (0, i)),
                      pl.BlockSpec((TOP_K, tm), lambda i: (0, i)),
                      pl.BlockSpec((tm, LANES), lambda i: (i, 0)),
                      pl.BlockSpec((N_EXPERTS, LANES), lambda i: (0, 0))]
        out_shape += [jax.ShapeDtypeStruct((rows, d // 2), U32),
                      jax.ShapeDtypeStruct((TOP_K, rows), I32),
                      jax.ShapeDtypeStruct((TOP_K, rows), I32),
                      jax.ShapeDtypeStruct((rows, LANES), F32),
                      jax.ShapeDtypeStruct((N_EXPERTS, LANES), F32)]
    return pl.pallas_call(
        functools.partial(_rowwise_kernel, gate_idx=None if resid is None else resid[2],
                          shift_idx=None if shift is None else shift[1], route=route is not None,
                          moe_resid=moe_resid, emit_h=emit_h),
        grid=(rows // tm,),
        in_specs=in_specs, out_specs=out_specs, out_shape=out_shape, scratch_shapes=scratch,
        compiler_params=_cparams(1),
        name="rowwise",
    )(*args)


def _mm_plain_kernel(x_ref, w_ref, o_ref, *, scale_tiles, scale):
    acc = jnp.dot(x_ref[...], w_ref[...], preferred_element_type=F32)
    if scale_tiles:
        acc = acc * jnp.where(pl.program_id(1) < scale_tiles, scale, 1.0)
    o_ref[...] = acc.astype(o_ref.dtype)


def _mm_resid_kernel(x_ref, w_ref, h_ref, mod_ref, o_ref, *, gate_idx):
    acc = jnp.dot(x_ref[...], w_ref[...], preferred_element_type=F32)
    o_ref[...] = h_ref[...] + mod_ref[0, gate_idx:gate_idx + 1, :] * acc


def _mm_qkv_kernel(x_ref, w_ref, cos_ref, sin_ref, qn_ref, kn_ref, o_ref, *, nq_tiles, nk_tiles, qk_norm):
    j = pl.program_id(1)
    is_q = j < nq_tiles
    is_qk = j < nq_tiles + nk_tiles
    post = jnp.where(is_q, HEAD_DIM ** -0.5, 1.0)
    gain = jnp.where(is_q, qn_ref[...], kn_ref[...])
    w = w_ref[...]
    tm, tn = o_ref.shape
    def product(r):
        return jnp.dot(x_ref[r:r + QKV_ROWS, :], w, preferred_element_type=F32)

    nxt = product(0)
    for r in range(0, tm, QKV_ROWS):
        acc = nxt
        if r + QKV_ROWS < tm:
            nxt = product(r + QKV_ROWS)
        cos = cos_ref[r:r + QKV_ROWS, :]
        sin = sin_ref[r:r + QKV_ROWS, :]
        for s in range(0, tn, HEAD_DIM):
            raw = acc[:, s:s + HEAD_DIM]
            xh = raw
            if qk_norm:
                xh = xh * lax.rsqrt(jnp.mean(xh * xh, axis=-1, keepdims=True) + RMS_EPS) * gain
            xh = (xh * cos + pltpu.roll(xh, HEAD_DIM // 2, axis=1) * sin) * post
            o_ref[r:r + QKV_ROWS, s:s + HEAD_DIM] = jnp.where(is_qk, xh, raw).astype(o_ref.dtype)


def matmul(x, w, *, tm=1024, tn=512, out_dtype=BF16, rows=None, cols=None, scale_tiles=0, scale=1.0,
           resid=None, qkv=None):
    m, k = x.shape
    m = m if rows is None else rows
    n = w.shape[1] if cols is None else cols
    tm = _pick(m if resid is None else math.gcd(m, resid[3].tile_unit), tm)
    tn = _pick(n, tn)
    grid = (m // tm, n // tn)
    x_spec = pl.BlockSpec((tm, k), lambda i, j: (i, 0))
    w_spec = pl.BlockSpec((k, tn), lambda i, j: (0, j))
    o_spec = pl.BlockSpec((tm, tn), lambda i, j: (i, j))
    if resid is not None:
        h, mods, gate_idx, seg = resid
        kern = functools.partial(_mm_resid_kernel, gate_idx=gate_idx)
        in_specs = [x_spec, w_spec, o_spec,
                    pl.BlockSpec((1, N_MOD, tn), lambda i, j: (seg.of_row(i * tm), 0, j))]
        args = (x, w, h, mods)
        out_dtype = F32
    elif qkv is not None:
        cos, sin, qn, kn, qk_norm = qkv
        kern = functools.partial(_mm_qkv_kernel, nq_tiles=N_HEADS * HEAD_DIM // tn,
                                 nk_tiles=N_KV_HEADS * HEAD_DIM // tn, qk_norm=qk_norm)
        tab_spec = pl.BlockSpec((tm, HEAD_DIM), lambda i, j: (i, 0))
        vec_spec = pl.BlockSpec((1, HEAD_DIM), lambda i, j: (0, 0))
        in_specs = [x_spec, w_spec, tab_spec, tab_spec, vec_spec, vec_spec]
        args = (x, w, cos, sin, qn.reshape(1, HEAD_DIM), kn.reshape(1, HEAD_DIM))
    else:
        kern = functools.partial(_mm_plain_kernel, scale_tiles=scale_tiles, scale=scale)
        in_specs = [x_spec, w_spec]
        args = (x, w)
    return pl.pallas_call(
        kern, grid=grid, in_specs=in_specs, out_specs=o_spec,
        out_shape=jax.ShapeDtypeStruct((m, n), out_dtype),
        compiler_params=_cparams(2),
        name="matmul",
    )(*args)


def _attn_kernel(*refs, tq, lat_keys, window, has_sink):
    it = iter(refs)
    sink_ref = next(it) if has_sink else None
    q_ref = next(it)
    kc_ref, vc_ref = next(it), next(it)
    kl_ref, vl_ref = (next(it), next(it)) if lat_keys else (None, None)
    o_ref = refs[-1]
    kvh = pl.program_id(1)
    qi = pl.program_id(2)
    nt = (((1,), (1,)), ((), ()))
    kc = kc_ref[...]
    vc = _with_ones(vc_ref[...])
    if lat_keys and window:
        seq = kl_ref.shape[0]
        band = min(tq + 2 * WINDOW, seq)
        start = pl.multiple_of(jnp.clip(qi * tq - WINDOW, 0, seq - band), LANES)
        kl = kl_ref[pl.ds(start, band), :]
        vl = _with_ones(vl_ref[pl.ds(start, band), :])
        q_pos = qi * tq + lax.broadcasted_iota(jnp.int32, (tq, band), 0)
        k_pos = start + lax.broadcasted_iota(jnp.int32, (tq, band), 1)
        in_window = jnp.abs(q_pos - k_pos) <= WINDOW
    elif lat_keys:
        kl = kl_ref[...]
        vl = _with_ones(vl_ref[...])
    def scores(g):
        qg = q_ref[:, g * HEAD_DIM:(g + 1) * HEAD_DIM]
        s_c = lax.dot_general(qg, kc, nt, preferred_element_type=F32)
        if not lat_keys:
            return s_c, None
        s_l = lax.dot_general(qg, kl, nt, preferred_element_type=F32)
        return s_c, jnp.where(in_window, s_l, NEG_INF) if window else s_l

    nxt = scores(0)
    for g in range(KV_GROUP):
        s_c, s_l = nxt
        if g + 1 < KV_GROUP:
            nxt = scores(g + 1)
        m = jnp.max(s_c, axis=-1, keepdims=True)
        if lat_keys:
            m = jnp.maximum(m, jnp.max(s_l, axis=-1, keepdims=True))
        if has_sink:
            sink = sink_ref[kvh * KV_GROUP + g]
            m = jnp.maximum(m, sink)
        acc = jnp.dot(jnp.exp((s_c - m).astype(BF16)), vc, preferred_element_type=F32)
        if lat_keys:
            acc = acc + jnp.dot(jnp.exp((s_l - m).astype(BF16)), vl, preferred_element_type=F32)
        den = acc[:, HEAD_DIM:HEAD_DIM + 1]
        if has_sink:
            den = den + jnp.exp(sink - m)
        o_ref[:, g * HEAD_DIM:(g + 1) * HEAD_DIM] = (acc[:, :HEAD_DIM] / den).astype(o_ref.dtype)


def _with_ones(v):
    return jnp.concatenate([v, jnp.ones_like(v)], axis=1)


def attention(qkv, dims, *, lat_queries, window, sink, out=None, out_rows=None):
    bsz, seq, n_ctx = dims
    n_lat = bsz * seq
    g_cols = KV_GROUP * HEAD_DIM
    k_col0 = N_HEADS * HEAD_DIM // HEAD_DIM
    v_col0 = k_col0 + N_KV_HEADS
    has_sink = sink is not None
    if lat_queries:
        tq = _pick(seq, 512)
        q_tiles = seq // tq
        q_row0 = 0
    else:
        tq = n_ctx
        q_tiles = 1
        q_row0 = n_lat // tq
    aliases = {}
    if out is not None:
        out_rows = out.shape[0]
    ctx_blk0 = n_lat // n_ctx
    in_specs = []
    args = []
    if has_sink:
        in_specs.append(pl.BlockSpec(memory_space=pltpu.SMEM))
        args.append(sink.astype(F32))
    in_specs += [
        pl.BlockSpec((tq, g_cols), lambda b, h, i: (q_row0 + b * q_tiles + i, h)),
        pl.BlockSpec((n_ctx, HEAD_DIM), lambda b, h, i: (ctx_blk0 + b, k_col0 + h)),
        pl.BlockSpec((n_ctx, HEAD_DIM), lambda b, h, i: (ctx_blk0 + b, v_col0 + h)),
    ]
    args += [qkv, qkv, qkv]
    if lat_queries:
        in_specs += [
            pl.BlockSpec((seq, HEAD_DIM), lambda b, h, i: (b, k_col0 + h)),
            pl.BlockSpec((seq, HEAD_DIM), lambda b, h, i: (b, v_col0 + h)),
        ]
        args += [qkv, qkv]
    if out is not None:
        aliases = {len(args): 0}
        in_specs.append(pl.BlockSpec(memory_space=pl.ANY))
        args.append(out)
    return pl.pallas_call(
        functools.partial(_attn_kernel, tq=tq, lat_keys=lat_queries, window=window, has_sink=has_sink),
        grid=(bsz, N_KV_HEADS, q_tiles),
        in_specs=in_specs,
        out_specs=pl.BlockSpec((tq, g_cols), lambda b, h, i: (q_row0 + b * q_tiles + i, h)),
        out_shape=jax.ShapeDtypeStruct((out_rows, N_HEADS * HEAD_DIM), BF16),
        input_output_aliases=aliases,
        compiler_params=_cparams(3),
        name="attention",
    )(*args)


def _mlstm_gate_kernel(x_ref, w_ref, b_ref, o_ref):
    g = jnp.dot(x_ref[...], w_ref[...], preferred_element_type=F32) + b_ref[...]
    g = GATE_CAP * jnp.tanh(g / GATE_CAP)
    log_f = jnp.minimum(g, 0.0) - jnp.log(1.0 + jnp.exp(-jnp.abs(g)))
    col = lax.broadcasted_iota(jnp.int32, g.shape, 1)
    is_f = (col // M_HEADS) % 2 == 1
    o_ref[...] = jnp.transpose(jnp.where(is_f, log_f, g))


def mlstm_gates(a, w_gate_pad, gate_b_pad):
    n, k = a.shape
    tm = _pick(n, 1024)
    return pl.pallas_call(
        _mlstm_gate_kernel,
        grid=(n // tm,),
        in_specs=[pl.BlockSpec((tm, k), lambda i: (i, 0)),
                  pl.BlockSpec((k, LANES), lambda i: (0, 0)),
                  pl.BlockSpec((1, LANES), lambda i: (0, 0))],
        out_specs=pl.BlockSpec((LANES, tm), lambda i: (0, i)),
        out_shape=jax.ShapeDtypeStruct((LANES, n), F32),
        compiler_params=_cparams(1),
        name="mlstm_gates",
    )(a, w_gate_pad, gate_b_pad)


def _mlstm_kernel(*refs):
    ct_ref, n_ref, m_ref = refs[-3:]
    c = pl.program_id(1)
    L = M_CHUNK

    @pl.when(c == 0)
    def _():
        ct_ref[...] = jnp.zeros_like(ct_ref)
        n_ref[...] = jnp.zeros_like(n_ref)
        m_ref[...] = jnp.zeros_like(m_ref)

    nt = (((1,), (1,)), ((), ()))
    tn = (((0,), (0,)), ((), ()))
    for d in range(2):
        q_ref, k_ref, v_ref, gi_ref, gf_ref = refs[5 * d:5 * d + 5]
        o_ref = refs[10 + d]
        diff = lax.broadcasted_iota(jnp.int32, (L, L), 0) - lax.broadcasted_iota(jnp.int32, (L, L), 1)
        if d == 1:
            diff = -diff
        mask = diff >= 0
        mask_t = diff <= 0
        eye = diff == 0
        def early(h):
            st = d * M_HEADS + h
            qc = q_ref[:, h * M_QK_DIM:(h + 1) * M_QK_DIM]
            kc = k_ref[:, h * M_QK_DIM:(h + 1) * M_QK_DIM]
            i_row = gi_ref[h:h + 1, :]
            f_row = gf_ref[h:h + 1, :]
            f_col = jnp.sum(jnp.where(eye, f_row, 0.0), axis=1, keepdims=True)
            i_col = jnp.sum(jnp.where(eye, i_row, 0.0), axis=1, keepdims=True)
            b_col = jnp.sum(jnp.where(mask, f_row, 0.0), axis=1, keepdims=True)
            b_row = jnp.sum(jnp.where(mask_t, f_col, 0.0), axis=0, keepdims=True)
            log_intra = jnp.where(mask, b_col - b_row + i_row, NEG_INF)
            qk = lax.dot_general(qc, kc, nt, preferred_element_type=F32)
            inter = jnp.dot(qc, ct_ref[st].astype(BF16), preferred_element_type=F32)
            return qc, kc, f_row, i_col, b_col, log_intra, qk, inter

        nxt = early(0)
        for h in range(M_HEADS):
            st = d * M_HEADS + h
            qc, kc, f_row, i_col, b_col, log_intra, qk, inter = nxt
            if h + 1 < M_HEADS:
                nxt = early(h + 1)
            vc = v_ref[:, h * M_V_DIM:(h + 1) * M_V_DIM]
            m_prev = m_ref[st, :, 0:1]
            n_prev = n_ref[st]
            log_inter = b_col + m_prev
            m_t = jnp.maximum(log_inter, jnp.max(log_intra, axis=1, keepdims=True))
            w_inter = jnp.exp(log_inter - m_t)
            s_qk = qk * jnp.exp(log_intra - m_t)
            num = jnp.dot(s_qk.astype(BF16), vc, preferred_element_type=F32) + w_inter * inter
            den = (jnp.sum(s_qk, axis=1, keepdims=True)
                   + w_inter * jnp.sum(qc.astype(F32) * n_prev, axis=1, keepdims=True))
            h_out = num / jnp.maximum(jnp.abs(den), jnp.exp(-m_t))
            o_ref[:, h * M_V_DIM:(h + 1) * M_V_DIM] = h_out.astype(o_ref.dtype)
            b_tot = jnp.sum(f_row, axis=1, keepdims=True)
            log_w = b_tot - b_col + i_col
            m_new = jnp.maximum(b_tot + m_prev, jnp.max(log_w, axis=0, keepdims=True))
            decay = jnp.exp(b_tot + m_prev - m_new)
            w_k = jnp.exp(log_w - m_new)
            vw = (vc.astype(F32) * w_k).astype(BF16)
            ct_ref[st] = decay * ct_ref[st] + lax.dot_general(kc, vw, tn, preferred_element_type=F32)
            n_ref[st] = decay * n_prev + jnp.sum(kc.astype(F32) * w_k, axis=0, keepdims=True)
            m_ref[st] = jnp.broadcast_to(m_new, (1, LANES))


def mlstm_scan(proj, gates_t, dims):
    bsz, seq, n_ctx = dims
    nt_rows = proj.shape[0]
    L = M_CHUNK
    ncc, nlc = n_ctx // L, seq // L
    lat_blocks = bsz * nlc
    qk_w = M_HEADS * M_QK_DIM

    def row_blk(b, d, c):
        cc = c if d == 0 else ncc - 1 - c
        lc = c - ncc if d == 0 else nlc - 1 - (c - ncc)
        return jnp.where(c < ncc, lat_blocks + b * ncc + cc, b * nlc + lc)

    def dir_specs(d):
        return [
            pl.BlockSpec((L, qk_w), lambda b, c: (row_blk(b, d, c), 0)),
            pl.BlockSpec((L, qk_w), lambda b, c: (row_blk(b, d, c), 1)),
            pl.BlockSpec((L, D_MODEL), lambda b, c: (row_blk(b, d, c), 1)),
            pl.BlockSpec((M_HEADS, L), lambda b, c: (2 * d, row_blk(b, d, c))),
            pl.BlockSpec((M_HEADS, L), lambda b, c: (2 * d + 1, row_blk(b, d, c))),
        ]

    return pl.pallas_call(
        _mlstm_kernel,
        grid=(bsz, ncc + nlc),
        in_specs=dir_specs(0) + dir_specs(1),
        out_specs=[pl.BlockSpec((L, D_MODEL), lambda b, c: (row_blk(b, 0, c), 0)),
                   pl.BlockSpec((L, D_MODEL), lambda b, c: (row_blk(b, 1, c), 0))],
        out_shape=[jax.ShapeDtypeStruct((nt_rows, D_MODEL), BF16)] * 2,
        scratch_shapes=[pltpu.VMEM((2 * M_HEADS, M_QK_DIM, M_V_DIM), F32),
                        pltpu.VMEM((2 * M_HEADS, 1, M_QK_DIM), F32),
                        pltpu.VMEM((2 * M_HEADS, 1, LANES), F32)],
        compiler_params=_cparams(2),
        name="mlstm_scan",
    )(*([proj, proj, proj, gates_t, gates_t] * 2))


def _mlstm_finish_kernel(hf_ref, hb_ref, og_ref, g_ref, o_ref):
    for h in range(M_HEADS):
        sl = slice(h * M_V_DIM, (h + 1) * M_V_DIM)
        x = hf_ref[:, sl].astype(F32) + hb_ref[:, sl].astype(F32)
        y = x * lax.rsqrt(jnp.mean(x * x, axis=-1, keepdims=True) + RMS_EPS) * g_ref[:, sl]
        o_ref[:, sl] = (y * jax.nn.sigmoid(og_ref[:, sl].astype(F32))).astype(o_ref.dtype)


def mlstm_finish(h_fwd, h_bwd, proj, head_norm):
    n, d = h_fwd.shape
    tm = _pick(n, 256)
    row_spec = pl.BlockSpec((tm, d), lambda i: (i, 0))
    return pl.pallas_call(
        _mlstm_finish_kernel,
        grid=(n // tm,),
        in_specs=[row_spec, row_spec,
                  pl.BlockSpec((tm, d), lambda i: (i, 2)),
                  pl.BlockSpec((1, d), lambda i: (0, 0))],
        out_specs=row_spec,
        out_shape=jax.ShapeDtypeStruct((n, d), BF16),
        compiler_params=_cparams(1),
        name="mlstm_finish",
    )(h_fwd, h_bwd, proj, head_norm.reshape(1, d))


def _pack_rows(x):
    half = x.shape[1] // 2
    return _pack_pair(x[:, :half], x[:, half:])


def _pack_pair(lo, hi):
    lo = lax.bitcast_convert_type(lo.astype(BF16).astype(F32), U32)
    hi = lax.bitcast_convert_type(hi.astype(BF16).astype(F32), U32)
    return (hi & jnp.uint32(0xFFFF0000)) | (lo >> 16)


def _unpack_rows(p):
    lo = lax.bitcast_convert_type(p << 16, F32)
    hi = lax.bitcast_convert_type(p & jnp.uint32(0xFFFF0000), F32)
    return lo, hi


def _route(a, w_t, bias, counts):
    tm = a.shape[0]
    per = N_EXPERTS // N_GROUPS
    nt = (((1,), (1,)), ((), ()))
    w_hi = w_t.astype(BF16)
    w_lo = (w_t - w_hi.astype(F32)).astype(BF16)
    a_hi = a.astype(BF16)
    a_lo = (a - a_hi.astype(F32)).astype(BF16)
    logits = (lax.dot_general(w_hi, a_hi, nt, preferred_element_type=F32)
              + lax.dot_general(w_lo, a_hi, nt, preferred_element_type=F32)
              + lax.dot_general(w_hi, a_lo, nt, preferred_element_type=F32))
    scores = jax.nn.sigmoid(logits).reshape(N_GROUPS, per, tm)
    biased = scores + bias.reshape(N_GROUPS, per, 1)
    e_iota = lax.broadcasted_iota(jnp.int32, (N_GROUPS, per, tm), 1).astype(F32)
    g_iota = lax.broadcasted_iota(jnp.int32, (N_GROUPS, 1, tm), 0).astype(F32)
    lin_iota = lax.broadcasted_iota(jnp.int32, (N_GROUPS, per, tm), 0).astype(F32) * per + e_iota
    m1 = jnp.max(biased, axis=1, keepdims=True)
    i1 = jnp.min(jnp.where(biased == m1, e_iota, float(per)), axis=1, keepdims=True)
    m2 = jnp.max(jnp.where(e_iota == i1, NEG_INF, biased), axis=1, keepdims=True)
    gscore = m1 + m2
    gsel = jnp.zeros(gscore.shape, F32)
    for _ in range(TOPK_GROUPS):
        cur = jnp.where(gsel > 0.0, NEG_INF, gscore)
        gm = jnp.max(cur, axis=0, keepdims=True)
        gi = jnp.min(jnp.where(cur == gm, g_iota, float(N_GROUPS)), axis=0, keepdims=True)
        gsel = jnp.where(g_iota == gi, 1.0, gsel)
    cand = jnp.where(gsel > 0.0, biased, NEG_INF)
    sel = jnp.zeros(cand.shape, F32)
    picks = []
    for _ in range(TOP_K):
        cur = jnp.where(sel > 0.0, NEG_INF, cand)
        em = jnp.max(jnp.max(cur, axis=1, keepdims=True), axis=0, keepdims=True)
        hit = jnp.where(cur == em, lin_iota, float(N_EXPERTS))
        ei = jnp.min(jnp.min(hit, axis=1, keepdims=True), axis=0, keepdims=True)
        sel = jnp.where(lin_iota == ei, 1.0, sel)
        picks.append(ei)
    sel2 = sel.reshape(N_EXPERTS, tm)
    before = lax.broadcasted_iota(jnp.int32, (tm, tm), 0) < lax.broadcasted_iota(jnp.int32, (tm, tm), 1)
    rank = jnp.dot(sel2.astype(BF16), jnp.where(before, 1.0, 0.0).astype(BF16),
                   preferred_element_type=F32) + counts
    new_counts = counts + jnp.sum(sel2, axis=1, keepdims=True)
    rank3 = rank.reshape(N_GROUPS, per, tm)
    wsum = jnp.sum(jnp.sum(jnp.where(sel > 0.0, scores, 0.0), axis=1, keepdims=True), axis=0, keepdims=True)

    def pick(ei, table):
        v = jnp.where(lin_iota == ei, table, 0.0)
        return jnp.sum(jnp.sum(v, axis=1, keepdims=True), axis=0, keepdims=True).reshape(1, tm)

    ids = jnp.concatenate([ei.reshape(1, tm) for ei in picks], axis=0).astype(I32)
    ranks = jnp.concatenate([pick(ei, rank3) for ei in picks], axis=0).astype(I32)
    w = jnp.concatenate([pick(ei, scores) for ei in picks], axis=0) / wsum.reshape(1, tm) * ROUTED_SCALE
    wmat = jnp.transpose(jnp.concatenate([w, jnp.zeros((LANES - TOP_K, tm), F32)], axis=0))
    return ids, ranks, wmat, new_counts


def _dest_kernel(starts_ref, ids_ref, rank_ref, dest_ref):
    ids = ids_ref[...]
    acc = rank_ref[...]
    for e in range(N_EXPERTS):
        acc = acc + jnp.where(ids == e, starts_ref[e], 0)
    dest_ref[...] = acc


def dest_rows(starts, ids, ranks):
    return pl.pallas_call(
        _dest_kernel,
        in_specs=[pl.BlockSpec(memory_space=pltpu.SMEM), pl.BlockSpec(memory_space=pltpu.VMEM),
                  pl.BlockSpec(memory_space=pltpu.VMEM)],
        out_specs=pl.BlockSpec(memory_space=pltpu.VMEM),
        out_shape=jax.ShapeDtypeStruct(ids.shape, I32),
        compiler_params=pltpu.CompilerParams(vmem_limit_bytes=VMEM_LIMIT),
        name="dest_rows",
    )(starts, ids, ranks)


def _dispatch_kernel(starts_ref, ends_ref, dest_ref, f_ref, gu_ref, dn_ref, xs_ref, sh_ref,
                     zero_ref, gu_bf, dn_bf, sem):
    tm = f_ref.shape[0]

    def zero_copy(e):
        row0 = pl.multiple_of(ends_ref[e] - MOE_TM, MOE_TM)
        return pltpu.make_async_copy(zero_ref, xs_ref.at[pl.ds(row0, MOE_TM)], sem)

    @pl.when(pl.program_id(0) == 0)
    def _():
        gu_bf[...] = gu_ref[...].astype(BF16)
        dn_bf[...] = dn_ref[...].astype(BF16)
        zero_ref[...] = jnp.zeros_like(zero_ref)
        for e in range(N_EXPERTS):
            @pl.when(ends_ref[e] > starts_ref[e])
            def _():
                zero_copy(e).start()
        for e in range(N_EXPERTS):
            @pl.when(ends_ref[e] > starts_ref[e])
            def _():
                zero_copy(e).wait()

    def row_copy(t, d):
        return pltpu.make_async_copy(f_ref.at[pl.ds(t, 1)], xs_ref.at[pl.ds(d, 1)], sem)

    def issue(t, c):
        for k in range(TOP_K):
            row_copy(t, dest_ref[k, t]).start(priority=1)
        return c

    def drain(t, c):
        for k in range(TOP_K):
            row_copy(0, 0).wait()
        return c

    lax.fori_loop(0, tm, issue, 0)
    _swiglu_expert(f_ref, gu_bf, dn_bf, sh_ref, pack_out=False)
    lax.fori_loop(0, tm, drain, 0)


def dispatch(starts, ends, dest, packed, n_rows, shared_gu, shared_down):
    n, half = packed.shape
    tm = _pick(n, 256)

    def whole(shape):
        return pl.BlockSpec(shape, lambda i, s, e: (0,) * len(shape))

    return pl.pallas_call(
        _dispatch_kernel,
        grid_spec=pltpu.PrefetchScalarGridSpec(
            num_scalar_prefetch=2, grid=(n // tm,),
            in_specs=[pl.BlockSpec((TOP_K, tm), lambda i, s, e: (0, i), memory_space=pltpu.SMEM),
                      pl.BlockSpec((tm, half), lambda i, s, e: (i, 0)),
                      whole(shared_gu.shape), whole(shared_down.shape)],
            out_specs=[pl.BlockSpec(memory_space=pl.ANY),
                       pl.BlockSpec((tm, 2 * half), lambda i, s, e: (i, 0))],
            scratch_shapes=[pltpu.VMEM((MOE_TM, half), U32), pltpu.VMEM(shared_gu.shape, BF16),
                            pltpu.VMEM(shared_down.shape, BF16), pltpu.SemaphoreType.DMA(())]),
        out_shape=[jax.ShapeDtypeStruct((n_rows, half), U32), jax.ShapeDtypeStruct((n, 2 * half), BF16)],
        compiler_params=_cparams(1),
        name="dispatch",
    )(starts, ends, dest, packed, shared_gu, shared_down)


def _swiglu_expert(x_ref, gu_bf, dn_bf, o_ref, pack_out):
    half = x_ref.shape[1]
    chunks = range(0, half, MOE_CHUNK)

    def unpack(c):
        lo, hi = _unpack_rows(x_ref[:, c:c + MOE_CHUNK])
        return lo.astype(BF16), hi.astype(BF16)

    def down(c):
        return (jnp.dot(act, dn_bf[:, c:c + MOE_CHUNK], preferred_element_type=F32),
                jnp.dot(act, dn_bf[:, half + c:half + c + MOE_CHUNK], preferred_element_type=F32))

    gu = None
    nxt = unpack(0)
    for c in chunks:
        lo, hi = nxt
        if c + MOE_CHUNK < half:
            nxt = unpack(c + MOE_CHUNK)
        part = (jnp.dot(lo, gu_bf[c:c + MOE_CHUNK, :], preferred_element_type=F32)
                + jnp.dot(hi, gu_bf[half + c:half + c + MOE_CHUNK, :], preferred_element_type=F32))
        gu = part if gu is None else gu + part
    g = gu[:, :D_EXPERT]
    act = (g * jax.nn.sigmoid(g) * gu[:, D_EXPERT:]).astype(BF16)
    nxt = down(0)
    for c in chunks:
        y_lo, y_hi = nxt
        if c + MOE_CHUNK < half:
            nxt = down(c + MOE_CHUNK)
        if pack_out:
            o_ref[:, c:c + MOE_CHUNK] = _pack_pair(y_lo, y_hi)
        else:
            o_ref[:, c:c + MOE_CHUNK] = y_lo.astype(o_ref.dtype)
            o_ref[:, half + c:half + c + MOE_CHUNK] = y_hi.astype(o_ref.dtype)


def _grouped_kernel(te_ref, nt_ref, x_ref, gu_ref, dn_ref, o_ref, gu_bf, dn_bf):
    j = pl.program_id(0)

    @pl.when(j < nt_ref[0])
    def _():
        @pl.when(jnp.logical_or(j == 0, te_ref[j] != te_ref[jnp.maximum(j - 1, 0)]))
        def _():
            gu_bf[...] = gu_ref[0].astype(BF16)
            dn_bf[...] = dn_ref[0].astype(BF16)

        _swiglu_expert(x_ref, gu_bf, dn_bf, o_ref, pack_out=True)


def grouped_experts(tile_expert, n_tiles, xs, exp_gu, exp_down):
    rows, half = xs.shape
    d = 2 * half

    def tile(j, nt):
        return jnp.minimum(j, nt[0] - 1)

    return pl.pallas_call(
        _grouped_kernel,
        grid_spec=pltpu.PrefetchScalarGridSpec(
            num_scalar_prefetch=2, grid=(rows // MOE_TM,),
            in_specs=[pl.BlockSpec((MOE_TM, half), lambda j, te, nt: (tile(j, nt), 0)),
                      pl.BlockSpec((1, d, 2 * D_EXPERT), lambda j, te, nt: (te[tile(j, nt)], 0, 0)),
                      pl.BlockSpec((1, D_EXPERT, d), lambda j, te, nt: (te[tile(j, nt)], 0, 0))],
            out_specs=pl.BlockSpec((MOE_TM, half), lambda j, te, nt: (tile(j, nt), 0)),
            scratch_shapes=[pltpu.VMEM((d, 2 * D_EXPERT), BF16), pltpu.VMEM((D_EXPERT, d), BF16)]),
        out_shape=jax.ShapeDtypeStruct((rows, half), U32),
        compiler_params=_cparams(1),
        name="grouped_experts",
    )(tile_expert, n_tiles, xs, exp_gu, exp_down)


def _gather_combine(dest_ref, next_dest_ref, ys_ref, w_ref, sh_ref, buf_ref, sems):
    tm, d = sh_ref.shape
    half = d // 2
    step = pl.program_id(0)
    slot = step % 2

    def row_copy(s, k, t, src):
        return pltpu.make_async_copy(ys_ref.at[pl.ds(src, 1)], buf_ref.at[s, k, pl.ds(t, 1)], sems.at[s])

    def request(idx_ref, s):
        def body(t, c):
            for k in range(TOP_K):
                row_copy(s, k, t, idx_ref[k, t]).start(priority=1)
            return c
        lax.fori_loop(0, tm, body, 0)

    @pl.when(step == 0)
    def _():
        request(dest_ref, 0)

    @pl.when(step + 1 < pl.num_programs(0))
    def _():
        request(next_dest_ref, 1 - slot)

    def drain(t, c):
        for k in range(TOP_K):
            row_copy(slot, 0, 0, 0).wait()
        return c

    lax.fori_loop(0, tm, drain, 0)
    w = w_ref[...]
    y_lo = sh_ref[:, :half].astype(F32)
    y_hi = sh_ref[:, half:].astype(F32)
    for k in range(TOP_K):
        lo, hi = _unpack_rows(buf_ref[slot, k])
        y_lo = y_lo + w[:, k:k + 1] * lo
        y_hi = y_hi + w[:, k:k + 1] * hi
    return jnp.concatenate([y_lo, y_hi], axis=1)


class MoeOut(NamedTuple):
    dest: jax.Array
    ys: jax.Array
    wmat: jax.Array
    shared: jax.Array


def moe_experts(packed, ids, ranks, wmat, counts, moe):
    _, _, exp_gu, exp_down, shared_gu, shared_down = moe
    n = packed.shape[0]
    cnt = counts[:, 0].astype(I32)
    padded = (cnt + MOE_TM - 1) // MOE_TM * MOE_TM
    ends = jnp.cumsum(padded)
    starts = ends - padded
    n_max = n * TOP_K // MOE_TM + N_EXPERTS
    n_tiles = (ends[-1] // MOE_TM).reshape(1)
    tile_row0 = jnp.arange(n_max, dtype=I32) * MOE_TM
    tile_expert = jnp.minimum(jnp.sum(ends[None, :] <= tile_row0[:, None], axis=1), N_EXPERTS - 1).astype(I32)
    dest = dest_rows(starts, ids, ranks)
    xs, shared = dispatch(starts, ends, dest, packed, n_max * MOE_TM, shared_gu, shared_down)
    ys = grouped_experts(tile_expert, n_tiles, xs, exp_gu, exp_down)
    return MoeOut(dest, ys, wmat, shared)


def _rope_tables(seq, n_rows):
    pos = jnp.arange(seq)
    row = (pos // GRID_W).astype(F32)
    col = (pos % GRID_W).astype(F32)
    n_freq = HEAD_DIM // 4
    inv_freq = ROPE_THETA ** (-jnp.arange(n_freq, dtype=F32) / n_freq)
    ang = jnp.concatenate([row[:, None] * inv_freq, col[:, None] * inv_freq], axis=-1)
    cos, sin = jnp.cos(ang), jnp.sin(ang)
    cos_full = jnp.concatenate([cos, cos], axis=-1)
    sin_full = jnp.concatenate([-sin, sin], axis=-1)
    return cos_full, sin_full


def _trunk(x, c, ctx, c_ctx, layers, final_norm):
    bsz, seq, d = x.shape
    n_ctx = ctx.shape[1]
    dims = (bsz, seq, n_ctx)
    n_lat = bsz * seq
    n_all = n_lat + bsz * n_ctx
    depth = len(layers)

    seg = _Segments(seq, bsz, bsz * n_ctx)

    h = jnp.concatenate([x.reshape(n_lat, d), ctx.reshape(bsz * n_ctx, d)], axis=0)
    mod_rows = -(-(bsz + 1) // 8) * 8
    cond = jnp.zeros((mod_rows, d), F32).at[:bsz].set(c).at[bsz].set(c_ctx)

    cos1, sin1 = _rope_tables(seq, n_all)
    cos_t = jnp.concatenate([jnp.tile(cos1, (bsz, 1)), jnp.ones((bsz * n_ctx, HEAD_DIM), F32)], axis=0)
    sin_t = jnp.concatenate([jnp.tile(sin1, (bsz, 1)), jnp.zeros((bsz * n_ctx, HEAD_DIM), F32)], axis=0)
    ones_hd = jnp.ones((HEAD_DIM,), F32)

    tables = [ada_table(cond, *layer[1]) for layer in layers]
    a, = rowwise(h, layers[0][2], seg, shift=(tables[0], 0))
    for li, (kind, ada, norm1, mixer, norm2, moe) in enumerate(layers):
        need_ctx = li < depth - 1
        rows_out = n_all if need_ctx else n_lat
        mods = tables[li]
        resid = (h, mods, 2, seg)
        if kind == "mlstm":
            w_in, gate_b, head_norm, w_o = mixer
            n_main = 2 * M_HEADS * M_QK_DIM + M_HEADS * M_V_DIM + D_MODEL
            proj = matmul(a, w_in.astype(BF16), cols=n_main, scale_tiles=M_HEADS * M_QK_DIM // 512,
                          scale=M_QK_DIM ** -0.5)
            n_gate = 4 * M_HEADS
            w_gate = jnp.zeros((d, LANES), BF16).at[:, :n_gate].set(w_in[:, n_main:].astype(BF16))
            gate_b_pad = jnp.zeros((1, LANES), F32).at[0, :n_gate].set(gate_b.astype(F32))
            gates_t = mlstm_gates(a, w_gate, gate_b_pad)
            h_fwd, h_bwd = mlstm_scan(proj, gates_t, dims)
            mixed = mlstm_finish(h_fwd, h_bwd, proj, head_norm)
            h = matmul(mixed, w_o.astype(BF16), resid=resid, rows=rows_out)
        else:
            if kind == "global":
                w_qkv, q_norm, k_norm, w_o = mixer
                qkv = matmul(a, w_qkv.astype(BF16), qkv=(cos_t, sin_t, q_norm, k_norm, True))
                sink = None
            else:
                w_qkv, sink, w_o = mixer
                qkv = matmul(a, w_qkv.astype(BF16), qkv=(cos_t, sin_t, ones_hd, ones_hd, False))
            mixed = attention(qkv, dims, lat_queries=True, window=kind == "swa", sink=sink, out_rows=rows_out)
            if need_ctx:
                mixed = attention(qkv, dims, lat_queries=False, window=False, sink=sink, out=mixed)
            h = matmul(mixed, w_o.astype(BF16), resid=resid, rows=rows_out)
        router_w, router_b, exp_gu, exp_down, shared_gu, shared_down = moe
        routed = rowwise(h, norm2, seg, shift=(mods, 3), rows=rows_out, route=(router_w.T, router_b))
        y = moe_experts(*routed, moe)
        if li + 1 < depth:
            h, a = rowwise(h, layers[li + 1][2], seg, resid=(y, mods, 5), shift=(tables[li + 1], 0), rows=rows_out)
        else:
            out, = rowwise(h, final_norm, seg, resid=(y, mods, 5), rows=rows_out, out_dtype=F32, emit_h=False)
    return out.reshape(bsz, seq, d)


def kernel(x, c, ctx, c_ctx, l0_ada_down, l0_ada_up, l0_ada_b, l0_norm1, l0_attn_qkv, l0_q_norm, l0_k_norm, l0_attn_o, l0_norm2, l0_router_w, l0_router_b, l0_exp_gu, l0_exp_down, l0_shared_gu, l0_shared_down, l1_ada_down, l1_ada_up, l1_ada_b, l1_norm1, l1_mlstm_in, l1_mlstm_gate_b, l1_mlstm_head_norm, l1_mlstm_o, l1_norm2, l1_router_w, l1_router_b, l1_exp_gu, l1_exp_down, l1_shared_gu, l1_shared_down, l2_ada_down, l2_ada_up, l2_ada_b, l2_norm1, l2_swa_qkv, l2_swa_sink, l2_swa_o, l2_norm2, l2_router_w, l2_router_b, l2_exp_gu, l2_exp_down, l2_shared_gu, l2_shared_down, l3_ada_down, l3_ada_up, l3_ada_b, l3_norm1, l3_attn_qkv, l3_q_norm, l3_k_norm, l3_attn_o, l3_norm2, l3_router_w, l3_router_b, l3_exp_gu, l3_exp_down, l3_shared_gu, l3_shared_down, final_norm):
    layers = [
        ("global", (l0_ada_down, l0_ada_up, l0_ada_b), l0_norm1, (l0_attn_qkv, l0_q_norm, l0_k_norm, l0_attn_o), l0_norm2,
         (l0_router_w, l0_router_b, l0_exp_gu, l0_exp_down, l0_shared_gu, l0_shared_down)),
        ("mlstm", (l1_ada_down, l1_ada_up, l1_ada_b), l1_norm1, (l1_mlstm_in, l1_mlstm_gate_b, l1_mlstm_head_norm, l1_mlstm_o), l1_norm2,
         (l1_router_w, l1_router_b, l1_exp_gu, l1_exp_down, l1_shared_gu, l1_shared_down)),
        ("swa", (l2_ada_down, l2_ada_up, l2_ada_b), l2_norm1, (l2_swa_qkv, l2_swa_sink, l2_swa_o), l2_norm2,
         (l2_router_w, l2_router_b, l2_exp_gu, l2_exp_down, l2_shared_gu, l2_shared_down)),
        ("global", (l3_ada_down, l3_ada_up, l3_ada_b), l3_norm1, (l3_attn_qkv, l3_q_norm, l3_k_norm, l3_attn_o), l3_norm2,
         (l3_router_w, l3_router_b, l3_exp_gu, l3_exp_down, l3_shared_gu, l3_shared_down)),
    ]
    return _trunk(x, c, ctx, c_ctx, layers, final_norm)
```
